```python
import math, functools
import jax, jax.numpy as jnp
from jax import lax
import numpy as np

D_MODEL = 2048
BATCH = 2
SEQ = 4096
DEPTH = 1
DEC_BATCH = 8
DEC_SEQ = 4
PAST_LEN = 16384
PAGE_SIZE = 128

MIX_W = D_MODEL
ATT_W = MIX_W // 2
SSM_W = MIX_W - ATT_W
V_HEAD = 128
N_HEADS = ATT_W // V_HEAD
QK_DIM = V_HEAD // 2
SSM_GROUP = 16
N_SSM_GROUPS = SSM_W // SSM_GROUP
SSM_STATE = 64
D_IN = 3 * ATT_W + SSM_W
D_FF = ((8 * D_MODEL // 3 + 127) // 128) * 128
CONV_W = 3
PLE_DIM = 256
N_BUCKETS = 32
MAX_DISTANCE = 128
Q_BLOCK = 128
NORM_EPS = 1e-6
SUBLN_EPS = 1e-5

kernel_name = "hymba_diffattn_s5_convffn_step"


def rmsnorm(x, g, eps=NORM_EPS):
    xf = x.astype(jnp.float32)
    r = xf * lax.rsqrt(jnp.mean(xf * xf, axis=-1, keepdims=True) + eps)
    return (r * g.astype(jnp.float32)).astype(x.dtype)


def t5_bucket(dist):
    max_exact = N_BUCKETS // 2
    n = jnp.maximum(dist, 0)
    nf = jnp.maximum(n, 1).astype(jnp.float32)
    large = max_exact + (jnp.log(nf / max_exact) / math.log(MAX_DISTANCE / max_exact)
                         * (N_BUCKETS - max_exact)).astype(jnp.int32)
    large = jnp.minimum(large, N_BUCKETS - 1)
    return jnp.where(n < max_exact, n, large)


def rel_bias(table, qpos, kpos):
    b = table[t5_bucket(qpos[:, None] - kpos[None, :])]
    return jnp.transpose(b, (2, 0, 1)).astype(jnp.float32)


def diff_weights(s, lam):
    p = jax.nn.softmax(s, axis=-1)
    return p[:, :, 0] - lam * p[:, :, 1]


def prompt_diff_attention(q, k, v, lam, table):
    B, T = q.shape[0], q.shape[1]
    scale = QK_DIM ** -0.5
    kpos = jnp.arange(T)

    def one_block(qb):
        start = qb * Q_BLOCK
        q_blk = lax.dynamic_slice_in_dim(q, start, Q_BLOCK, axis=1)
        qpos = start + jnp.arange(Q_BLOCK)
        s = jnp.einsum('bqhjd,bkhjd->bhjqk', q_blk, k).astype(jnp.float32) * scale
        s = s + rel_bias(table, qpos, kpos)[None, :, None]
        s = jnp.where(kpos[None, :] <= qpos[:, None], s, -jnp.inf)
        w = diff_weights(s, lam)
        return jnp.einsum('bhqk,bkhe->bqhe', w.astype(v.dtype), v)

    out = lax.map(one_block, jnp.arange(T // Q_BLOCK))
    return jnp.moveaxis(out, 0, 1).reshape(B, T, N_HEADS, V_HEAD)


def sample_diff_attention(q, k, v, lam, table, k_pool, v_pool, page_table):
    DB, S = q.shape[0], q.shape[1]
    scale = QK_DIM ** -0.5
    k_past = k_pool[page_table].reshape(DB, -1, N_HEADS, 2, QK_DIM)
    v_past = v_pool[page_table].reshape(DB, -1, N_HEADS, V_HEAD)
    P = k_past.shape[1]
    qpos = P + jnp.arange(S)
    s_past = jnp.einsum('bqhjd,bkhjd->bhjqk', q, k_past).astype(jnp.float32) * scale
    s_past = s_past + rel_bias(table, qpos, jnp.arange(P))[None, :, None]
    s_new = jnp.einsum('bqhjd,bkhjd->bhjqk', q, k).astype(jnp.float32) * scale
    s_new = s_new + rel_bias(table, qpos, qpos)[None, :, None]
    s_new = jnp.where(qpos[None, :] <= qpos[:, None], s_new, -jnp.inf)
    w = diff_weights(jnp.concatenate([s_past, s_new], axis=-1), lam).astype(v.dtype)
    return (jnp.einsum('bhqk,bkhe->bqhe', w[..., :P], v_past)
            + jnp.einsum('bhqk,bkhe->bqhe', w[..., P:], v))


def complex_affine_combine(e1, e2):
    a1r, a1i, b1r, b1i = e1
    a2r, a2i, b2r, b2i = e2
    return (a2r * a1r - a2i * a1i,
            a2r * a1i + a2i * a1r,
            a2r * b1r - a2i * b1i + b2r,
            a2r * b1i + a2i * b1r + b2i)


def s5_scan(u, h0_re, h0_im, a_re, a_im, b_re, b_im, c_re, c_im, d, log_dt):
    f32 = jnp.float32
    u = u.astype(f32)
    a_re, a_im = a_re.astype(f32), a_im.astype(f32)
    b_re, b_im = b_re.astype(f32), b_im.astype(f32)
    dt = jnp.exp(log_dt.astype(f32))[:, None]
    decay = jnp.exp(a_re * dt)
    ab_re = decay * jnp.cos(a_im * dt)
    ab_im = decay * jnp.sin(a_im * dt)
    den = a_re * a_re + a_im * a_im
    num_re = ab_re - 1.0
    coef_re = (num_re * a_re + ab_im * a_im) / den
    coef_im = (ab_im * a_re - num_re * a_im) / den
    bb_re = coef_re[..., None] * b_re - coef_im[..., None] * b_im
    bb_im = coef_re[..., None] * b_im + coef_im[..., None] * b_re
    bu_re = jnp.einsum('btgc,gnc->btgn', u, bb_re)
    bu_im = jnp.einsum('btgc,gnc->btgn', u, bb_im)
    shape = bu_re.shape
    elems = (jnp.broadcast_to(ab_re, shape), jnp.broadcast_to(ab_im, shape), bu_re, bu_im)
    acc_re, acc_im, hc_re, hc_im = lax.associative_scan(complex_affine_combine, elems, axis=1)
    h0r = h0_re.astype(f32)[:, None]
    h0i = h0_im.astype(f32)[:, None]
    h_re = hc_re + acc_re * h0r - acc_im * h0i
    h_im = hc_im + acc_re * h0i + acc_im * h0r
    y = (jnp.einsum('btgn,gcn->btgc', h_re, c_re.astype(f32))
         - jnp.einsum('btgn,gcn->btgc', h_im, c_im.astype(f32))
         + d.astype(f32) * u)
    return y, h_re[:, -1], h_im[:, -1]


def conv_ffn(x, prev, w_up, conv_w, conv_b, w_down):
    T = x.shape[1]
    up = x @ w_up
    gate, val = up[..., :D_FF], up[..., D_FF:]
    buf = jnp.concatenate([prev.astype(gate.dtype), gate], axis=1)
    conv = conv_b + sum(conv_w[j] * buf[:, j:j + T] for j in range(CONV_W))
    h = jax.nn.silu(conv) * val
    return h @ w_down, buf[:, buf.shape[1] - (CONV_W - 1):]


def decoder_layer(h, pe, lp, layer_idx, attend, h0_re, h0_im, conv_prev):
    B, T, _ = h.shape
    u = rmsnorm(h, lp['norm_mix_g'])
    z = u @ lp['w_in']
    q = z[..., :ATT_W].reshape(B, T, N_HEADS, 2, QK_DIM)
    k = z[..., ATT_W:2 * ATT_W].reshape(B, T, N_HEADS, 2, QK_DIM)
    v = z[..., 2 * ATT_W:3 * ATT_W].reshape(B, T, N_HEADS, V_HEAD)
    us = z[..., 3 * ATT_W:].reshape(B, T, N_SSM_GROUPS, SSM_GROUP)
    f32 = jnp.float32
    lam_init = 0.8 - 0.6 * math.exp(-0.3 * layer_idx)
    lam = (jnp.exp(jnp.sum(lp['lambda_q1'].astype(f32) * lp['lambda_k1'].astype(f32)))
           - jnp.exp(jnp.sum(lp['lambda_q2'].astype(f32) * lp['lambda_k2'].astype(f32))) + lam_init)
    o_att = attend(q, k, v, lam)
    o_att = rmsnorm(o_att, lp['attn_subln_g'], SUBLN_EPS) * (1.0 - lam_init)
    y, hr, hi = s5_scan(us, h0_re, h0_im, lp['ssm_a_re'], lp['ssm_a_im'], lp['ssm_b_re'], lp['ssm_b_im'],
                        lp['ssm_c_re'], lp['ssm_c_im'], lp['ssm_d'], lp['ssm_log_dt'])
    g = jax.nn.gelu(y)
    g = g * jax.nn.sigmoid(jnp.einsum('btgc,gce->btge', g, lp['ssm_glu_w'].astype(f32))
                           + lp['ssm_glu_b'].astype(f32))
    o_ssm = rmsnorm(g.reshape(B, T, SSM_W).astype(h.dtype), lp['ssm_out_g'])
    o = jnp.concatenate([o_att.reshape(B, T, ATT_W), o_ssm], axis=-1)
    h = h + o @ lp['w_out']
    f, conv_new = conv_ffn(rmsnorm(h, lp['norm_ffn_g']), conv_prev, lp['ffn_w_up'],
                           lp['ffn_conv_w'], lp['ffn_conv_b'], lp['ffn_w_down'])
    h = h + f
    gate = jax.nn.sigmoid(rmsnorm(h, lp['norm_ple_g']) @ lp['ple_w_gate'])
    h = h + gate * (pe @ lp['ple_w_proj'])
    return h, k, v, hr, hi, conv_new


def setup_inputs(seed: int = 0) -> dict:
    key = jax.random.key(seed)
    keys = iter(jax.random.split(key, 64))

    def nrm(shape, scale):
        return scale * jax.random.normal(next(keys), shape, jnp.float32)

    def gain(shape):
        return 1.0 + nrm(shape, 0.01)

    n_pages = PAST_LEN // PAGE_SIZE
    n_used = DEC_BATCH * n_pages
    n_pool = n_used + max(1, n_used // 4)
    G, N, GC = N_SSM_GROUPS, SSM_STATE, SSM_GROUP
    perm = jax.random.permutation(next(keys), n_pool)
    page_table = perm[:n_used].reshape(DEC_BATCH, n_pages).astype(jnp.int32)
    a_im = (jnp.pi * jnp.broadcast_to(jnp.arange(N, dtype=jnp.float32), (DEPTH, G, N))
            + nrm((DEPTH, G, N), 0.01))
    log_dt = jax.random.uniform(next(keys), (DEPTH, G), jnp.float32, math.log(1e-3), math.log(1e-1))
    return {
        'x_prompt': nrm((BATCH, SEQ, D_MODEL), 1.0),
        'x_sample': nrm((DEC_BATCH, DEC_SEQ, D_MODEL), 1.0),
        'cache_k': nrm((DEPTH, n_pool, PAGE_SIZE, N_HEADS, 2, QK_DIM), 1.0),
        'cache_v': nrm((DEPTH, n_pool, PAGE_SIZE, N_HEADS, V_HEAD), 1.0),
        'state_ssm_re': nrm((DEPTH, DEC_BATCH, G, N), 0.1),
        'state_ssm_im': nrm((DEPTH, DEC_BATCH, G, N), 0.1),
        'state_conv': nrm((DEPTH, DEC_BATCH, CONV_W - 1, D_FF), 1.0),
        'page_table': page_table,
        'p_prompt': nrm((DEPTH, BATCH, SEQ, PLE_DIM), 1.0),
        'p_sample': nrm((DEPTH, DEC_BATCH, DEC_SEQ, PLE_DIM), 1.0),
        'rel_bias': nrm((N_BUCKETS, N_HEADS), 0.5),
        'norm_mix_g': gain((DEPTH, D_MODEL)),
        'w_in': nrm((DEPTH, D_MODEL, D_IN), D_MODEL ** -0.5),
        'lambda_q1': nrm((DEPTH, QK_DIM), 0.1),
        'lambda_k1': nrm((DEPTH, QK_DIM), 0.1),
        'lambda_q2': nrm((DEPTH, QK_DIM), 0.1),
        'lambda_k2': nrm((DEPTH, QK_DIM), 0.1),
        'attn_subln_g': gain((DEPTH, N_HEADS, V_HEAD)),
        'ssm_a_re': -0.5 + nrm((DEPTH, G, N), 0.01),
        'ssm_a_im': a_im,
        'ssm_b_re': nrm((DEPTH, G, N, GC), (2 * GC) ** -0.5),
        'ssm_b_im': nrm((DEPTH, G, N, GC), (2 * GC) ** -0.5),
        'ssm_c_re': nrm((DEPTH, G, GC, N), N ** -0.5),
        'ssm_c_im': nrm((DEPTH, G, GC, N), N ** -0.5),
        'ssm_d': nrm((DEPTH, G, GC), 1.0),
        'ssm_log_dt': log_dt,
        'ssm_glu_w': nrm((DEPTH, G, GC, GC), GC ** -0.5),
        'ssm_glu_b': nrm((DEPTH, G, GC), 0.01),
        'ssm_out_g': gain((DEPTH, SSM_W)),
        'w_out': nrm((DEPTH, MIX_W, D_MODEL), MIX_W ** -0.5),
        'norm_ffn_g': gain((DEPTH, D_MODEL)),
        'ffn_w_up': nrm((DEPTH, D_MODEL, 2 * D_FF), D_MODEL ** -0.5),
        'ffn_conv_w': nrm((DEPTH, CONV_W, D_FF), CONV_W ** -0.5),
        'ffn_conv_b': nrm((DEPTH, D_FF), 0.01),
        'ffn_w_down': nrm((DEPTH, D_FF, D_MODEL), D_FF ** -0.5),
        'norm_ple_g': gain((DEPTH, D_MODEL)),
        'ple_w_gate': nrm((DEPTH, D_MODEL, D_MODEL), D_MODEL ** -0.5),
        'ple_w_proj': nrm((DEPTH, PLE_DIM, D_MODEL), PLE_DIM ** -0.5),
        'norm_final_g': gain((D_MODEL,)),
    }


def reference(x_prompt, x_sample, cache_k, cache_v, state_ssm_re, state_ssm_im, state_conv, page_table,
              p_prompt, p_sample, rel_bias, norm_mix_g, w_in, lambda_q1, lambda_k1, lambda_q2, lambda_k2,
              attn_subln_g, ssm_a_re, ssm_a_im, ssm_b_re, ssm_b_im, ssm_c_re, ssm_c_im, ssm_d, ssm_log_dt,
              ssm_glu_w, ssm_glu_b, ssm_out_g, w_out, norm_ffn_g, ffn_w_up, ffn_conv_w, ffn_conv_b,
              ffn_w_down, norm_ple_g, ple_w_gate, ple_w_proj, norm_final_g):
    hp, hs = x_prompt, x_sample
    kp, vp, srp, sip, cp = [], [], [], [], []
    ks, vs, srs, sis, cs = [], [], [], [], []
    for i in range(DEPTH):
        lp = {
            'norm_mix_g': norm_mix_g[i], 'w_in': w_in[i],
            'lambda_q1': lambda_q1[i], 'lambda_k1': lambda_k1[i],
            'lambda_q2': lambda_q2[i], 'lambda_k2': lambda_k2[i],
            'attn_subln_g': attn_subln_g[i],
            'ssm_a_re': ssm_a_re[i], 'ssm_a_im': ssm_a_im[i],
            'ssm_b_re': ssm_b_re[i], 'ssm_b_im': ssm_b_im[i],
            'ssm_c_re': ssm_c_re[i], 'ssm_c_im': ssm_c_im[i],
            'ssm_d': ssm_d[i], 'ssm_log_dt': ssm_log_dt[i],
            'ssm_glu_w': ssm_glu_w[i], 'ssm_glu_b': ssm_glu_b[i], 'ssm_out_g': ssm_out_g[i],
            'w_out': w_out[i], 'norm_ffn_g': norm_ffn_g[i], 'ffn_w_up': ffn_w_up[i],
            'ffn_conv_w': ffn_conv_w[i], 'ffn_conv_b': ffn_conv_b[i], 'ffn_w_down': ffn_w_down[i],
            'norm_ple_g': norm_ple_g[i], 'ple_w_gate': ple_w_gate[i], 'ple_w_proj': ple_w_proj[i],
        }
        zero_h = jnp.zeros((hp.shape[0], N_SSM_GROUPS, SSM_STATE), jnp.float32)
        zero_c = jnp.zeros((hp.shape[0], CONV_W - 1, D_FF), hp.dtype)
        attend_p = functools.partial(prompt_diff_attention, table=rel_bias)
        attend_s = functools.partial(sample_diff_attention, table=rel_bias, k_pool=cache_k[i],
                                     v_pool=cache_v[i], page_table=page_table)
        hp, k1, v1, r1, m1, c1 = decoder_layer(hp, p_prompt[i], lp, i, attend_p, zero_h, zero_h, zero_c)
        hs, k2, v2, r2, m2, c2 = decoder_layer(hs, p_sample[i], lp, i, attend_s,
                                               state_ssm_re[i], state_ssm_im[i], state_conv[i])
        kp.append(k1); vp.append(v1); srp.append(r1); sip.append(m1); cp.append(c1)
        ks.append(k2); vs.append(v2); srs.append(r2); sis.append(m2); cs.append(c2)
    y_prompt = rmsnorm(hp, norm_final_g)
    y_sample = rmsnorm(hs, norm_final_g)
    return (y_prompt, y_sample,
            jnp.stack(kp), jnp.stack(vp), jnp.stack(srp), jnp.stack(sip), jnp.stack(cp),
            jnp.stack(ks), jnp.stack(vs), jnp.stack(srs), jnp.stack(sis), jnp.stack(cs))
```

```python
import functools
import math

import jax
import jax.numpy as jnp
from jax import lax
from jax.experimental import pallas as pl
from jax.experimental.pallas import tpu as pltpu

F32 = jnp.float32
BF16 = jnp.bfloat16

D_MODEL = 2048
DEPTH = 1
PAGE_SIZE = 128
ATT_W = 1024
SSM_W = 1024
V_HEAD = 128
N_HEADS = 8
QK_DIM = 64
SSM_GROUP = 16
N_SSM_GROUPS = 64
SSM_STATE = 64
D_FF = 5504
CONV_W = 3
PLE_DIM = 256
N_BUCKETS = 32
MAX_DISTANCE = 128
NORM_EPS = 1e-6
SUBLN_EPS = 1e-5
QK_SCALE = QK_DIM ** -0.5

LANES = 128
SUBLANES = 8
VMEM_LIMIT = 56 * 1024 * 1024

GROUPS_PER_BLOCK = LANES // SSM_GROUP
N_GROUP_BLOCKS = N_SSM_GROUPS // GROUPS_PER_BLOCK
STATE_LANES = GROUPS_PER_BLOCK * SSM_STATE

D_FF_PAD = 5632
FFN_TF = 512
CONV_HALO = 16


def _params(sem):
    return pltpu.CompilerParams(dimension_semantics=sem, vmem_limit_bytes=VMEM_LIMIT)


def _rms(x, g, eps):
    return x * lax.rsqrt(jnp.mean(x * x, axis=-1, keepdims=True) + eps) * g


def _in_proj_kernel(x_ref, g_ref, w_ref, wkt_ref, q_ref, k_ref, v_ref, u_ref, xn_ref, *, k_transposed):
    j = pl.program_id(1)

    @pl.when(j == 0)
    def _():
        xn_ref[...] = _rms(x_ref[...], g_ref[...], NORM_EPS).astype(BF16)

    def proj():
        return jnp.dot(xn_ref[...], w_ref[...], preferred_element_type=F32)

    @pl.when(j == 0)
    def _():
        q_ref[...] = (proj() * QK_SCALE).astype(BF16)

    @pl.when(j == 1)
    def _():
        if k_transposed:
            k_ref[0] = lax.dot_general(wkt_ref[...], xn_ref[...], (((1,), (1,)), ((), ())),
                                       preferred_element_type=F32)
        else:
            k_ref[...] = proj()

    @pl.when(j == 2)
    def _():
        v_ref[...] = proj()

    @pl.when(j == 3)
    def _():
        u_ref[...] = proj()


def _in_proj(x, g, w, wkt, tm, rows_per_seq, k_transposed):
    m = x.shape[0]
    out_spec = pl.BlockSpec((tm, ATT_W), lambda i, j: (i, 0))
    if k_transposed:
        tiles = rows_per_seq // tm
        k_spec = pl.BlockSpec((1, ATT_W, tm), lambda i, j: (i // tiles, 0, i % tiles))
        k_shape = jax.ShapeDtypeStruct((m // rows_per_seq, ATT_W, rows_per_seq), F32)
        w_spec = pl.BlockSpec((D_MODEL, ATT_W), lambda i, j: (0, jnp.where(j == 1, 0, j)))
    else:
        k_spec = out_spec
        k_shape = jax.ShapeDtypeStruct((m, ATT_W), F32)
        w_spec = pl.BlockSpec((D_MODEL, ATT_W), lambda i, j: (0, j))
    return pl.pallas_call(
        functools.partial(_in_proj_kernel, k_transposed=k_transposed),
        grid=(m // tm, 4),
        in_specs=[
            pl.BlockSpec((tm, D_MODEL), lambda i, j: (i, 0)),
            pl.BlockSpec((1, D_MODEL), lambda i, j: (0, 0)),
            w_spec,
            pl.BlockSpec((ATT_W, D_MODEL), lambda i, j: (0, 0)),
        ],
        out_specs=[out_spec, k_spec, out_spec, out_spec],
        out_shape=[
            jax.ShapeDtypeStruct((m, ATT_W), BF16),
            k_shape,
            jax.ShapeDtypeStruct((m, ATT_W), F32),
            jax.ShapeDtypeStruct((m, SSM_W), F32),
        ],
        scratch_shapes=[pltpu.VMEM((tm, D_MODEL), BF16)],
        compiler_params=_params(("arbitrary", "arbitrary")),
        name="in_proj",
    )(x, g, w, wkt)


def _lambda(lam_ref, lam_init):
    s1 = jnp.sum(lam_ref[0:1, :] * lam_ref[1:2, :], axis=-1, keepdims=True)
    s2 = jnp.sum(lam_ref[2:3, :] * lam_ref[3:4, :], axis=-1, keepdims=True)
    return jnp.exp(s1) - jnp.exp(s2) + lam_init


def _online_softmax_step(s, v_blk, m_ref, l_ref, acc_ref):
    m_prev = m_ref[...]
    m_new = jnp.maximum(m_prev, jnp.max(s, axis=-1, keepdims=True))
    alpha = jnp.exp(m_prev - m_new)
    p = jnp.exp(s - m_new)
    l_ref[...] = alpha * l_ref[...] + jnp.sum(p, axis=-1, keepdims=True)
    acc_ref[...] = alpha * acc_ref[...] + jnp.dot(
        p.astype(BF16), v_blk, preferred_element_type=F32)
    m_ref[...] = m_new


def _attn_prompt_kernel(lam_ref, q_ref, k_ref, v_ref, bias_ref, g_ref, o_ref,
                        kb_ref, vb_ref, qs_ref, m_ref, l_ref, acc_ref, *, tq, lam_init):
    i = pl.program_id(2)

    @pl.when(i == 0)
    def _():
        kb_ref[...] = k_ref[0].astype(BF16)
        vb_ref[...] = v_ref[0].astype(BF16)

    q = q_ref[0]
    lane = lax.broadcasted_iota(jnp.int32, q.shape, 1)
    zero = jnp.zeros_like(q)
    qs_ref[0:tq, :] = jnp.where(lane < QK_DIM, q, zero)
    qs_ref[tq:2 * tq, :] = jnp.where(lane >= QK_DIM, q, zero)
    m_ref[...] = jnp.full(m_ref.shape, -jnp.inf, F32)
    l_ref[...] = jnp.zeros(l_ref.shape, F32)
    acc_ref[...] = jnp.zeros(acc_ref.shape, F32)

    def update(j, bias):
        off = pl.multiple_of(j * tq, tq)
        k_blk = kb_ref[:, pl.ds(off, tq)]
        v_blk = vb_ref[pl.ds(off, tq), :]
        s = jnp.dot(qs_ref[...], k_blk, preferred_element_type=F32)
        if bias is not None:
            s = (s.reshape(2, tq, tq) + bias[None]).reshape(2 * tq, tq)
        _online_softmax_step(s, v_blk, m_ref, l_ref, acc_ref)

    def far_body(j, carry):
        update(j, None)
        return carry

    lax.fori_loop(0, jnp.maximum(i - 1, 0), far_body, 0)

    @pl.when(i >= 1)
    def _():
        update(i - 1, bias_ref[0, 1])

    update(i, bias_ref[0, 0])

    inv_l = 1.0 / l_ref[...]
    lam = _lambda(lam_ref, lam_init)
    o = acc_ref[0:tq, :] * inv_l[0:tq] - lam * (acc_ref[tq:2 * tq, :] * inv_l[tq:2 * tq])
    o_ref[0] = (_rms(o, g_ref[0], SUBLN_EPS) * (1.0 - lam_init)).astype(o_ref.dtype)


def _attn_prompt(q, k, v, bias_tiles, lam_rows, subln_g, tq, lam_init):
    b, t, _ = q.shape
    kern = functools.partial(_attn_prompt_kernel, tq=tq, lam_init=lam_init)
    return pl.pallas_call(
        kern,
        grid=(b, N_HEADS, t // tq),
        in_specs=[
            pl.BlockSpec((4, QK_DIM), lambda bi, h, i: (0, 0)),
            pl.BlockSpec((1, tq, V_HEAD), lambda bi, h, i: (bi, i, h)),
            pl.BlockSpec((1, 2 * QK_DIM, t), lambda bi, h, i: (bi, h, 0)),
            pl.BlockSpec((1, t, V_HEAD), lambda bi, h, i: (bi, 0, h)),
            pl.BlockSpec((1, 2, tq, tq), lambda bi, h, i: (h, 0, 0, 0)),
            pl.BlockSpec((1, 1, V_HEAD), lambda bi, h, i: (h, 0, 0)),
        ],
        out_specs=pl.BlockSpec((1, tq, V_HEAD), lambda bi, h, i: (bi, i, h)),
        out_shape=jax.ShapeDtypeStruct((b, t, ATT_W), BF16),
        scratch_shapes=[
            pltpu.VMEM((2 * QK_DIM, t), BF16),
            pltpu.VMEM((t, V_HEAD), BF16),
            pltpu.VMEM((2 * tq, V_HEAD), BF16),
            pltpu.VMEM((2 * tq, 1), F32),
            pltpu.VMEM((2 * tq, 1), F32),
            pltpu.VMEM((2 * tq, V_HEAD), F32),
        ],
        compiler_params=_params(("arbitrary", "arbitrary", "arbitrary")),
        name="attn_prompt",
    )(lam_rows, q, k, v, bias_tiles, subln_g)


def _attn_sample_kernel(pt_ref, lam_ref, qbd_ref, *refs, n_pages_step, n_steps, n_new, lam_init):
    del pt_ref
    k_refs = refs[:n_pages_step]
    v_refs = refs[n_pages_step:2 * n_pages_step]
    (bias_ref, kn_ref, vn_ref, bn_ref, g_ref, o_ref, m_ref, l_ref, acc_ref) = refs[2 * n_pages_step:]
    step = pl.program_id(1)

    @pl.when(step == 0)
    def _():
        m_ref[...] = jnp.full(m_ref.shape, -jnp.inf, F32)
        l_ref[...] = jnp.zeros(l_ref.shape, F32)
        acc_ref[...] = jnp.zeros(acc_ref.shape, F32)

    qbd = qbd_ref[0]
    rows = 2 * n_new

    def block(k_page, v_ref, bias):
        s = jnp.dot(qbd, k_page.astype(BF16), preferred_element_type=F32) + bias
        m_prev = m_ref[...]
        m_new = jnp.maximum(m_prev, jnp.max(s, axis=-1, keepdims=True))
        alpha = jnp.exp(m_prev - m_new)
        p = jnp.exp(s - m_new)
        l_ref[...] = alpha * l_ref[...] + jnp.sum(p, axis=-1, keepdims=True)
        m_ref[...] = m_new
        pb = p.astype(BF16)
        for h in range(N_HEADS):
            v_h = v_ref[0, pl.ds(h, PAGE_SIZE, stride=N_HEADS), :].astype(BF16)
            r = slice(h * rows, (h + 1) * rows)
            acc_ref[r, :] = alpha[r] * acc_ref[r, :] + jnp.dot(pb[r], v_h, preferred_element_type=F32)

    for g in range(n_pages_step):
        block(k_refs[g][0], v_refs[g], bias_ref[:, g * PAGE_SIZE:(g + 1) * PAGE_SIZE])

    @pl.when(step == n_steps - 1)
    def _():
        block(kn_ref[0], vn_ref, bn_ref[...])
        lam = _lambda(lam_ref, lam_init)
        acc = acc_ref[...] * (1.0 / l_ref[...])
        for h in range(N_HEADS):
            blk = acc[h * rows:(h + 1) * rows]
            o = blk[0:n_new] - lam * blk[n_new:rows]
            o = _rms(o, g_ref[h:h + 1, :], SUBLN_EPS) * (1.0 - lam_init)
            o_ref[0, :, h * V_HEAD:(h + 1) * V_HEAD] = o


def _attn_sample(page_table, qbd, k_pool, v_pool, bias_past, k_new, v_new, bias_new,
                 lam_rows, subln_g, n_pages_step, lam_init):
    nb, n_pages = page_table.shape
    rows = qbd.shape[1]
    n_new = rows // (2 * N_HEADS)
    n_steps = n_pages // n_pages_step
    kern = functools.partial(_attn_sample_kernel, n_pages_step=n_pages_step, n_steps=n_steps,
                             n_new=n_new, lam_init=lam_init)

    def page_spec(g, shape):
        return pl.BlockSpec((1,) + shape, lambda b, s, pt: (pt[b, s * n_pages_step + g], 0, 0))

    grid_spec = pltpu.PrefetchScalarGridSpec(
        num_scalar_prefetch=1,
        grid=(nb, n_steps),
        in_specs=(
            [pl.BlockSpec((4, QK_DIM), lambda b, s, pt: (0, 0)),
             pl.BlockSpec((1, rows, ATT_W), lambda b, s, pt: (b, 0, 0))]
            + [page_spec(g, (ATT_W, PAGE_SIZE)) for g in range(n_pages_step)]
            + [page_spec(g, (PAGE_SIZE * N_HEADS, V_HEAD)) for g in range(n_pages_step)]
            + [pl.BlockSpec((rows, n_pages_step * PAGE_SIZE), lambda b, s, pt: (0, s)),
               pl.BlockSpec((1, ATT_W, PAGE_SIZE), lambda b, s, pt: (b, 0, 0)),
               pl.BlockSpec((1, PAGE_SIZE * N_HEADS, V_HEAD), lambda b, s, pt: (b, 0, 0)),
               pl.BlockSpec((rows, PAGE_SIZE), lambda b, s, pt: (0, 0)),
               pl.BlockSpec((N_HEADS, V_HEAD), lambda b, s, pt: (0, 0))]),
        out_specs=pl.BlockSpec((1, n_new, ATT_W), lambda b, s, pt: (b, 0, 0)),
        scratch_shapes=[
            pltpu.VMEM((rows, 1), F32),
            pltpu.VMEM((rows, 1), F32),
            pltpu.VMEM((rows, V_HEAD), F32),
        ],
    )
    return pl.pallas_call(
        kern,
        grid_spec=grid_spec,
        out_shape=jax.ShapeDtypeStruct((nb, n_new, ATT_W), F32),
        compiler_params=_params(("arbitrary", "arbitrary")),
        name="attn_sample",
    )(page_table, lam_rows, qbd, *([k_pool] * n_pages_step), *([v_pool] * n_pages_step),
      bias_past, k_new, v_new, bias_new, subln_g)


def _scan_time_on_sublanes(hbuf_ref, tbl, carry, tc):
    half = STATE_LANES
    coef = [(tbl[k, :, 0:half], tbl[k, :, half:2 * half]) for k in range(4)]

    def body(r, c):
        cr, ci = c
        off = pl.multiple_of(r * SUBLANES, SUBLANES)
        xr = hbuf_ref[pl.ds(off, SUBLANES), 0:half]
        xi = hbuf_ref[pl.ds(off, SUBLANES), half:2 * half]
        for k, shift in enumerate((1, 2, 4)):
            ar, ai = coef[k]
            rr = pltpu.roll(xr, shift, 0)
            ri = pltpu.roll(xi, shift, 0)
            xr, xi = xr + ar * rr - ai * ri, xi + ar * ri + ai * rr
        pr, pi = coef[3]
        yr = xr + pr * cr - pi * ci
        yi = xi + pr * ci + pi * cr
        hbuf_ref[pl.ds(off, SUBLANES), 0:half] = yr
        hbuf_ref[pl.ds(off, SUBLANES), half:2 * half] = yi
        last = SUBLANES - 1
        return (jnp.broadcast_to(yr[last:last + 1, :], yr.shape),
                jnp.broadcast_to(yi[last:last + 1, :], yi.shape))

    cr, ci = lax.fori_loop(0, tc // SUBLANES, body, (carry[:, 0:half], carry[:, half:2 * half]))
    return jnp.concatenate([cr, ci], axis=1)


def _scan_time_major(hbuf_ref, tbl, carry, tc):
    half = STATE_LANES
    lr = jnp.broadcast_to(tbl[3, 0:1, 0:half], (SUBLANES, half))
    li = jnp.broadcast_to(tbl[3, 0:1, half:2 * half], (SUBLANES, half))
    hr, hi = carry[:, 0:half], carry[:, half:2 * half]
    for t in range(tc // SUBLANES):
        rows = slice(t * SUBLANES, (t + 1) * SUBLANES)
        xr = hbuf_ref[rows, 0:half]
        xi = hbuf_ref[rows, half:2 * half]
        hr, hi = lr * hr - li * hi + xr, lr * hi + li * hr + xi
        hbuf_ref[rows, 0:half] = hr
        hbuf_ref[rows, half:2 * half] = hi
    return jnp.concatenate([hr, hi], axis=1)


def _ssm_kernel(u_ref, h0_ref, wb_ref, cb_ref, tbl_ref, d_ref, gw_ref, gbias_ref, og_ref,
                o_ref, hf_ref, hbuf_ref, gbuf_ref, carry_ref, *, tc, time_major):
    c = pl.program_id(1)

    @pl.when(c == 0)
    def _():
        carry_ref[...] = h0_ref[0]

    ssq = jnp.zeros((tc, 1), F32)
    for gb in range(N_GROUP_BLOCKS):
        cols = slice(gb * LANES, (gb + 1) * LANES)
        u = u_ref[0, :, cols]
        hbuf_ref[...] = jnp.dot(u.astype(BF16), wb_ref[gb], preferred_element_type=F32)
        scan = _scan_time_major if time_major else _scan_time_on_sublanes
        carry_ref[gb] = scan(hbuf_ref, tbl_ref[gb], carry_ref[gb], tc)
        y = jnp.dot(hbuf_ref[...].astype(BF16), cb_ref[gb], preferred_element_type=F32)
        y = y + d_ref[:, cols] * u
        g = jax.nn.gelu(y)
        gate = jnp.dot(g.astype(BF16), gw_ref[gb], preferred_element_type=F32) + gbias_ref[:, cols]
        g = g * jax.nn.sigmoid(gate)
        gbuf_ref[:, cols] = g
        ssq = ssq + jnp.sum(g * g, axis=-1, keepdims=True)
    inv = lax.rsqrt(ssq * (1.0 / SSM_W) + NORM_EPS)
    o_ref[0] = (gbuf_ref[...] * inv * og_ref[...]).astype(o_ref.dtype)
    hf_ref[0] = carry_ref[...]


def _ssm(u, h0, wb, cb, tbl, d, gw, gbias, og, tc, time_major, out_dtype):
    ns, t, _ = u.shape
    kern = functools.partial(_ssm_kernel, tc=tc, time_major=time_major)
    full = lambda *shape: pl.BlockSpec(shape, lambda s, c: (0,) * len(shape))
    state_shape = (N_GROUP_BLOCKS, SUBLANES, 2 * STATE_LANES)
    return pl.pallas_call(
        kern,
        grid=(ns, t // tc),
        in_specs=[
            pl.BlockSpec((1, tc, SSM_W), lambda s, c: (s, c, 0)),
            pl.BlockSpec((1,) + state_shape, lambda s, c: (s, 0, 0, 0)),
            full(N_GROUP_BLOCKS, LANES, 2 * STATE_LANES),
            full(N_GROUP_BLOCKS, 2 * STATE_LANES, LANES),
            full(N_GROUP_BLOCKS, 4, SUBLANES, 2 * STATE_LANES),
            full(1, SSM_W),
            full(N_GROUP_BLOCKS, LANES, LANES),
            full(1, SSM_W),
            full(1, SSM_W),
        ],
        out_specs=[
            pl.BlockSpec((1, tc, SSM_W), lambda s, c: (s, c, 0)),
            pl.BlockSpec((1,) + state_shape, lambda s, c: (s, 0, 0, 0)),
        ],
        out_shape=[
            jax.ShapeDtypeStruct((ns, t, SSM_W), out_dtype),
            jax.ShapeDtypeStruct((ns,) + state_shape, F32),
        ],
        scratch_shapes=[
            pltpu.VMEM((tc, 2 * STATE_LANES), F32),
            pltpu.VMEM((tc, SSM_W), F32),
            pltpu.VMEM(state_shape, F32),
        ],
        compiler_params=_params(("arbitrary", "arbitrary")),
        name="ssm_time_major" if time_major else "ssm_prompt",
    )(u, h0, wb, cb, tbl, d, gw, gbias, og)


def _out_proj_kernel(x_ref, oa_ref, os_ref, w_ref, o_ref):
    acc = jnp.dot(oa_ref[...].astype(BF16), w_ref[0], preferred_element_type=F32)
    acc = acc + jnp.dot(os_ref[...].astype(BF16), w_ref[1], preferred_element_type=F32)
    o_ref[...] = x_ref[...] + acc


def _out_proj(x, o_att, o_ssm, w, tm):
    m = x.shape[0]
    return pl.pallas_call(
        _out_proj_kernel,
        grid=(m // tm,),
        in_specs=[
            pl.BlockSpec((tm, D_MODEL), lambda i: (i, 0)),
            pl.BlockSpec((tm, ATT_W), lambda i: (i, 0)),
            pl.BlockSpec((tm, SSM_W), lambda i: (i, 0)),
            pl.BlockSpec((2, ATT_W, D_MODEL), lambda i: (0, 0, 0)),
        ],
        out_specs=pl.BlockSpec((tm, D_MODEL), lambda i: (i, 0)),
        out_shape=jax.ShapeDtypeStruct((m, D_MODEL), F32),
        compiler_params=_params(("arbitrary",)),
        name="out_proj",
    )(x, o_att, o_ssm, w)


def _ffn_kernel(h_ref, g_ref, prev_ref, wup_ref, cw_ref, cb_ref, wdn_ref, o_ref, cn_ref,
                fn_ref, acc_ref, halo_ref, *, tm, shift, tiles_per_seq, nf):
    i = pl.program_id(0)
    f = pl.program_id(1)

    @pl.when(f == 0)
    def _():
        fn_ref[...] = _rms(h_ref[...], g_ref[...], NORM_EPS).astype(BF16)
        acc_ref[...] = jnp.zeros(acc_ref.shape, F32)

    fn = fn_ref[...]
    gate = jnp.dot(fn, wup_ref[0], preferred_element_type=F32)
    val = jnp.dot(fn, wup_ref[1], preferred_element_type=F32)

    @pl.when((i % tiles_per_seq) == 0)
    def _():
        halo_ref[f] = prev_ref[0]

    gext = jnp.concatenate([halo_ref[f], gate], axis=0)
    g1 = gext[CONV_HALO - shift:CONV_HALO - shift + tm]
    g2 = gext[CONV_HALO - 2 * shift:CONV_HALO - 2 * shift + tm]
    conv = cb_ref[...] + (cw_ref[0:1, :] * g2 + cw_ref[1:2, :] * g1 + cw_ref[2:3, :] * gate)
    hact = (conv * jax.nn.sigmoid(conv)) * val
    acc_ref[...] += jnp.dot(hact.astype(BF16), wdn_ref[...], preferred_element_type=F32)
    halo_ref[f] = gate[tm - CONV_HALO:tm]
    cn_ref[0] = gate[tm - 2 * shift:tm]

    @pl.when(f == nf - 1)
    def _():
        o_ref[...] = h_ref[...] + acc_ref[...]


def _ffn(h, g, prev, wup, cw, cb, wdn, tm, shift, tiles_per_seq):
    m = h.shape[0]
    nf = D_FF_PAD // FFN_TF
    kern = functools.partial(_ffn_kernel, tm=tm, shift=shift, tiles_per_seq=tiles_per_seq, nf=nf)
    return pl.pallas_call(
        kern,
        grid=(m // tm, nf),
        in_specs=[
            pl.BlockSpec((tm, D_MODEL), lambda i, f: (i, 0)),
            pl.BlockSpec((1, D_MODEL), lambda i, f: (0, 0)),
            pl.BlockSpec((1, CONV_HALO, FFN_TF), lambda i, f: (i // tiles_per_seq, 0, f)),
            pl.BlockSpec((2, D_MODEL, FFN_TF), lambda i, f: (0, 0, f)),
            pl.BlockSpec((CONV_W, FFN_TF), lambda i, f: (0, f)),
            pl.BlockSpec((1, FFN_TF), lambda i, f: (0, f)),
            pl.BlockSpec((FFN_TF, D_MODEL), lambda i, f: (f, 0)),
        ],
        out_specs=[
            pl.BlockSpec((tm, D_MODEL), lambda i, f: (i, 0)),
            pl.BlockSpec((1, 2 * shift, FFN_TF), lambda i, f: (i, 0, f)),
        ],
        out_shape=[
            jax.ShapeDtypeStruct((m, D_MODEL), F32),
            jax.ShapeDtypeStruct((m // tm, 2 * shift, D_FF_PAD), F32),
        ],
        scratch_shapes=[
            pltpu.VMEM((tm, D_MODEL), BF16),
            pltpu.VMEM((tm, D_MODEL), F32),
            pltpu.VMEM((nf, CONV_HALO, FFN_TF), F32),
        ],
        compiler_params=_params(("arbitrary", "arbitrary")),
        name="conv_ffn",
    )(h, g, prev, wup, cw, cb, wdn)


def _ple_kernel(h_ref, gp_ref, pe_ref, wg_ref, wp_ref, gf_ref, o_ref):
    h = h_ref[...]
    hn = _rms(h, gp_ref[...], NORM_EPS).astype(BF16)
    gate = jax.nn.sigmoid(jnp.dot(hn, wg_ref[...], preferred_element_type=F32))
    proj = jnp.dot(pe_ref[...].astype(BF16), wp_ref[...], preferred_element_type=F32)
    o_ref[...] = _rms(h + gate * proj, gf_ref[...], NORM_EPS)


def _ple(h, gp, pe, wg, wp, gf, tm):
    m = h.shape[0]
    return pl.pallas_call(
        _ple_kernel,
        grid=(m // tm,),
        in_specs=[
            pl.BlockSpec((tm, D_MODEL), lambda i: (i, 0)),
            pl.BlockSpec((1, D_MODEL), lambda i: (0, 0)),
            pl.BlockSpec((tm, PLE_DIM), lambda i: (i, 0)),
            pl.BlockSpec((D_MODEL, D_MODEL), lambda i: (0, 0)),
            pl.BlockSpec((PLE_DIM, D_MODEL), lambda i: (0, 0)),
            pl.BlockSpec((1, D_MODEL), lambda i: (0, 0)),
        ],
        out_specs=pl.BlockSpec((tm, D_MODEL), lambda i: (i, 0)),
        out_shape=jax.ShapeDtypeStruct((m, D_MODEL), F32),
        compiler_params=_params(("arbitrary",)),
        name="ple_final",
    )(h, gp, pe, wg, wp, gf)


def _t5_bucket(dist):
    max_exact = N_BUCKETS // 2
    n = jnp.maximum(dist, 0)
    nf = jnp.maximum(n, 1).astype(F32)
    large = max_exact + (jnp.log(nf / max_exact) / math.log(MAX_DISTANCE / max_exact)
                         * (N_BUCKETS - max_exact)).astype(jnp.int32)
    large = jnp.minimum(large, N_BUCKETS - 1)
    return jnp.where(n < max_exact, n, large)


def _toeplitz(w, n):
    lead = w.shape[:-1]
    flat = jnp.tile(w, (1,) * len(lead) + (n,))[..., :n * (2 * n - 1)]
    return flat.reshape(lead + (n, 2 * n - 1))[..., :n]


def _prompt_bias_tiles(table, n):
    far = table[N_BUCKETS - 1]
    bvec = (table[_t5_bucket(jnp.arange(2 * n, dtype=jnp.int32))] - far[None, :]).T
    d = jnp.arange(2 * n, dtype=jnp.int32)
    d = jnp.where(d >= n, d - 2 * n, d)
    diag = jnp.where(d[None, :] <= 0, bvec[:, jnp.abs(d)], -jnp.inf)
    sub = bvec[:, jnp.clip(n - d, 0, 2 * n - 1)]
    return jnp.stack([_toeplitz(diag, n), _toeplitz(sub, n)], axis=1)


def _sample_bias(table, past_len, n_new):
    far = table[N_BUCKETS - 1]
    t = jnp.arange(n_new, dtype=jnp.int32)
    dist_past = past_len + t[:, None] - jnp.arange(past_len, dtype=jnp.int32)[None, :]
    bp = table[_t5_bucket(dist_past)] - far
    bp = jnp.transpose(bp, (2, 0, 1))
    bp = jnp.broadcast_to(bp[:, None], (N_HEADS, 2, n_new, past_len)).reshape(-1, past_len)
    col = jnp.arange(PAGE_SIZE, dtype=jnp.int32)
    dist_new = t[:, None] - col[None, :]
    bn = table[_t5_bucket(dist_new)] - far
    bn = jnp.where((dist_new >= 0)[..., None], bn, -jnp.inf)
    bn = jnp.transpose(bn, (2, 0, 1))
    bn = jnp.broadcast_to(bn[:, None], (N_HEADS, 2, n_new, PAGE_SIZE)).reshape(-1, PAGE_SIZE)
    return bp, bn


def _block_diag(w):
    nb, gp, a, b = w.shape
    eye = jnp.eye(gp, dtype=w.dtype)
    out = w[:, :, :, None, :] * eye[None, :, None, :, None]
    return out.reshape(nb, gp * a, gp * b)


def _ssm_tables(a_re, a_im, b_re, b_im, c_re, c_im, log_dt):
    g, n, gc = N_SSM_GROUPS, SSM_STATE, SSM_GROUP
    dt = jnp.exp(log_dt)[:, None]
    decay = jnp.exp(a_re * dt)
    ab_re = decay * jnp.cos(a_im * dt)
    ab_im = decay * jnp.sin(a_im * dt)
    den = a_re * a_re + a_im * a_im
    num_re = ab_re - 1.0
    coef_re = (num_re * a_re + ab_im * a_im) / den
    coef_im = (ab_im * a_re - num_re * a_im) / den
    bb_re = coef_re[..., None] * b_re - coef_im[..., None] * b_im
    bb_im = coef_re[..., None] * b_im + coef_im[..., None] * b_re
    blk = lambda w: w.reshape((N_GROUP_BLOCKS, GROUPS_PER_BLOCK) + w.shape[1:])
    to_cn = lambda w: jnp.swapaxes(blk(w), 2, 3)
    wb = jnp.concatenate([_block_diag(to_cn(bb_re)), _block_diag(to_cn(bb_im))], axis=2)
    to_nc = lambda w: jnp.swapaxes(blk(w), 2, 3)
    cb = jnp.concatenate([_block_diag(to_nc(c_re)), _block_diag(to_nc(-c_im))], axis=1)
    pr = [ab_re.reshape(N_GROUP_BLOCKS, STATE_LANES)]
    pi = [ab_im.reshape(N_GROUP_BLOCKS, STATE_LANES)]
    for _ in range(SUBLANES - 1):
        pr_n = pr[-1] * pr[0] - pi[-1] * pi[0]
        pi_n = pr[-1] * pi[0] + pi[-1] * pr[0]
        pr.append(pr_n)
        pi.append(pi_n)
    row = jnp.arange(SUBLANES)[None, :, None]

    def masked(k):
        re = jnp.where(row >= k, pr[k - 1][:, None, :], 0.0)
        im = jnp.where(row >= k, pi[k - 1][:, None, :], 0.0)
        return jnp.concatenate([re, im], axis=-1)

    carry_pow = jnp.concatenate([jnp.stack(pr, axis=1), jnp.stack(pi, axis=1)], axis=-1)
    tbl = jnp.stack([masked(1), masked(2), masked(4), carry_pow], axis=1)
    return wb.astype(BF16), cb.astype(BF16), tbl.astype(F32)


def _pad_ff(w, axis):
    pad = [(0, 0)] * w.ndim
    pad[axis] = (0, D_FF_PAD - D_FF)
    return jnp.pad(w, pad)


def _state_to_blocks(s):
    return s.reshape(s.shape[0], N_GROUP_BLOCKS, STATE_LANES)


def kernel(x_prompt, x_sample, cache_k, cache_v, state_ssm_re, state_ssm_im, state_conv, page_table, p_prompt, p_sample, rel_bias, norm_mix_g, w_in, lambda_q1, lambda_k1, lambda_q2, lambda_k2, attn_subln_g, ssm_a_re, ssm_a_im, ssm_b_re, ssm_b_im, ssm_c_re, ssm_c_im, ssm_d, ssm_log_dt, ssm_glu_w, ssm_glu_b, ssm_out_g, w_out, norm_ffn_g, ffn_w_up, ffn_conv_w, ffn_conv_b, ffn_w_down, norm_ple_g, ple_w_gate, ple_w_proj, norm_final_g):
    assert DEPTH == 1
    li = 0
    lam_init = 0.8 - 0.6 * math.exp(-0.3 * li)
    nb, t, _ = x_prompt.shape
    db, ds, _ = x_sample.shape
    assert db == SUBLANES
    m_p = nb * t
    m_s = db * ds
    past_len = page_table.shape[1] * PAGE_SIZE

    row = lambda v: v.reshape(1, -1).astype(F32)
    w_in_b = w_in[li].astype(BF16)
    w_kt_b = w_in[li][:, ATT_W:2 * ATT_W].T.astype(BF16)
    w_out_b = w_out[li].astype(BF16).reshape(2, ATT_W, D_MODEL)
    w_up = ffn_w_up[li]
    w_up_b = jnp.stack([_pad_ff(w_up[:, :D_FF], 1), _pad_ff(w_up[:, D_FF:], 1)]).astype(BF16)
    w_dn_b = _pad_ff(ffn_w_down[li], 0).astype(BF16)
    conv_w = _pad_ff(ffn_conv_w[li], 1)
    conv_b = _pad_ff(ffn_conv_b[li].reshape(1, D_FF), 1)
    w_gate_b = ple_w_gate[li].astype(BF16)
    w_proj_b = ple_w_proj[li].astype(BF16)
    lam_rows = jnp.stack([lambda_q1[li], lambda_k1[li], lambda_q2[li], lambda_k2[li]]).astype(F32)
    subln_g = attn_subln_g[li].astype(F32)
    wb, cb, tbl = _ssm_tables(ssm_a_re[li], ssm_a_im[li], ssm_b_re[li], ssm_b_im[li],
                              ssm_c_re[li], ssm_c_im[li], ssm_log_dt[li])
    glu_w = _block_diag(ssm_glu_w[li].reshape(N_GROUP_BLOCKS, GROUPS_PER_BLOCK, SSM_GROUP, SSM_GROUP)).astype(BF16)
    ssm_args = (wb, cb, tbl, row(ssm_d[li]), glu_w, row(ssm_glu_b[li]), row(ssm_out_g[li]))

    tq = 512
    xp = x_prompt.reshape(m_p, D_MODEL)
    q_p, kt_p, v_p, u_p = _in_proj(xp, row(norm_mix_g[li]), w_in_b, w_kt_b, tm=512,
                                   rows_per_seq=t, k_transposed=True)
    shp = lambda a: a.reshape(nb, t, -1)
    bias_tiles = _prompt_bias_tiles(rel_bias.astype(F32), tq)
    o_att_p = _attn_prompt(shp(q_p), kt_p, shp(v_p), bias_tiles, lam_rows,
                           subln_g.reshape(N_HEADS, 1, V_HEAD), tq, lam_init)
    zero_state = jnp.zeros((nb, N_GROUP_BLOCKS, SUBLANES, 2 * STATE_LANES), F32)
    o_ssm_p, hf_p = _ssm(shp(u_p), zero_state, *ssm_args, tc=256, time_major=False, out_dtype=BF16)
    h1_p = _out_proj(xp, o_att_p.reshape(m_p, ATT_W), o_ssm_p.reshape(m_p, SSM_W), w_out_b, tm=512)
    tm_ffn = 512
    zero_prev = jnp.zeros((nb, CONV_HALO, D_FF_PAD), F32)
    h2_p, conv_p = _ffn(h1_p, row(norm_ffn_g[li]), zero_prev, w_up_b, conv_w, conv_b, w_dn_b,
                        tm=tm_ffn, shift=1, tiles_per_seq=t // tm_ffn)
    y_p = _ple(h2_p, row(norm_ple_g[li]), p_prompt[li].reshape(m_p, PLE_DIM), w_gate_b, w_proj_b,
               row(norm_final_g), tm=512)

    tmaj = lambda a: jnp.swapaxes(a, 0, 1).reshape(m_s, -1)
    bmaj = lambda a: jnp.swapaxes(a.reshape(ds, db, -1), 0, 1)
    xs = tmaj(x_sample)
    q_s, k_s, v_s, u_s = _in_proj(xs, row(norm_mix_g[li]), w_in_b, w_kt_b, tm=m_s,
                                  rows_per_seq=m_s, k_transposed=False)
    q_b = bmaj(q_s)
    q_rep = jnp.tile(q_b, (1, 2 * N_HEADS, 1)).reshape(db, 2 * N_HEADS, ds, ATT_W)
    sel = (jnp.arange(ATT_W)[None, :] // QK_DIM) == jnp.arange(2 * N_HEADS)[:, None]
    qbd = jnp.where(sel[None, :, None, :], q_rep, jnp.zeros_like(q_rep)).reshape(db, 2 * N_HEADS * ds, ATT_W)
    pad_rows = lambda a: jnp.pad(bmaj(a), ((0, 0), (0, PAGE_SIZE - ds), (0, 0)))
    bias_past, bias_new = _sample_bias(rel_bias.astype(F32), past_len, ds)
    n_pool = cache_k.shape[1]
    k_pool_t = jnp.transpose(cache_k[li], (0, 2, 3, 4, 1)).reshape(n_pool, ATT_W, PAGE_SIZE)
    k_new_t = jnp.swapaxes(pad_rows(k_s), 1, 2)
    v_pool_r = cache_v[li].reshape(n_pool, PAGE_SIZE * N_HEADS, V_HEAD)
    v_new_r = pad_rows(v_s).reshape(db, PAGE_SIZE * N_HEADS, V_HEAD)
    o_att_s = _attn_sample(page_table, qbd, k_pool_t, v_pool_r,
                           bias_past, k_new_t, v_new_r, bias_new,
                           lam_rows, subln_g, n_pages_step=4, lam_init=lam_init)
    h0_s = jnp.concatenate([_state_to_blocks(state_ssm_re[li]), _state_to_blocks(state_ssm_im[li])], axis=-1)
    h0_s = jnp.swapaxes(h0_s, 0, 1)[None]
    o_ssm_s, hf_s = _ssm(u_s[None], h0_s, *ssm_args, tc=m_s, time_major=True, out_dtype=F32)
    h1_s = _out_proj(xs, tmaj(o_att_s), o_ssm_s[0], w_out_b, tm=m_s)
    prev_s = _pad_ff(jnp.swapaxes(state_conv[li], 0, 1).reshape(1, 2 * db, D_FF), 2)
    h2_s, conv_s = _ffn(h1_s, row(norm_ffn_g[li]), prev_s, w_up_b, conv_w, conv_b, w_dn_b,
                        tm=m_s, shift=db, tiles_per_seq=1)
    y_s = _ple(h2_s, row(norm_ple_g[li]), tmaj(p_sample[li]), w_gate_b, w_proj_b,
               row(norm_final_g), tm=m_s)

    def split_state(hf):
        re = hf[..., :STATE_LANES].reshape(1, -1, N_SSM_GROUPS, SSM_STATE)
        im = hf[..., STATE_LANES:].reshape(1, -1, N_SSM_GROUPS, SSM_STATE)
        return re, im

    re_p, im_p = split_state(hf_p[:, :, 0, :])
    re_s, im_s = split_state(jnp.swapaxes(hf_s[0], 0, 1))
    tiles = t // tm_ffn
    conv_p = conv_p[tiles - 1::tiles][None, :, :, :D_FF]
    conv_s = jnp.swapaxes(conv_s.reshape(2, db, D_FF_PAD), 0, 1)[None, :, :, :D_FF]
    return (y_p.reshape(nb, t, D_MODEL),
            bmaj(y_s),
            jnp.transpose(kt_p.reshape(1, nb, N_HEADS, 2, QK_DIM, t), (0, 1, 5, 2, 3, 4)),
            v_p.reshape(1, nb, t, N_HEADS, V_HEAD),
            re_p, im_p, conv_p,
            bmaj(k_s).reshape(1, db, ds, N_HEADS, 2, QK_DIM),
            bmaj(v_s).reshape(1, db, ds, N_HEADS, V_HEAD),
            re_s, im_s, conv_s)
```

```python
import functools
import math

import jax
import jax.numpy as jnp
from jax import lax
from jax.experimental import pallas as pl
from jax.experimental.pallas import tpu as pltpu

F32 = jnp.float32
BF16 = jnp.bfloat16

D_MODEL = 2048
DEPTH = 1
PAGE_SIZE = 128
ATT_W = 1024
SSM_W = 1024
V_HEAD = 128
N_HEADS = 8
QK_DIM = 64
SSM_GROUP = 16
N_SSM_GROUPS = 64
SSM_STATE = 64
D_FF = 5504
CONV_W = 3
PLE_DIM = 256
N_BUCKETS = 32
MAX_DISTANCE = 128
NORM_EPS = 1e-6
SUBLN_EPS = 1e-5
LOG2E = math.log2(math.e)
QK_SCALE = QK_DIM ** -0.5 * LOG2E

LANES = 128
SUBLANES = 8
VMEM_LIMIT = 56 * 1024 * 1024

GROUPS_PER_BLOCK = LANES // SSM_GROUP
N_GROUP_BLOCKS = N_SSM_GROUPS // GROUPS_PER_BLOCK
STATE_LANES = GROUPS_PER_BLOCK * SSM_STATE

D_FF_PAD = 5632
FFN_TF = 512
CONV_HALO = 16


def _params(sem):
    return pltpu.CompilerParams(dimension_semantics=sem, vmem_limit_bytes=VMEM_LIMIT)


def _rms(x, g, eps):
    return x * lax.rsqrt(jnp.mean(x * x, axis=-1, keepdims=True) + eps) * g


def _in_proj_kernel(x_ref, g_ref, w_ref, wkt_ref, q_ref, k_ref, v_ref, u_ref, xn_ref, *, k_transposed):
    j = pl.program_id(1)

    @pl.when(j == 0)
    def _():
        xn_ref[...] = _rms(x_ref[...], g_ref[...], NORM_EPS).astype(BF16)

    def proj():
        return jnp.dot(xn_ref[...], w_ref[...], preferred_element_type=F32)

    @pl.when(j == 0)
    def _():
        q_ref[...] = (proj() * QK_SCALE).astype(BF16)

    @pl.when(j == 1)
    def _():
        if k_transposed:
            k_ref[0] = lax.dot_general(wkt_ref[...], xn_ref[...], (((1,), (1,)), ((), ())),
                                       preferred_element_type=F32)
        else:
            k_ref[...] = proj()

    @pl.when(j == 2)
    def _():
        v_ref[...] = proj()

    @pl.when(j == 3)
    def _():
        u_ref[...] = proj()


def _in_proj(x, g, w, wkt, tm, rows_per_seq, k_transposed):
    m = x.shape[0]
    out_spec = pl.BlockSpec((tm, ATT_W), lambda i, j: (i, 0))
    if k_transposed:
        tiles = rows_per_seq // tm
        k_spec = pl.BlockSpec((1, ATT_W, tm), lambda i, j: (i // tiles, 0, i % tiles))
        k_shape = jax.ShapeDtypeStruct((m // rows_per_seq, ATT_W, rows_per_seq), F32)
        w_spec = pl.BlockSpec((D_MODEL, ATT_W), lambda i, j: (0, jnp.where(j == 1, 0, j)))
    else:
        k_spec = out_spec
        k_shape = jax.ShapeDtypeStruct((m, ATT_W), F32)
        w_spec = pl.BlockSpec((D_MODEL, ATT_W), lambda i, j: (0, j))
    return pl.pallas_call(
        functools.partial(_in_proj_kernel, k_transposed=k_transposed),
        grid=(m // tm, 4),
        in_specs=[
            pl.BlockSpec((tm, D_MODEL), lambda i, j: (i, 0)),
            pl.BlockSpec((1, D_MODEL), lambda i, j: (0, 0)),
            w_spec,
            pl.BlockSpec((ATT_W, D_MODEL), lambda i, j: (0, 0)),
        ],
        out_specs=[out_spec, k_spec, out_spec, out_spec],
        out_shape=[
            jax.ShapeDtypeStruct((m, ATT_W), BF16),
            k_shape,
            jax.ShapeDtypeStruct((m, ATT_W), F32),
            jax.ShapeDtypeStruct((m, SSM_W), F32),
        ],
        scratch_shapes=[pltpu.VMEM((tm, D_MODEL), BF16)],
        compiler_params=_params(("arbitrary", "arbitrary")),
        name="in_proj",
    )(x, g, w, wkt)


def _lambda(lam_ref, lam_init):
    s1 = jnp.sum(lam_ref[0:1, :] * lam_ref[1:2, :], axis=-1, keepdims=True)
    s2 = jnp.sum(lam_ref[2:3, :] * lam_ref[3:4, :], axis=-1, keepdims=True)
    return jnp.exp(s1) - jnp.exp(s2) + lam_init


def _online_softmax_step(s, v_blk, m_ref, l_ref, acc_ref):
    m_prev = m_ref[...]
    m_new = jnp.maximum(m_prev, jnp.max(s, axis=-1, keepdims=True))
    alpha = jnp.exp2(m_prev - m_new)
    p = jnp.exp2(s - jnp.concatenate([m_new] * (s.shape[1] // LANES), axis=1))
    l_ref[...] = alpha * l_ref[...] + jnp.sum(p, axis=-1, keepdims=True)
    acc_ref[...] = alpha * acc_ref[...] + jnp.dot(
        p.astype(BF16), v_blk, preferred_element_type=F32)
    m_ref[...] = m_new


def _attn_prompt_kernel(lam_ref, q_ref, k_ref, v_ref, bias_ref, g_ref, o_ref,
                        kb_ref, vb_ref, qs_ref, m_ref, l_ref, acc_ref, *, tq, lam_init):
    i = pl.program_id(2)

    @pl.when(i == 0)
    def _():
        kb_ref[...] = k_ref[0].astype(BF16)
        vb_ref[...] = v_ref[0].astype(BF16)

    q = q_ref[0]
    lane = lax.broadcasted_iota(jnp.int32, q.shape, 1)
    zero = jnp.zeros_like(q)
    qs_ref[0:tq, :] = jnp.where(lane < QK_DIM, q, zero)
    qs_ref[tq:2 * tq, :] = jnp.where(lane >= QK_DIM, q, zero)
    m_ref[...] = jnp.full(m_ref.shape, -jnp.inf, F32)
    l_ref[...] = jnp.zeros(l_ref.shape, F32)
    acc_ref[...] = jnp.zeros(acc_ref.shape, F32)

    def update(j, width, bias):
        off = pl.multiple_of(j * tq, tq)
        k_blk = kb_ref[:, pl.ds(off, width)]
        v_blk = vb_ref[pl.ds(off, width), :]
        s = jnp.dot(qs_ref[...], k_blk, preferred_element_type=F32)
        if bias is not None:
            s = (s.reshape(2, tq, width) + bias[None]).reshape(2 * tq, width)
        _online_softmax_step(s, v_blk, m_ref, l_ref, acc_ref)

    def far_body(j, carry):
        update(j, tq, None)
        return carry

    lax.fori_loop(0, jnp.maximum(i - 1, 0), far_body, 0)

    @pl.when(i >= 1)
    def _():
        update(i - 1, 2 * tq, bias_ref[0])

    @pl.when(i == 0)
    def _():
        update(0, tq, bias_ref[0, :, tq:2 * tq])

    inv_l = 1.0 / l_ref[...]
    lam = _lambda(lam_ref, lam_init)
    o = acc_ref[0:tq, :] * inv_l[0:tq] - lam * (acc_ref[tq:2 * tq, :] * inv_l[tq:2 * tq])
    o_ref[0] = (_rms(o, g_ref[0], SUBLN_EPS) * (1.0 - lam_init)).astype(o_ref.dtype)


def _attn_prompt(q, k, v, bias_tiles, lam_rows, subln_g, tq, lam_init):
    b, t, _ = q.shape
    kern = functools.partial(_attn_prompt_kernel, tq=tq, lam_init=lam_init)
    return pl.pallas_call(
        kern,
        grid=(b, N_HEADS, t // tq),
        in_specs=[
            pl.BlockSpec((4, QK_DIM), lambda bi, h, i: (0, 0)),
            pl.BlockSpec((1, tq, V_HEAD), lambda bi, h, i: (bi, i, h)),
            pl.BlockSpec((1, 2 * QK_DIM, t), lambda bi, h, i: (bi, h, 0)),
            pl.BlockSpec((1, t, V_HEAD), lambda bi, h, i: (bi, 0, h)),
            pl.BlockSpec((1, tq, 2 * tq), lambda bi, h, i: (h, 0, 0)),
            pl.BlockSpec((1, 1, V_HEAD), lambda bi, h, i: (h, 0, 0)),
        ],
        out_specs=pl.BlockSpec((1, tq, V_HEAD), lambda bi, h, i: (bi, i, h)),
        out_shape=jax.ShapeDtypeStruct((b, t, ATT_W), BF16),
        scratch_shapes=[
            pltpu.VMEM((2 * QK_DIM, t), BF16),
            pltpu.VMEM((t, V_HEAD), BF16),
            pltpu.VMEM((2 * tq, V_HEAD), BF16),
            pltpu.VMEM((2 * tq, LANES), F32),
            pltpu.VMEM((2 * tq, LANES), F32),
            pltpu.VMEM((2 * tq, V_HEAD), F32),
        ],
        compiler_params=_params(("arbitrary", "arbitrary", "arbitrary")),
        name="attn_prompt",
    )(lam_rows, q, k, v, bias_tiles, subln_g)


def _attn_sample_kernel(pt_ref, lam_ref, qbd_ref, *refs, n_pages_step, n_steps, n_new, lam_init):
    del pt_ref
    k_refs = refs[:n_pages_step]
    v_refs = refs[n_pages_step:2 * n_pages_step]
    (bias_ref, kn_ref, vn_ref, bn_ref, g_ref, o_ref, m_ref, l_ref, acc_ref) = refs[2 * n_pages_step:]
    step = pl.program_id(1)

    @pl.when(step == 0)
    def _():
        m_ref[...] = jnp.full(m_ref.shape, -jnp.inf, F32)
        l_ref[...] = jnp.zeros(l_ref.shape, F32)
        acc_ref[...] = jnp.zeros(acc_ref.shape, F32)

    qbd = qbd_ref[0]
    rows = 2 * n_new

    def update(k_pages, v_page_refs, bias_last):
        parts = [jnp.dot(qbd, kp.astype(BF16), preferred_element_type=F32) for kp in k_pages]
        parts[-1] = parts[-1] + bias_last
        s = jnp.concatenate(parts, axis=1)
        m_prev = m_ref[...]
        m_new = jnp.maximum(m_prev, jnp.max(s, axis=-1, keepdims=True))
        alpha = jnp.exp2(m_prev - m_new)
        p = jnp.exp2(s - jnp.concatenate([m_new] * len(parts), axis=1))
        l_ref[...] = alpha * l_ref[...] + jnp.sum(p, axis=-1, keepdims=True)
        m_ref[...] = m_new
        pb = p.astype(BF16)
        for h in range(N_HEADS):
            v_h = jnp.concatenate(
                [vr[0, pl.ds(h, PAGE_SIZE, stride=N_HEADS), :].astype(BF16) for vr in v_page_refs], axis=0)
            r = slice(h * rows, (h + 1) * rows)
            acc_ref[r, :] = alpha[r] * acc_ref[r, :] + jnp.dot(pb[r], v_h, preferred_element_type=F32)

    is_last = step == n_steps - 1
    bias_last = jnp.where(is_last, bias_ref[...], jnp.zeros(bias_ref.shape, F32))
    update([kr[0] for kr in k_refs], v_refs, bias_last)

    @pl.when(is_last)
    def _():
        update([kn_ref[0]], [vn_ref], bn_ref[...])
        lam = _lambda(lam_ref, lam_init)
        acc = acc_ref[...] * (1.0 / l_ref[...])
        for h in range(N_HEADS):
            blk = acc[h * rows:(h + 1) * rows]
            o = blk[0:n_new] - lam * blk[n_new:rows]
            o = _rms(o, g_ref[h:h + 1, :], SUBLN_EPS) * (1.0 - lam_init)
            o_ref[0, :, h * V_HEAD:(h + 1) * V_HEAD] = o


def _attn_sample(page_table, qbd, k_pool, v_pool, bias_past, k_new, v_new, bias_new,
                 lam_rows, subln_g, n_pages_step, lam_init):
    nb, n_pages = page_table.shape
    rows = qbd.shape[1]
    n_new = rows // (2 * N_HEADS)
    n_steps = n_pages // n_pages_step
    kern = functools.partial(_attn_sample_kernel, n_pages_step=n_pages_step, n_steps=n_steps,
                             n_new=n_new, lam_init=lam_init)

    def page_spec(g, shape):
        return pl.BlockSpec((1,) + shape, lambda b, s, pt: (pt[b, s * n_pages_step + g], 0, 0))

    grid_spec = pltpu.PrefetchScalarGridSpec(
        num_scalar_prefetch=1,
        grid=(nb, n_steps),
        in_specs=(
            [pl.BlockSpec((4, QK_DIM), lambda b, s, pt: (0, 0)),
             pl.BlockSpec((1, rows, ATT_W), lambda b, s, pt: (b, 0, 0))]
            + [page_spec(g, (ATT_W, PAGE_SIZE)) for g in range(n_pages_step)]
            + [page_spec(g, (PAGE_SIZE * N_HEADS, V_HEAD)) for g in range(n_pages_step)]
            + [pl.BlockSpec((rows, PAGE_SIZE), lambda b, s, pt: (0, 0)),
               pl.BlockSpec((1, ATT_W, PAGE_SIZE), lambda b, s, pt: (b, 0, 0)),
               pl.BlockSpec((1, PAGE_SIZE * N_HEADS, V_HEAD), lambda b, s, pt: (b, 0, 0)),
               pl.BlockSpec((rows, PAGE_SIZE), lambda b, s, pt: (0, 0)),
               pl.BlockSpec((N_HEADS, V_HEAD), lambda b, s, pt: (0, 0))]),
        out_specs=pl.BlockSpec((1, n_new, ATT_W), lambda b, s, pt: (b, 0, 0)),
        scratch_shapes=[
            pltpu.VMEM((rows, LANES), F32),
            pltpu.VMEM((rows, LANES), F32),
            pltpu.VMEM((rows, V_HEAD), F32),
        ],
    )
    return pl.pallas_call(
        kern,
        grid_spec=grid_spec,
        out_shape=jax.ShapeDtypeStruct((nb, n_new, ATT_W), F32),
        compiler_params=_params(("arbitrary", "arbitrary")),
        name="attn_sample",
    )(page_table, lam_rows, qbd, *([k_pool] * n_pages_step), *([v_pool] * n_pages_step),
      bias_past, k_new, v_new, bias_new, subln_g)


STATE_COLS = 2 * STATE_LANES // LANES


def _state_rows(g, tc, col):
    return pl.ds(g * tc * STATE_COLS + col, tc, stride=STATE_COLS)


def _ssm_kernel(u_ref, h0_ref, wb_ref, cb_ref, lam_ref, d_ref, gw_ref, gbias_ref, og_ref,
                o_ref, hf_ref, hb_ref, gbuf_ref, carry_ref, *, tc, chains):
    c = pl.program_id(1)

    @pl.when(c == 0)
    def _():
        carry_ref[...] = h0_ref[0]

    for g in range(N_GROUP_BLOCKS):
        u = u_ref[0, :, g * LANES:(g + 1) * LANES]
        bu = jnp.dot(u.astype(BF16), wb_ref[g], preferred_element_type=F32)
        for k in range(STATE_COLS):
            hb_ref[_state_rows(g, tc, k), :] = bu[:, k * LANES:(k + 1) * LANES]

    coef = [(lam_ref[g, 0], lam_ref[g, 1]) for g in range(N_GROUP_BLOCKS)]

    def step(g, row, h):
        start = (g * tc + row) * STATE_COLS
        if not isinstance(start, int):
            start = pl.multiple_of(start, STATE_COLS)
        rows = pl.ds(start, STATE_COLS)
        a, b = coef[g]
        h = a * h + b * pltpu.roll(h, STATE_COLS // 2, 0) + hb_ref[rows, :]
        hb_ref[rows, :] = h
        return h

    if chains == 1:
        def body(t, hs):
            return tuple(step(g, t, hs[g]) for g in range(N_GROUP_BLOCKS))

        hs = lax.fori_loop(0, tc, body, tuple(carry_ref[g, 0] for g in range(N_GROUP_BLOCKS)),
                           unroll=SUBLANES)
        for g in range(N_GROUP_BLOCKS):
            carry_ref[g, 0] = hs[g]
    else:
        for g in range(N_GROUP_BLOCKS):
            hs = [carry_ref[g, b] for b in range(chains)]
            for row in range(tc):
                hs[row % chains] = step(g, row, hs[row % chains])
            for b in range(chains):
                carry_ref[g, b] = hs[b]

    ssq = jnp.zeros((tc, 1), F32)
    for g in range(N_GROUP_BLOCKS):
        cols = slice(g * LANES, (g + 1) * LANES)
        h = jnp.concatenate([hb_ref[_state_rows(g, tc, k), :] for k in range(STATE_COLS)], axis=1)
        y = jnp.dot(h.astype(BF16), cb_ref[g], preferred_element_type=F32)
        y = y + d_ref[:, cols] * u_ref[0, :, cols]
        a = jax.nn.gelu(y)
        gate = jnp.dot(a.astype(BF16), gw_ref[g], preferred_element_type=F32) + gbias_ref[:, cols]
        a = a * jax.nn.sigmoid(gate)
        gbuf_ref[:, cols] = a
        ssq = ssq + jnp.sum(a * a, axis=-1, keepdims=True)
    inv = lax.rsqrt(ssq * (1.0 / SSM_W) + NORM_EPS)
    o_ref[0] = (gbuf_ref[...] * inv * og_ref[...]).astype(o_ref.dtype)
    hf_ref[0] = carry_ref[...]


def _ssm(u, h0, wb, cb, lam, d, gw, gbias, og, tc, chains, out_dtype):
    ns, t, _ = u.shape
    kern = functools.partial(_ssm_kernel, tc=tc, chains=chains)
    full = lambda *shape: pl.BlockSpec(shape, lambda s, c: (0,) * len(shape))
    state_shape = (N_GROUP_BLOCKS, chains, STATE_COLS, LANES)
    state_spec = pl.BlockSpec((1,) + state_shape, lambda s, c: (s, 0, 0, 0, 0))
    return pl.pallas_call(
        kern,
        grid=(ns, t // tc),
        in_specs=[
            pl.BlockSpec((1, tc, SSM_W), lambda s, c: (s, c, 0)),
            state_spec,
            full(N_GROUP_BLOCKS, LANES, 2 * STATE_LANES),
            full(N_GROUP_BLOCKS, 2 * STATE_LANES, LANES),
            full(N_GROUP_BLOCKS, 2, STATE_COLS, LANES),
            full(1, SSM_W),
            full(N_GROUP_BLOCKS, LANES, LANES),
            full(1, SSM_W),
            full(1, SSM_W),
        ],
        out_specs=[pl.BlockSpec((1, tc, SSM_W), lambda s, c: (s, c, 0)), state_spec],
        out_shape=[
            jax.ShapeDtypeStruct((ns, t, SSM_W), out_dtype),
            jax.ShapeDtypeStruct((ns,) + state_shape, F32),
        ],
        scratch_shapes=[
            pltpu.VMEM((N_GROUP_BLOCKS * tc * STATE_COLS, LANES), F32),
            pltpu.VMEM((tc, SSM_W), F32),
            pltpu.VMEM(state_shape, F32),
        ],
        compiler_params=_params(("arbitrary", "arbitrary")),
        name="ssm_chains%d" % chains,
    )(u, h0, wb, cb, lam, d, gw, gbias, og)


def _out_proj_kernel(x_ref, oa_ref, os_ref, w_ref, o_ref):
    acc = jnp.dot(oa_ref[...].astype(BF16), w_ref[0], preferred_element_type=F32)
    acc = acc + jnp.dot(os_ref[...].astype(BF16), w_ref[1], preferred_element_type=F32)
    o_ref[...] = x_ref[...] + acc


def _out_proj(x, o_att, o_ssm, w, tm):
    m = x.shape[0]
    return pl.pallas_call(
        _out_proj_kernel,
        grid=(m // tm,),
        in_specs=[
            pl.BlockSpec((tm, D_MODEL), lambda i: (i, 0)),
            pl.BlockSpec((tm, ATT_W), lambda i: (i, 0)),
            pl.BlockSpec((tm, SSM_W), lambda i: (i, 0)),
            pl.BlockSpec((2, ATT_W, D_MODEL), lambda i: (0, 0, 0)),
        ],
        out_specs=pl.BlockSpec((tm, D_MODEL), lambda i: (i, 0)),
        out_shape=jax.ShapeDtypeStruct((m, D_MODEL), F32),
        compiler_params=_params(("arbitrary",)),
        name="out_proj",
    )(x, o_att, o_ssm, w)


def _ffn_kernel(h_ref, g_ref, prev_ref, wg_ref, wv_ref, cw_ref, cb_ref, wdn_ref, o_ref, cn_ref,
                fn_ref, acc_ref, halo_ref, *, tm, shift, tiles_per_seq, nf):
    i = pl.program_id(0)
    f = pl.program_id(1)

    @pl.when(f == 0)
    def _():
        fn_ref[...] = _rms(h_ref[...], g_ref[...], NORM_EPS).astype(BF16)
        acc_ref[...] = jnp.zeros(acc_ref.shape, F32)

    fn = fn_ref[...]
    gate = jnp.dot(fn, wg_ref[...], preferred_element_type=F32)
    val = jnp.dot(fn, wv_ref[...], preferred_element_type=F32)

    @pl.when((i % tiles_per_seq) == 0)
    def _():
        halo_ref[f] = prev_ref[0]

    gext = jnp.concatenate([halo_ref[f], gate], axis=0)
    g1 = gext[CONV_HALO - shift:CONV_HALO - shift + tm]
    g2 = gext[CONV_HALO - 2 * shift:CONV_HALO - 2 * shift + tm]
    conv = cb_ref[...] + (cw_ref[0:1, :] * g2 + cw_ref[1:2, :] * g1 + cw_ref[2:3, :] * gate)
    hact = (conv * jax.nn.sigmoid(conv)) * val
    acc_ref[...] += jnp.dot(hact.astype(BF16), wdn_ref[...], preferred_element_type=F32)
    halo_ref[f] = gate[tm - CONV_HALO:tm]
    cn_ref[0] = gate[tm - 2 * shift:tm]

    @pl.when(f == nf - 1)
    def _():
        o_ref[...] = h_ref[...] + acc_ref[...]


def _ffn(h, g, prev, wg, wv, cw, cb, wdn, tm, shift, tiles_per_seq):
    m = h.shape[0]
    nf = D_FF_PAD // FFN_TF
    kern = functools.partial(_ffn_kernel, tm=tm, shift=shift, tiles_per_seq=tiles_per_seq, nf=nf)
    return pl.pallas_call(
        kern,
        grid=(m // tm, nf),
        in_specs=[
            pl.BlockSpec((tm, D_MODEL), lambda i, f: (i, 0)),
            pl.BlockSpec((1, D_MODEL), lambda i, f: (0, 0)),
            pl.BlockSpec((1, CONV_HALO, FFN_TF), lambda i, f: (i // tiles_per_seq, 0, f)),
            pl.BlockSpec((D_MODEL, FFN_TF), lambda i, f: (0, f)),
            pl.BlockSpec((D_MODEL, FFN_TF), lambda i, f: (0, f)),
            pl.BlockSpec((CONV_W, FFN_TF), lambda i, f: (0, f)),
            pl.BlockSpec((1, FFN_TF), lambda i, f: (0, f)),
            pl.BlockSpec((FFN_TF, D_MODEL), lambda i, f: (f, 0)),
        ],
        out_specs=[
            pl.BlockSpec((tm, D_MODEL), lambda i, f: (i, 0)),
            pl.BlockSpec((1, 2 * shift, FFN_TF), lambda i, f: (i, 0, f)),
        ],
        out_shape=[
            jax.ShapeDtypeStruct((m, D_MODEL), F32),
            jax.ShapeDtypeStruct((m // tm, 2 * shift, D_FF_PAD), F32),
        ],
        scratch_shapes=[
            pltpu.VMEM((tm, D_MODEL), BF16),
            pltpu.VMEM((tm, D_MODEL), F32),
            pltpu.VMEM((nf, CONV_HALO, FFN_TF), F32),
        ],
        compiler_params=_params(("arbitrary", "arbitrary")),
        name="conv_ffn",
    )(h, g, prev, wg, wv, cw, cb, wdn)


def _ple_kernel(h_ref, gp_ref, pe_ref, wg_ref, wp_ref, gf_ref, o_ref):
    h = h_ref[...]
    hn = _rms(h, gp_ref[...], NORM_EPS).astype(BF16)
    gate = jax.nn.sigmoid(jnp.dot(hn, wg_ref[...], preferred_element_type=F32))
    proj = jnp.dot(pe_ref[...].astype(BF16), wp_ref[...], preferred_element_type=F32)
    o_ref[...] = _rms(h + gate * proj, gf_ref[...], NORM_EPS)


def _ple(h, gp, pe, wg, wp, gf, tm):
    m = h.shape[0]
    return pl.pallas_call(
        _ple_kernel,
        grid=(m // tm,),
        in_specs=[
            pl.BlockSpec((tm, D_MODEL), lambda i: (i, 0)),
            pl.BlockSpec((1, D_MODEL), lambda i: (0, 0)),
            pl.BlockSpec((tm, PLE_DIM), lambda i: (i, 0)),
            pl.BlockSpec((D_MODEL, D_MODEL), lambda i: (0, 0)),
            pl.BlockSpec((PLE_DIM, D_MODEL), lambda i: (0, 0)),
            pl.BlockSpec((1, D_MODEL), lambda i: (0, 0)),
        ],
        out_specs=pl.BlockSpec((tm, D_MODEL), lambda i: (i, 0)),
        out_shape=jax.ShapeDtypeStruct((m, D_MODEL), F32),
        compiler_params=_params(("arbitrary",)),
        name="ple_final",
    )(h, gp, pe, wg, wp, gf)


def _t5_bucket(dist):
    max_exact = N_BUCKETS // 2
    n = jnp.maximum(dist, 0)
    nf = jnp.maximum(n, 1).astype(F32)
    large = max_exact + (jnp.log(nf / max_exact) / math.log(MAX_DISTANCE / max_exact)
                         * (N_BUCKETS - max_exact)).astype(jnp.int32)
    large = jnp.minimum(large, N_BUCKETS - 1)
    return jnp.where(n < max_exact, n, large)


def _toeplitz(w, n):
    lead = w.shape[:-1]
    flat = jnp.tile(w, (1,) * len(lead) + (n,))[..., :n * (2 * n - 1)]
    return flat.reshape(lead + (n, 2 * n - 1))[..., :n]


def _prompt_bias_tiles(table, n):
    assert n >= MAX_DISTANCE
    far = table[N_BUCKETS - 1]
    bvec = (table[_t5_bucket(jnp.arange(2 * n, dtype=jnp.int32))] - far[None, :]).T
    bvec = bvec * LOG2E
    d = jnp.arange(2 * n, dtype=jnp.int32)
    d = jnp.where(d >= n, d - 2 * n, d)
    diag = jnp.where(d[None, :] <= 0, bvec[:, jnp.abs(d)], -jnp.inf)
    sub = bvec[:, jnp.clip(n - d, 0, 2 * n - 1)]
    return jnp.concatenate([_toeplitz(sub, n), _toeplitz(diag, n)], axis=-1)


def _sample_bias(table, n_new):
    assert PAGE_SIZE >= MAX_DISTANCE
    far = table[N_BUCKETS - 1]
    t = jnp.arange(n_new, dtype=jnp.int32)
    col = jnp.arange(PAGE_SIZE, dtype=jnp.int32)

    def rows(b):
        b = jnp.transpose(b * LOG2E, (2, 0, 1))
        return jnp.broadcast_to(b[:, None], (N_HEADS, 2, n_new, PAGE_SIZE)).reshape(-1, PAGE_SIZE)

    dist_last = PAGE_SIZE + t[:, None] - col[None, :]
    dist_new = t[:, None] - col[None, :]
    b_new = jnp.where((dist_new >= 0)[..., None], table[_t5_bucket(dist_new)] - far, -jnp.inf)
    return rows(table[_t5_bucket(dist_last)] - far), rows(b_new)


def _block_diag(w):
    nb, gp, a, b = w.shape
    eye = jnp.eye(gp, dtype=w.dtype)
    out = w[:, :, :, None, :] * eye[None, :, None, :, None]
    return out.reshape(nb, gp * a, gp * b)


def _ssm_tables(a_re, a_im, b_re, b_im, c_re, c_im, log_dt):
    g, n, gc = N_SSM_GROUPS, SSM_STATE, SSM_GROUP
    dt = jnp.exp(log_dt)[:, None]
    decay = jnp.exp(a_re * dt)
    ab_re = decay * jnp.cos(a_im * dt)
    ab_im = decay * jnp.sin(a_im * dt)
    den = a_re * a_re + a_im * a_im
    num_re = ab_re - 1.0
    coef_re = (num_re * a_re + ab_im * a_im) / den
    coef_im = (ab_im * a_re - num_re * a_im) / den
    bb_re = coef_re[..., None] * b_re - coef_im[..., None] * b_im
    bb_im = coef_re[..., None] * b_im + coef_im[..., None] * b_re
    blk = lambda w: w.reshape((N_GROUP_BLOCKS, GROUPS_PER_BLOCK) + w.shape[1:])
    to_cn = lambda w: jnp.swapaxes(blk(w), 2, 3)
    wb = jnp.concatenate([_block_diag(to_cn(bb_re)), _block_diag(to_cn(bb_im))], axis=2)
    to_nc = lambda w: jnp.swapaxes(blk(w), 2, 3)
    cb = jnp.concatenate([_block_diag(to_nc(c_re)), _block_diag(to_nc(-c_im))], axis=1)
    lr = ab_re.reshape(N_GROUP_BLOCKS, STATE_COLS // 2, LANES)
    li = ab_im.reshape(N_GROUP_BLOCKS, STATE_COLS // 2, LANES)
    lam = jnp.stack([jnp.concatenate([lr, lr], axis=1), jnp.concatenate([-li, li], axis=1)], axis=1)
    return wb.astype(BF16), cb.astype(BF16), lam.astype(F32)


def _pad_ff(w, axis):
    pad = [(0, 0)] * w.ndim
    pad[axis] = (0, D_FF_PAD - D_FF)
    return jnp.pad(w, pad)


def _state_to_tiles(re, im):
    shape = (re.shape[0], N_GROUP_BLOCKS, STATE_COLS // 2, LANES)
    return jnp.concatenate([re.reshape(shape), im.reshape(shape)], axis=2)


def _tiles_to_state(tiles):
    half = STATE_COLS // 2
    shape = (1, tiles.shape[0], N_SSM_GROUPS, SSM_STATE)
    return tiles[:, :, :half].reshape(shape), tiles[:, :, half:].reshape(shape)


def kernel(x_prompt, x_sample, cache_k, cache_v, state_ssm_re, state_ssm_im, state_conv, page_table, p_prompt, p_sample, rel_bias, norm_mix_g, w_in, lambda_q1, lambda_k1, lambda_q2, lambda_k2, attn_subln_g, ssm_a_re, ssm_a_im, ssm_b_re, ssm_b_im, ssm_c_re, ssm_c_im, ssm_d, ssm_log_dt, ssm_glu_w, ssm_glu_b, ssm_out_g, w_out, norm_ffn_g, ffn_w_up, ffn_conv_w, ffn_conv_b, ffn_w_down, norm_ple_g, ple_w_gate, ple_w_proj, norm_final_g):
    assert DEPTH == 1
    li = 0
    lam_init = 0.8 - 0.6 * math.exp(-0.3 * li)
    nb, t, _ = x_prompt.shape
    db, ds, _ = x_sample.shape
    assert db == SUBLANES
    m_p = nb * t
    m_s = db * ds

    row = lambda v: v.reshape(1, -1).astype(F32)
    w_in_b = w_in[li].astype(BF16)
    w_kt_b = w_in_b[:, ATT_W:2 * ATT_W].T
    w_out_b = w_out[li].astype(BF16).reshape(2, ATT_W, D_MODEL)
    w_ug_b = _pad_ff(ffn_w_up[li][:, :D_FF].astype(BF16), 1)
    w_uv_b = _pad_ff(ffn_w_up[li][:, D_FF:].astype(BF16), 1)
    w_dn_b = _pad_ff(ffn_w_down[li].astype(BF16), 0)
    conv_w = _pad_ff(ffn_conv_w[li], 1)
    conv_b = _pad_ff(ffn_conv_b[li].reshape(1, D_FF), 1)
    w_gate_b = ple_w_gate[li].astype(BF16)
    w_proj_b = ple_w_proj[li].astype(BF16)
    lam_rows = jnp.stack([lambda_q1[li], lambda_k1[li], lambda_q2[li], lambda_k2[li]]).astype(F32)
    subln_g = attn_subln_g[li].astype(F32)
    wb, cb, tbl = _ssm_tables(ssm_a_re[li], ssm_a_im[li], ssm_b_re[li], ssm_b_im[li],
                              ssm_c_re[li], ssm_c_im[li], ssm_log_dt[li])
    glu_w = _block_diag(ssm_glu_w[li].reshape(N_GROUP_BLOCKS, GROUPS_PER_BLOCK, SSM_GROUP, SSM_GROUP)).astype(BF16)
    ssm_args = (wb, cb, tbl, row(ssm_d[li]), glu_w, row(ssm_glu_b[li]), row(ssm_out_g[li]))

    tq = 512
    xp = x_prompt.reshape(m_p, D_MODEL)
    q_p, kt_p, v_p, u_p = _in_proj(xp, row(norm_mix_g[li]), w_in_b, w_kt_b, tm=512,
                                   rows_per_seq=t, k_transposed=True)
    shp = lambda a: a.reshape(nb, t, -1)
    bias_tiles = _prompt_bias_tiles(rel_bias.astype(F32), tq)
    o_att_p = _attn_prompt(shp(q_p), kt_p, shp(v_p), bias_tiles, lam_rows,
                           subln_g.reshape(N_HEADS, 1, V_HEAD), tq, lam_init)
    zero_state = jnp.zeros((nb, N_GROUP_BLOCKS, 1, STATE_COLS, LANES), F32)
    o_ssm_p, hf_p = _ssm(shp(u_p), zero_state, *ssm_args, tc=256, chains=1, out_dtype=BF16)
    h1_p = _out_proj(xp, o_att_p.reshape(m_p, ATT_W), o_ssm_p.reshape(m_p, SSM_W), w_out_b, tm=512)
    tm_ffn = 512
    zero_prev = jnp.zeros((nb, CONV_HALO, D_FF_PAD), F32)
    h2_p, conv_p = _ffn(h1_p, row(norm_ffn_g[li]), zero_prev, w_ug_b, w_uv_b, conv_w, conv_b, w_dn_b,
                        tm=tm_ffn, shift=1, tiles_per_seq=t // tm_ffn)
    y_p = _ple(h2_p, row(norm_ple_g[li]), p_prompt[li].reshape(m_p, PLE_DIM), w_gate_b, w_proj_b,
               row(norm_final_g), tm=512)

    tmaj = lambda a: jnp.swapaxes(a, 0, 1).reshape(m_s, -1)
    bmaj = lambda a: jnp.swapaxes(a.reshape(ds, db, -1), 0, 1)
    xs = tmaj(x_sample)
    q_s, k_s, v_s, u_s = _in_proj(xs, row(norm_mix_g[li]), w_in_b, w_kt_b, tm=m_s,
                                  rows_per_seq=m_s, k_transposed=False)
    q_b = bmaj(q_s)
    q_rep = jnp.tile(q_b, (1, 2 * N_HEADS, 1)).reshape(db, 2 * N_HEADS, ds, ATT_W)
    sel = (jnp.arange(ATT_W)[None, :] // QK_DIM) == jnp.arange(2 * N_HEADS)[:, None]
    qbd = jnp.where(sel[None, :, None, :], q_rep, jnp.zeros_like(q_rep)).reshape(db, 2 * N_HEADS * ds, ATT_W)
    pad_rows = lambda a: jnp.pad(bmaj(a), ((0, 0), (0, PAGE_SIZE - ds), (0, 0)))
    bias_past, bias_new = _sample_bias(rel_bias.astype(F32), ds)
    n_pool = cache_k.shape[1]
    k_pool_t = jnp.transpose(cache_k[li], (0, 2, 3, 4, 1)).reshape(n_pool, ATT_W, PAGE_SIZE)
    k_new_t = jnp.swapaxes(pad_rows(k_s), 1, 2)
    v_pool_r = cache_v[li].reshape(n_pool, PAGE_SIZE * N_HEADS, V_HEAD)
    v_new_r = pad_rows(v_s).reshape(db, PAGE_SIZE * N_HEADS, V_HEAD)
    o_att_s = _attn_sample(page_table, qbd, k_pool_t, v_pool_r,
                           bias_past, k_new_t, v_new_r, bias_new,
                           lam_rows, subln_g, n_pages_step=8, lam_init=lam_init)
    h0_s = jnp.swapaxes(_state_to_tiles(state_ssm_re[li], state_ssm_im[li]), 0, 1)[None]
    o_ssm_s, hf_s = _ssm(u_s[None], h0_s, *ssm_args, tc=m_s, chains=db, out_dtype=F32)
    h1_s = _out_proj(xs, tmaj(o_att_s), o_ssm_s[0], w_out_b, tm=m_s)
    prev_s = _pad_ff(jnp.swapaxes(state_conv[li], 0, 1).reshape(1, 2 * db, D_FF), 2)
    h2_s, conv_s = _ffn(h1_s, row(norm_ffn_g[li]), prev_s, w_ug_b, w_uv_b, conv_w, conv_b, w_dn_b,
                        tm=m_s, shift=db, tiles_per_seq=1)
    y_s = _ple(h2_s, row(norm_ple_g[li]), tmaj(p_sample[li]), w_gate_b, w_proj_b,
               row(norm_final_g), tm=m_s)

    re_p, im_p = _tiles_to_state(hf_p[:, :, 0])
    re_s, im_s = _tiles_to_state(jnp.swapaxes(hf_s[0], 0, 1))
    tiles = t // tm_ffn
    conv_p = conv_p[tiles - 1::tiles][None, :, :, :D_FF]
    conv_s = jnp.swapaxes(conv_s.reshape(2, db, D_FF_PAD), 0, 1)[None, :, :, :D_FF]
    return (y_p.reshape(nb, t, D_MODEL),
            bmaj(y_s),
            jnp.transpose(kt_p.reshape(1, nb, N_HEADS, 2, QK_DIM, t), (0, 1, 5, 2, 3, 4)),
            v_p.reshape(1, nb, t, N_HEADS, V_HEAD),
            re_p, im_p, conv_p,
            bmaj(k_s).reshape(1, db, ds, N_HEADS, 2, QK_DIM),
            bmaj(v_s).reshape(1, db, ds, N_HEADS, V_HEAD),
            re_s, im_s, conv_s)
```

```python
import functools
import math

import jax
import jax.numpy as jnp
from jax import lax
from jax.experimental import pallas as pl
from jax.experimental.pallas import tpu as pltpu

F32 = jnp.float32
BF16 = jnp.bfloat16

D_MODEL = 2048
DEPTH = 1
PAGE_SIZE = 128
ATT_W = 1024
SSM_W = 1024
V_HEAD = 128
N_HEADS = 8
QK_DIM = 64
SSM_GROUP = 16
N_SSM_GROUPS = 64
SSM_STATE = 64
D_FF = 5504
CONV_W = 3
PLE_DIM = 256
N_BUCKETS = 32
MAX_DISTANCE = 128
NORM_EPS = 1e-6
SUBLN_EPS = 1e-5
LOG2E = math.log2(math.e)
QK_SCALE = QK_DIM ** -0.5 * LOG2E

LANES = 128
SUBLANES = 8
MXU_WIDTH = 256
VMEM_LIMIT = 56 * 1024 * 1024

GROUPS_PER_BLOCK = LANES // SSM_GROUP
N_GROUP_BLOCKS = N_SSM_GROUPS // GROUPS_PER_BLOCK
STATE_LANES = GROUPS_PER_BLOCK * SSM_STATE

D_FF_PAD = 5632
FFN_TF = 512
CONV_HALO = 16


def _params(sem):
    return pltpu.CompilerParams(dimension_semantics=sem, vmem_limit_bytes=VMEM_LIMIT)


def _rms(x, g, eps):
    return x * lax.rsqrt(jnp.mean(x * x, axis=-1, keepdims=True) + eps) * g


def _in_proj_kernel(x_ref, g_ref, w_ref, wkt_ref, q_ref, k_ref, v_ref, u_ref, xn_ref, *, k_transposed):
    j = pl.program_id(1)

    @pl.when(j == 0)
    def _():
        xn_ref[...] = _rms(x_ref[...], g_ref[...], NORM_EPS).astype(BF16)

    def proj():
        return jnp.dot(xn_ref[...], w_ref[...], preferred_element_type=F32)

    @pl.when(j == 0)
    def _():
        q_ref[...] = (proj() * QK_SCALE).astype(BF16)

    @pl.when(j == 1)
    def _():
        if k_transposed:
            k_ref[0] = lax.dot_general(wkt_ref[...], xn_ref[...], (((1,), (1,)), ((), ())),
                                       preferred_element_type=F32)
        else:
            k_ref[...] = proj()

    @pl.when(j == 2)
    def _():
        v_ref[...] = proj()

    @pl.when(j == 3)
    def _():
        u_ref[...] = proj()


def _in_proj(x, g, w, wkt, tm, rows_per_seq, k_transposed):
    m = x.shape[0]
    out_spec = pl.BlockSpec((tm, ATT_W), lambda i, j: (i, 0))
    if k_transposed:
        tiles = rows_per_seq // tm
        k_spec = pl.BlockSpec((1, ATT_W, tm), lambda i, j: (i // tiles, 0, i % tiles))
        k_shape = jax.ShapeDtypeStruct((m // rows_per_seq, ATT_W, rows_per_seq), F32)
        w_spec = pl.BlockSpec((D_MODEL, ATT_W), lambda i, j: (0, jnp.where(j == 1, 0, j)))
    else:
        k_spec = out_spec
        k_shape = jax.ShapeDtypeStruct((m, ATT_W), F32)
        w_spec = pl.BlockSpec((D_MODEL, ATT_W), lambda i, j: (0, j))
    return pl.pallas_call(
        functools.partial(_in_proj_kernel, k_transposed=k_transposed),
        grid=(m // tm, 4),
        in_specs=[
            pl.BlockSpec((tm, D_MODEL), lambda i, j: (i, 0)),
            pl.BlockSpec((1, D_MODEL), lambda i, j: (0, 0)),
            w_spec,
            pl.BlockSpec((ATT_W, D_MODEL), lambda i, j: (0, 0)),
        ],
        out_specs=[out_spec, k_spec, out_spec, out_spec],
        out_shape=[
            jax.ShapeDtypeStruct((m, ATT_W), BF16),
            k_shape,
            jax.ShapeDtypeStruct((m, ATT_W), F32),
            jax.ShapeDtypeStruct((m, SSM_W), F32),
        ],
        scratch_shapes=[pltpu.VMEM((tm, D_MODEL), BF16)],
        compiler_params=_params(("arbitrary", "arbitrary")),
        name="in_proj",
    )(x, g, w, wkt)


def _lambda(lam_ref, lam_init):
    s1 = jnp.sum(lam_ref[0:1, :] * lam_ref[1:2, :], axis=-1, keepdims=True)
    s2 = jnp.sum(lam_ref[2:3, :] * lam_ref[3:4, :], axis=-1, keepdims=True)
    return jnp.exp(s1) - jnp.exp(s2) + lam_init


def _online_softmax_step(s, v_blk, m_ref, l_ref, acc_ref):
    m_prev = m_ref[...]
    m_new = jnp.maximum(m_prev, jnp.max(s, axis=-1, keepdims=True))
    alpha = jnp.exp2(m_prev - m_new)
    p = jnp.exp2(s - jnp.concatenate([m_new] * (s.shape[1] // LANES), axis=1))
    l_ref[...] = alpha * l_ref[...] + jnp.sum(p, axis=-1, keepdims=True)
    acc_ref[...] = alpha * acc_ref[...] + jnp.dot(
        p.astype(BF16), v_blk, preferred_element_type=F32)
    m_ref[...] = m_new


def _attn_prompt_kernel(lam_ref, q_ref, k_ref, v_ref, band_ref, g_ref, o_ref,
                        kb_ref, vb_ref, qs_ref, bias_ref, m_ref, l_ref, acc_ref, *, tq, lam_init):
    i = pl.program_id(2)

    @pl.when(i == 0)
    def _():
        kb_ref[...] = k_ref[0].astype(BF16)
        vb_ref[...] = v_ref[0].astype(BF16)
        w = band_ref.shape[1]
        left, right = band_ref[0, :, 0:w], band_ref[0, :, w:2 * w]
        zero = jnp.zeros((w, w), F32)
        masked = jnp.full((w, w), -jnp.inf, F32)
        n = tq // w
        for r in range(n):
            for c in range(n):
                rows = slice(r * w, (r + 1) * w)
                bias_ref[rows, c * w:(c + 1) * w] = left if (r == 0 and c == n - 1) else zero
                bias_ref[rows, tq + c * w:tq + (c + 1) * w] = (
                    right if c == r else left if c == r - 1 else masked if c > r else zero)

    q = q_ref[0]
    lane = lax.broadcasted_iota(jnp.int32, q.shape, 1)
    zero = jnp.zeros_like(q)
    qs_ref[0:tq, :] = jnp.where(lane < QK_DIM, q, zero)
    qs_ref[tq:2 * tq, :] = jnp.where(lane >= QK_DIM, q, zero)
    m_ref[...] = jnp.full(m_ref.shape, -jnp.inf, F32)
    l_ref[...] = jnp.zeros(l_ref.shape, F32)
    acc_ref[...] = jnp.zeros(acc_ref.shape, F32)

    def update(j, width, bias):
        off = pl.multiple_of(j * tq, tq)
        s = jnp.dot(qs_ref[...], kb_ref[:, pl.ds(off, width)], preferred_element_type=F32)
        if bias is not None:
            s = (s.reshape(2, tq, width) + bias[None]).reshape(2 * tq, width)
        _online_softmax_step(s, vb_ref[pl.ds(off, width), :], m_ref, l_ref, acc_ref)

    def far_body(j, carry):
        update(j, tq, None)
        return carry

    lax.fori_loop(0, jnp.maximum(i - 1, 0), far_body, 0)

    @pl.when(i >= 1)
    def _():
        update(i - 1, 2 * tq, bias_ref[...])

    @pl.when(i == 0)
    def _():
        update(0, tq, bias_ref[:, tq:2 * tq])

    inv_l = 1.0 / l_ref[...]
    lam = _lambda(lam_ref, lam_init)
    o = acc_ref[0:tq, :] * inv_l[0:tq] - lam * (acc_ref[tq:2 * tq, :] * inv_l[tq:2 * tq])
    o_ref[0] = (_rms(o, g_ref[0], SUBLN_EPS) * (1.0 - lam_init)).astype(o_ref.dtype)


def _attn_prompt(q, k, v, band, lam_rows, subln_g, tq, lam_init):
    b, t, _ = q.shape
    w = band.shape[1]
    assert tq % w == 0 and band.shape[2] == 2 * w
    kern = functools.partial(_attn_prompt_kernel, tq=tq, lam_init=lam_init)
    return pl.pallas_call(
        kern,
        grid=(b, N_HEADS, t // tq),
        in_specs=[
            pl.BlockSpec((4, QK_DIM), lambda bi, h, i: (0, 0)),
            pl.BlockSpec((1, tq, V_HEAD), lambda bi, h, i: (bi, i, h)),
            pl.BlockSpec((1, 2 * QK_DIM, t), lambda bi, h, i: (bi, h, 0)),
            pl.BlockSpec((1, t, V_HEAD), lambda bi, h, i: (bi, 0, h)),
            pl.BlockSpec((1, w, 2 * w), lambda bi, h, i: (h, 0, 0)),
            pl.BlockSpec((1, 1, V_HEAD), lambda bi, h, i: (h, 0, 0)),
        ],
        out_specs=pl.BlockSpec((1, tq, V_HEAD), lambda bi, h, i: (bi, i, h)),
        out_shape=jax.ShapeDtypeStruct((b, t, ATT_W), BF16),
        scratch_shapes=[
            pltpu.VMEM((2 * QK_DIM, t), BF16),
            pltpu.VMEM((t, V_HEAD), BF16),
            pltpu.VMEM((2 * tq, V_HEAD), BF16),
            pltpu.VMEM((tq, 2 * tq), F32),
            pltpu.VMEM((2 * tq, LANES), F32),
            pltpu.VMEM((2 * tq, LANES), F32),
            pltpu.VMEM((2 * tq, V_HEAD), F32),
        ],
        compiler_params=_params(("arbitrary", "arbitrary", "arbitrary")),
        name="attn_prompt",
    )(lam_rows, q, k, v, band, subln_g)


def _attn_sample_kernel(pt_ref, lam_ref, qbd_ref, *refs, n_pages_step, n_steps, n_new, lam_init):
    del pt_ref
    k_refs = refs[:n_pages_step]
    v_refs = refs[n_pages_step:2 * n_pages_step]
    (bias_ref, kn_ref, vn_ref, bn_ref, g_ref, o_ref, m_ref, l_ref, acc_ref) = refs[2 * n_pages_step:]
    step = pl.program_id(1)

    @pl.when(step == 0)
    def _():
        m_ref[...] = jnp.full(m_ref.shape, -jnp.inf, F32)
        l_ref[...] = jnp.zeros(l_ref.shape, F32)
        acc_ref[...] = jnp.zeros(acc_ref.shape, F32)

    qbd = qbd_ref[0]
    rows = 2 * n_new

    def update(k_pages, v_page_refs, bias_last):
        parts = [jnp.dot(qbd, kp.astype(BF16), preferred_element_type=F32) for kp in k_pages]
        parts[-1] = parts[-1] + bias_last
        s = jnp.concatenate(parts, axis=1)
        m_prev = m_ref[...]
        m_new = jnp.maximum(m_prev, jnp.max(s, axis=-1, keepdims=True))
        alpha = jnp.exp2(m_prev - m_new)
        p = jnp.exp2(s - jnp.concatenate([m_new] * len(parts), axis=1))
        l_ref[...] = alpha * l_ref[...] + jnp.sum(p, axis=-1, keepdims=True)
        m_ref[...] = m_new
        pb = p.astype(BF16)
        for h in range(N_HEADS):
            v_h = jnp.concatenate(
                [vr[0, pl.ds(h, PAGE_SIZE, stride=N_HEADS), :].astype(BF16) for vr in v_page_refs], axis=0)
            r = slice(h * rows, (h + 1) * rows)
            acc_ref[r, :] = alpha[r] * acc_ref[r, :] + jnp.dot(pb[r], v_h, preferred_element_type=F32)

    is_last = step == n_steps - 1
    bias_last = jnp.where(is_last, bias_ref[...], jnp.zeros(bias_ref.shape, F32))
    update([kr[0] for kr in k_refs], v_refs, bias_last)

    @pl.when(is_last)
    def _():
        update([kn_ref[0]], [vn_ref], bn_ref[...])
        lam = _lambda(lam_ref, lam_init)
        acc = acc_ref[...] * (1.0 / l_ref[...])
        for h in range(N_HEADS):
            blk = acc[h * rows:(h + 1) * rows]
            o = blk[0:n_new] - lam * blk[n_new:rows]
            o = _rms(o, g_ref[h:h + 1, :], SUBLN_EPS) * (1.0 - lam_init)
            o_ref[0, :, h * V_HEAD:(h + 1) * V_HEAD] = o


def _attn_sample(page_table, qbd, k_pool, v_pool, bias_past, k_new, v_new, bias_new,
                 lam_rows, subln_g, n_pages_step, lam_init):
    nb, n_pages = page_table.shape
    rows = qbd.shape[1]
    n_new = rows // (2 * N_HEADS)
    n_steps = n_pages // n_pages_step
    kern = functools.partial(_attn_sample_kernel, n_pages_step=n_pages_step, n_steps=n_steps,
                             n_new=n_new, lam_init=lam_init)

    def page_spec(g, shape):
        return pl.BlockSpec((1,) + shape, lambda b, s, pt: (pt[b, s * n_pages_step + g], 0, 0))

    grid_spec = pltpu.PrefetchScalarGridSpec(
        num_scalar_prefetch=1,
        grid=(nb, n_steps),
        in_specs=(
            [pl.BlockSpec((4, QK_DIM), lambda b, s, pt: (0, 0)),
             pl.BlockSpec((1, rows, ATT_W), lambda b, s, pt: (b, 0, 0))]
            + [page_spec(g, (ATT_W, PAGE_SIZE)) for g in range(n_pages_step)]
            + [page_spec(g, (PAGE_SIZE * N_HEADS, V_HEAD)) for g in range(n_pages_step)]
            + [pl.BlockSpec((rows, PAGE_SIZE), lambda b, s, pt: (0, 0)),
               pl.BlockSpec((1, ATT_W, PAGE_SIZE), lambda b, s, pt: (b, 0, 0)),
               pl.BlockSpec((1, PAGE_SIZE * N_HEADS, V_HEAD), lambda b, s, pt: (b, 0, 0)),
               pl.BlockSpec((rows, PAGE_SIZE), lambda b, s, pt: (0, 0)),
               pl.BlockSpec((N_HEADS, V_HEAD), lambda b, s, pt: (0, 0))]),
        out_specs=pl.BlockSpec((1, n_new, ATT_W), lambda b, s, pt: (b, 0, 0)),
        scratch_shapes=[
            pltpu.VMEM((rows, LANES), F32),
            pltpu.VMEM((rows, LANES), F32),
            pltpu.VMEM((rows, V_HEAD), F32),
        ],
    )
    return pl.pallas_call(
        kern,
        grid_spec=grid_spec,
        out_shape=jax.ShapeDtypeStruct((nb, n_new, ATT_W), F32),
        compiler_params=_params(("arbitrary", "arbitrary")),
        name="attn_sample",
    )(page_table, lam_rows, qbd, *([k_pool] * n_pages_step), *([v_pool] * n_pages_step),
      bias_past, k_new, v_new, bias_new, subln_g)


STATE_COLS = 2 * STATE_LANES // LANES


def _state_rows(g, tc, col):
    return pl.ds(g * tc * STATE_COLS + col, tc, stride=STATE_COLS)


def _ssm_kernel(u_ref, h0_ref, wb_ref, cb_ref, lam_ref, d_ref, gw_ref, gbias_ref, og_ref,
                o_ref, hf_ref, hb_ref, gbuf_ref, carry_ref, *, tc, chains):
    c = pl.program_id(1)

    @pl.when(c == 0)
    def _():
        carry_ref[...] = h0_ref[0]

    for g in range(N_GROUP_BLOCKS):
        u = u_ref[0, :, g * LANES:(g + 1) * LANES]
        bu = jnp.dot(u.astype(BF16), wb_ref[g], preferred_element_type=F32)
        for k in range(STATE_COLS):
            hb_ref[_state_rows(g, tc, k), :] = bu[:, k * LANES:(k + 1) * LANES]

    coef = [(lam_ref[g, 0], lam_ref[g, 1]) for g in range(N_GROUP_BLOCKS)]

    def step(g, row, h):
        start = (g * tc + row) * STATE_COLS
        if not isinstance(start, int):
            start = pl.multiple_of(start, STATE_COLS)
        rows = pl.ds(start, STATE_COLS)
        a, b = coef[g]
        h = a * h + b * pltpu.roll(h, STATE_COLS // 2, 0) + hb_ref[rows, :]
        hb_ref[rows, :] = h
        return h

    if chains == 1:
        def body(t, hs):
            return tuple(step(g, t, hs[g]) for g in range(N_GROUP_BLOCKS))

        hs = lax.fori_loop(0, tc, body, tuple(carry_ref[g, 0] for g in range(N_GROUP_BLOCKS)),
                           unroll=SUBLANES)
        for g in range(N_GROUP_BLOCKS):
            carry_ref[g, 0] = hs[g]
    else:
        for g in range(N_GROUP_BLOCKS):
            hs = [carry_ref[g, b] for b in range(chains)]
            for row in range(tc):
                hs[row % chains] = step(g, row, hs[row % chains])
            for b in range(chains):
                carry_ref[g, b] = hs[b]

    ssq = jnp.zeros((tc, 1), F32)
    for g in range(N_GROUP_BLOCKS):
        cols = slice(g * LANES, (g + 1) * LANES)
        h = jnp.concatenate([hb_ref[_state_rows(g, tc, k), :] for k in range(STATE_COLS)], axis=1)
        y = jnp.dot(h.astype(BF16), cb_ref[g], preferred_element_type=F32)
        y = y + d_ref[:, cols] * u_ref[0, :, cols]
        a = jax.nn.gelu(y)
        gate = jnp.dot(a.astype(BF16), gw_ref[g], preferred_element_type=F32) + gbias_ref[:, cols]
        a = a * jax.nn.sigmoid(gate)
        gbuf_ref[:, cols] = a
        ssq = ssq + jnp.sum(a * a, axis=-1, keepdims=True)
    inv = lax.rsqrt(ssq * (1.0 / SSM_W) + NORM_EPS)
    o_ref[0] = (gbuf_ref[...] * inv * og_ref[...]).astype(o_ref.dtype)
    hf_ref[0] = carry_ref[...]


def _ssm(u, h0, wb, cb, lam, d, gw, gbias, og, tc, chains, out_dtype):
    ns, t, _ = u.shape
    kern = functools.partial(_ssm_kernel, tc=tc, chains=chains)
    full = lambda *shape: pl.BlockSpec(shape, lambda s, c: (0,) * len(shape))
    state_shape = (N_GROUP_BLOCKS, chains, STATE_COLS, LANES)
    state_spec = pl.BlockSpec((1,) + state_shape, lambda s, c: (s, 0, 0, 0, 0))
    return pl.pallas_call(
        kern,
        grid=(ns, t // tc),
        in_specs=[
            pl.BlockSpec((1, tc, SSM_W), lambda s, c: (s, c, 0)),
            state_spec,
            full(N_GROUP_BLOCKS, LANES, 2 * STATE_LANES),
            full(N_GROUP_BLOCKS, 2 * STATE_LANES, LANES),
            full(N_GROUP_BLOCKS, 2, STATE_COLS, LANES),
            full(1, SSM_W),
            full(N_GROUP_BLOCKS, LANES, LANES),
            full(1, SSM_W),
            full(1, SSM_W),
        ],
        out_specs=[pl.BlockSpec((1, tc, SSM_W), lambda s, c: (s, c, 0)), state_spec],
        out_shape=[
            jax.ShapeDtypeStruct((ns, t, SSM_W), out_dtype),
            jax.ShapeDtypeStruct((ns,) + state_shape, F32),
        ],
        scratch_shapes=[
            pltpu.VMEM((N_GROUP_BLOCKS * tc * STATE_COLS, LANES), F32),
            pltpu.VMEM((tc, SSM_W), F32),
            pltpu.VMEM(state_shape, F32),
        ],
        compiler_params=_params(("arbitrary", "arbitrary")),
        name="ssm_chains%d" % chains,
    )(u, h0, wb, cb, lam, d, gw, gbias, og)


def _out_proj_kernel(x_ref, oa_ref, os_ref, w_ref, o_ref):
    acc = jnp.dot(oa_ref[...].astype(BF16), w_ref[0], preferred_element_type=F32)
    acc = acc + jnp.dot(os_ref[...].astype(BF16), w_ref[1], preferred_element_type=F32)
    o_ref[...] = x_ref[...] + acc


def _out_proj(x, o_att, o_ssm, w, tm):
    m = x.shape[0]
    return pl.pallas_call(
        _out_proj_kernel,
        grid=(m // tm,),
        in_specs=[
            pl.BlockSpec((tm, D_MODEL), lambda i: (i, 0)),
            pl.BlockSpec((tm, ATT_W), lambda i: (i, 0)),
            pl.BlockSpec((tm, SSM_W), lambda i: (i, 0)),
            pl.BlockSpec((2, ATT_W, D_MODEL), lambda i: (0, 0, 0)),
        ],
        out_specs=pl.BlockSpec((tm, D_MODEL), lambda i: (i, 0)),
        out_shape=jax.ShapeDtypeStruct((m, D_MODEL), F32),
        compiler_params=_params(("arbitrary",)),
        name="out_proj",
    )(x, o_att, o_ssm, w)


def _ffn_kernel(h_ref, g_ref, prev_ref, wg_ref, wv_ref, cw_ref, cb_ref, wdn_ref, o_ref, cn_ref,
                fn_ref, acc_ref, halo_ref, *, tm, shift, tiles_per_seq, nf):
    i = pl.program_id(0)
    f = pl.program_id(1)

    @pl.when(f == 0)
    def _():
        fn_ref[...] = _rms(h_ref[...], g_ref[...], NORM_EPS).astype(BF16)
        acc_ref[...] = jnp.zeros(acc_ref.shape, F32)

    @pl.when((i % tiles_per_seq) == 0)
    def _():
        halo_ref[f] = prev_ref[0]

    fn = fn_ref[...]
    down = None
    for c0 in range(0, FFN_TF, MXU_WIDTH):
        cols = slice(c0, c0 + MXU_WIDTH)
        gate = jnp.dot(fn, wg_ref[:, cols], preferred_element_type=F32)
        val = jnp.dot(fn, wv_ref[:, cols], preferred_element_type=F32)
        gext = jnp.concatenate([halo_ref[f, :, cols], gate], axis=0)
        g1 = gext[CONV_HALO - shift:CONV_HALO - shift + tm]
        g2 = gext[CONV_HALO - 2 * shift:CONV_HALO - 2 * shift + tm]
        conv = cb_ref[:, cols] + (cw_ref[0:1, cols] * g2 + cw_ref[1:2, cols] * g1 + cw_ref[2:3, cols] * gate)
        hact = (conv * jax.nn.sigmoid(conv)) * val
        part = jnp.dot(hact.astype(BF16), wdn_ref[cols, :], preferred_element_type=F32)
        down = part if down is None else down + part
        halo_ref[f, :, cols] = gate[tm - CONV_HALO:tm]
        cn_ref[0, :, cols] = gate[tm - 2 * shift:tm]
    acc_ref[...] += down

    @pl.when(f == nf - 1)
    def _():
        o_ref[...] = h_ref[...] + acc_ref[...]


def _ffn(h, g, prev, wg, wv, cw, cb, wdn, tm, shift, tiles_per_seq):
    m = h.shape[0]
    nf = D_FF_PAD // FFN_TF
    kern = functools.partial(_ffn_kernel, tm=tm, shift=shift, tiles_per_seq=tiles_per_seq, nf=nf)
    return pl.pallas_call(
        kern,
        grid=(m // tm, nf),
        in_specs=[
            pl.BlockSpec((tm, D_MODEL), lambda i, f: (i, 0)),
            pl.BlockSpec((1, D_MODEL), lambda i, f: (0, 0)),
            pl.BlockSpec((1, CONV_HALO, FFN_TF), lambda i, f: (i // tiles_per_seq, 0, f)),
            pl.BlockSpec((D_MODEL, FFN_TF), lambda i, f: (0, f)),
            pl.BlockSpec((D_MODEL, FFN_TF), lambda i, f: (0, f)),
            pl.BlockSpec((CONV_W, FFN_TF), lambda i, f: (0, f)),
            pl.BlockSpec((1, FFN_TF), lambda i, f: (0, f)),
            pl.BlockSpec((FFN_TF, D_MODEL), lambda i, f: (f, 0)),
        ],
        out_specs=[
            pl.BlockSpec((tm, D_MODEL), lambda i, f: (i, 0)),
            pl.BlockSpec((1, 2 * shift, FFN_TF), lambda i, f: (i, 0, f)),
        ],
        out_shape=[
            jax.ShapeDtypeStruct((m, D_MODEL), F32),
            jax.ShapeDtypeStruct((m // tm, 2 * shift, D_FF_PAD), F32),
        ],
        scratch_shapes=[
            pltpu.VMEM((tm, D_MODEL), BF16),
            pltpu.VMEM((tm, D_MODEL), F32),
            pltpu.VMEM((nf, CONV_HALO, FFN_TF), F32),
        ],
        compiler_params=_params(("arbitrary", "arbitrary")),
        name="conv_ffn",
    )(h, g, prev, wg, wv, cw, cb, wdn)


def _ple_kernel(h_ref, gp_ref, pe_ref, wg_ref, wp_ref, gf_ref, o_ref):
    h = h_ref[...]
    hn = _rms(h, gp_ref[...], NORM_EPS).astype(BF16)
    gate = jax.nn.sigmoid(jnp.dot(hn, wg_ref[...], preferred_element_type=F32))
    proj = jnp.dot(pe_ref[...].astype(BF16), wp_ref[...], preferred_element_type=F32)
    o_ref[...] = _rms(h + gate * proj, gf_ref[...], NORM_EPS)


def _ple(h, gp, pe, wg, wp, gf, tm):
    m = h.shape[0]
    return pl.pallas_call(
        _ple_kernel,
        grid=(m // tm,),
        in_specs=[
            pl.BlockSpec((tm, D_MODEL), lambda i: (i, 0)),
            pl.BlockSpec((1, D_MODEL), lambda i: (0, 0)),
            pl.BlockSpec((tm, PLE_DIM), lambda i: (i, 0)),
            pl.BlockSpec((D_MODEL, D_MODEL), lambda i: (0, 0)),
            pl.BlockSpec((PLE_DIM, D_MODEL), lambda i: (0, 0)),
            pl.BlockSpec((1, D_MODEL), lambda i: (0, 0)),
        ],
        out_specs=pl.BlockSpec((tm, D_MODEL), lambda i: (i, 0)),
        out_shape=jax.ShapeDtypeStruct((m, D_MODEL), F32),
        compiler_params=_params(("arbitrary",)),
        name="ple_final",
    )(h, gp, pe, wg, wp, gf)


def _t5_bucket(dist):
    max_exact = N_BUCKETS // 2
    n = jnp.maximum(dist, 0)
    nf = jnp.maximum(n, 1).astype(F32)
    large = max_exact + (jnp.log(nf / max_exact) / math.log(MAX_DISTANCE / max_exact)
                         * (N_BUCKETS - max_exact)).astype(jnp.int32)
    large = jnp.minimum(large, N_BUCKETS - 1)
    return jnp.where(n < max_exact, n, large)


def _toeplitz(w, n):
    lead = w.shape[:-1]
    flat = jnp.tile(w, (1,) * len(lead) + (n,))[..., :n * (2 * n - 1)]
    return flat.reshape(lead + (n, 2 * n - 1))[..., :n]


def _prompt_bias_tiles(table, n):
    assert n >= MAX_DISTANCE
    far = table[N_BUCKETS - 1]
    bvec = (table[_t5_bucket(jnp.arange(2 * n, dtype=jnp.int32))] - far[None, :]).T
    bvec = bvec * LOG2E
    d = jnp.arange(2 * n, dtype=jnp.int32)
    d = jnp.where(d >= n, d - 2 * n, d)
    diag = jnp.where(d[None, :] <= 0, bvec[:, jnp.abs(d)], -jnp.inf)
    sub = bvec[:, jnp.clip(n - d, 0, 2 * n - 1)]
    return jnp.concatenate([_toeplitz(sub, n), _toeplitz(diag, n)], axis=-1)


def _sample_bias(table, n_new):
    assert PAGE_SIZE >= MAX_DISTANCE
    far = table[N_BUCKETS - 1]
    t = jnp.arange(n_new, dtype=jnp.int32)
    col = jnp.arange(PAGE_SIZE, dtype=jnp.int32)

    def rows(b):
        b = jnp.transpose(b * LOG2E, (2, 0, 1))
        return jnp.broadcast_to(b[:, None], (N_HEADS, 2, n_new, PAGE_SIZE)).reshape(-1, PAGE_SIZE)

    dist_last = PAGE_SIZE + t[:, None] - col[None, :]
    dist_new = t[:, None] - col[None, :]
    b_new = jnp.where((dist_new >= 0)[..., None], table[_t5_bucket(dist_new)] - far, -jnp.inf)
    return rows(table[_t5_bucket(dist_last)] - far), rows(b_new)


def _block_diag(w):
    nb, gp, a, b = w.shape
    eye = jnp.eye(gp, dtype=w.dtype)
    out = w[:, :, :, None, :] * eye[None, :, None, :, None]
    return out.reshape(nb, gp * a, gp * b)


def _ssm_tables(a_re, a_im, b_re, b_im, c_re, c_im, log_dt):
    g, n, gc = N_SSM_GROUPS, SSM_STATE, SSM_GROUP
    dt = jnp.exp(log_dt)[:, None]
    decay = jnp.exp(a_re * dt)
    ab_re = decay * jnp.cos(a_im * dt)
    ab_im = decay * jnp.sin(a_im * dt)
    den = a_re * a_re + a_im * a_im
    num_re = ab_re - 1.0
    coef_re = (num_re * a_re + ab_im * a_im) / den
    coef_im = (ab_im * a_re - num_re * a_im) / den
    bb_re = coef_re[..., None] * b_re - coef_im[..., None] * b_im
    bb_im = coef_re[..., None] * b_im + coef_im[..., None] * b_re
    blk = lambda w: w.reshape((N_GROUP_BLOCKS, GROUPS_PER_BLOCK) + w.shape[1:])
    to_cn = lambda w: jnp.swapaxes(blk(w), 2, 3)
    wb = jnp.concatenate([_block_diag(to_cn(bb_re)), _block_diag(to_cn(bb_im))], axis=2)
    to_nc = lambda w: jnp.swapaxes(blk(w), 2, 3)
    cb = jnp.concatenate([_block_diag(to_nc(c_re)), _block_diag(to_nc(-c_im))], axis=1)
    lr = ab_re.reshape(N_GROUP_BLOCKS, STATE_COLS // 2, LANES)
    li = ab_im.reshape(N_GROUP_BLOCKS, STATE_COLS // 2, LANES)
    lam = jnp.stack([jnp.concatenate([lr, lr], axis=1), jnp.concatenate([-li, li], axis=1)], axis=1)
    return wb.astype(BF16), cb.astype(BF16), lam.astype(F32)


def _cast_in_kernel(w_ref, wb_ref, wkt_ref):
    w = w_ref[...]
    wb_ref[...] = w.astype(BF16)
    wkt_ref[...] = w[:, ATT_W:2 * ATT_W].T.astype(BF16)


def _cast_w_in(w, tr=256):
    return pl.pallas_call(
        _cast_in_kernel,
        grid=(D_MODEL // tr,),
        in_specs=[pl.BlockSpec((tr, 4 * ATT_W), lambda i: (i, 0))],
        out_specs=[pl.BlockSpec((tr, 4 * ATT_W), lambda i: (i, 0)),
                   pl.BlockSpec((ATT_W, tr), lambda i: (0, i))],
        out_shape=[jax.ShapeDtypeStruct((D_MODEL, 4 * ATT_W), BF16),
                   jax.ShapeDtypeStruct((ATT_W, D_MODEL), BF16)],
        compiler_params=_params(("arbitrary",)),
        name="cast_w_in",
    )(w)


def _cast_up_kernel(w_ref, wg_ref, wv_ref):
    zeros = jnp.zeros((wg_ref.shape[0], D_FF_PAD - D_FF), BF16)
    for n, o_ref in enumerate((wg_ref, wv_ref)):
        o_ref[:, 0:D_FF] = w_ref[:, n * D_FF:(n + 1) * D_FF].astype(BF16)
        o_ref[:, D_FF:D_FF_PAD] = zeros


def _cast_w_up(w, tr=64):
    out = jax.ShapeDtypeStruct((D_MODEL, D_FF_PAD), BF16)
    spec = pl.BlockSpec((tr, D_FF_PAD), lambda i: (i, 0))
    return pl.pallas_call(
        _cast_up_kernel,
        grid=(D_MODEL // tr,),
        in_specs=[pl.BlockSpec((tr, 2 * D_FF), lambda i: (i, 0))],
        out_specs=[spec, spec],
        out_shape=[out, out],
        compiler_params=_params(("arbitrary",)),
        name="cast_w_up",
    )(w)


def _cast_down_kernel(w_ref, o_ref, *, n_valid):
    keep = pl.program_id(0) < n_valid
    o_ref[...] = jnp.where(keep, w_ref[...], 0.0).astype(BF16)


def _cast_w_down(w, tr=LANES):
    assert D_FF % tr == 0 and D_FF_PAD % tr == 0
    n_valid = D_FF // tr
    return pl.pallas_call(
        functools.partial(_cast_down_kernel, n_valid=n_valid),
        grid=(D_FF_PAD // tr,),
        in_specs=[pl.BlockSpec((tr, D_MODEL), lambda i: (jnp.minimum(i, n_valid - 1), 0))],
        out_specs=pl.BlockSpec((tr, D_MODEL), lambda i: (i, 0)),
        out_shape=jax.ShapeDtypeStruct((D_FF_PAD, D_MODEL), BF16),
        compiler_params=_params(("arbitrary",)),
        name="cast_w_down",
    )(w)


def _pad_ff(w, axis):
    pad = [(0, 0)] * w.ndim
    pad[axis] = (0, D_FF_PAD - D_FF)
    return jnp.pad(w, pad)


def _state_to_tiles(re, im):
    shape = (re.shape[0], N_GROUP_BLOCKS, STATE_COLS // 2, LANES)
    return jnp.concatenate([re.reshape(shape), im.reshape(shape)], axis=2)


def _tiles_to_state(tiles):
    half = STATE_COLS // 2
    shape = (1, tiles.shape[0], N_SSM_GROUPS, SSM_STATE)
    return tiles[:, :, :half].reshape(shape), tiles[:, :, half:].reshape(shape)


def kernel(x_prompt, x_sample, cache_k, cache_v, state_ssm_re, state_ssm_im, state_conv, page_table, p_prompt, p_sample, rel_bias, norm_mix_g, w_in, lambda_q1, lambda_k1, lambda_q2, lambda_k2, attn_subln_g, ssm_a_re, ssm_a_im, ssm_b_re, ssm_b_im, ssm_c_re, ssm_c_im, ssm_d, ssm_log_dt, ssm_glu_w, ssm_glu_b, ssm_out_g, w_out, norm_ffn_g, ffn_w_up, ffn_conv_w, ffn_conv_b, ffn_w_down, norm_ple_g, ple_w_gate, ple_w_proj, norm_final_g):
    assert DEPTH == 1
    li = 0
    lam_init = 0.8 - 0.6 * math.exp(-0.3 * li)
    nb, t, _ = x_prompt.shape
    db, ds, _ = x_sample.shape
    assert db == SUBLANES
    m_p = nb * t
    m_s = db * ds

    row = lambda v: v.reshape(1, -1).astype(F32)
    w_in_b, w_kt_b = _cast_w_in(w_in[li])
    w_out_b = w_out[li].astype(BF16).reshape(2, ATT_W, D_MODEL)
    w_ug_b, w_uv_b = _cast_w_up(ffn_w_up[li])
    w_dn_b = _cast_w_down(ffn_w_down[li])
    conv_w = _pad_ff(ffn_conv_w[li], 1)
    conv_b = _pad_ff(ffn_conv_b[li].reshape(1, D_FF), 1)
    w_gate_b = ple_w_gate[li].astype(BF16)
    w_proj_b = ple_w_proj[li].astype(BF16)
    lam_rows = jnp.stack([lambda_q1[li], lambda_k1[li], lambda_q2[li], lambda_k2[li]]).astype(F32)
    subln_g = attn_subln_g[li].astype(F32)
    wb, cb, tbl = _ssm_tables(ssm_a_re[li], ssm_a_im[li], ssm_b_re[li], ssm_b_im[li],
                              ssm_c_re[li], ssm_c_im[li], ssm_log_dt[li])
    glu_w = _block_diag(ssm_glu_w[li].reshape(N_GROUP_BLOCKS, GROUPS_PER_BLOCK, SSM_GROUP, SSM_GROUP)).astype(BF16)
    ssm_args = (wb, cb, tbl, row(ssm_d[li]), glu_w, row(ssm_glu_b[li]), row(ssm_out_g[li]))

    tq = 512
    xp = x_prompt.reshape(m_p, D_MODEL)
    q_p, kt_p, v_p, u_p = _in_proj(xp, row(norm_mix_g[li]), w_in_b, w_kt_b, tm=512,
                                   rows_per_seq=t, k_transposed=True)
    shp = lambda a: a.reshape(nb, t, -1)
    band = _prompt_bias_tiles(rel_bias.astype(F32), MAX_DISTANCE)
    o_att_p = _attn_prompt(shp(q_p), kt_p, shp(v_p), band, lam_rows,
                           subln_g.reshape(N_HEADS, 1, V_HEAD), tq, lam_init)
    zero_state = jnp.zeros((nb, N_GROUP_BLOCKS, 1, STATE_COLS, LANES), F32)
    o_ssm_p, hf_p = _ssm(shp(u_p), zero_state, *ssm_args, tc=256, chains=1, out_dtype=BF16)
    h1_p = _out_proj(xp, o_att_p.reshape(m_p, ATT_W), o_ssm_p.reshape(m_p, SSM_W), w_out_b, tm=512)
    tm_ffn = 512
    zero_prev = jnp.zeros((nb, CONV_HALO, D_FF_PAD), F32)
    h2_p, conv_p = _ffn(h1_p, row(norm_ffn_g[li]), zero_prev, w_ug_b, w_uv_b, conv_w, conv_b, w_dn_b,
                        tm=tm_ffn, shift=1, tiles_per_seq=t // tm_ffn)
    y_p = _ple(h2_p, row(norm_ple_g[li]), p_prompt[li].reshape(m_p, PLE_DIM), w_gate_b, w_proj_b,
               row(norm_final_g), tm=512)

    tmaj = lambda a: jnp.swapaxes(a, 0, 1).reshape(m_s, -1)
    bmaj = lambda a: jnp.swapaxes(a.reshape(ds, db, -1), 0, 1)
    xs = tmaj(x_sample)
    q_s, k_s, v_s, u_s = _in_proj(xs, row(norm_mix_g[li]), w_in_b, w_kt_b, tm=m_s,
                                  rows_per_seq=m_s, k_transposed=False)
    q_b = bmaj(q_s)
    q_rep = jnp.tile(q_b, (1, 2 * N_HEADS, 1)).reshape(db, 2 * N_HEADS, ds, ATT_W)
    sel = (jnp.arange(ATT_W)[None, :] // QK_DIM) == jnp.arange(2 * N_HEADS)[:, None]
    qbd = jnp.where(sel[None, :, None, :], q_rep, jnp.zeros_like(q_rep)).reshape(db, 2 * N_HEADS * ds, ATT_W)
    pad_rows = lambda a: jnp.pad(bmaj(a), ((0, 0), (0, PAGE_SIZE - ds), (0, 0)))
    bias_past, bias_new = _sample_bias(rel_bias.astype(F32), ds)
    n_pool = cache_k.shape[1]
    k_pool_t = jnp.transpose(cache_k[li], (0, 2, 3, 4, 1)).reshape(n_pool, ATT_W, PAGE_SIZE)
    k_new_t = jnp.swapaxes(pad_rows(k_s), 1, 2)
    v_pool_r = cache_v[li].reshape(n_pool, PAGE_SIZE * N_HEADS, V_HEAD)
    v_new_r = pad_rows(v_s).reshape(db, PAGE_SIZE * N_HEADS, V_HEAD)
    o_att_s = _attn_sample(page_table, qbd, k_pool_t, v_pool_r,
                           bias_past, k_new_t, v_new_r, bias_new,
                           lam_rows, subln_g, n_pages_step=16, lam_init=lam_init)
    h0_s = jnp.swapaxes(_state_to_tiles(state_ssm_re[li], state_ssm_im[li]), 0, 1)[None]
    o_ssm_s, hf_s = _ssm(u_s[None], h0_s, *ssm_args, tc=m_s, chains=db, out_dtype=F32)
    h1_s = _out_proj(xs, tmaj(o_att_s), o_ssm_s[0], w_out_b, tm=m_s)
    prev_s = _pad_ff(jnp.swapaxes(state_conv[li], 0, 1).reshape(1, 2 * db, D_FF), 2)
    h2_s, conv_s = _ffn(h1_s, row(norm_ffn_g[li]), prev_s, w_ug_b, w_uv_b, conv_w, conv_b, w_dn_b,
                        tm=m_s, shift=db, tiles_per_seq=1)
    y_s = _ple(h2_s, row(norm_ple_g[li]), tmaj(p_sample[li]), w_gate_b, w_proj_b,
               row(norm_final_g), tm=m_s)

    re_p, im_p = _tiles_to_state(hf_p[:, :, 0])
    re_s, im_s = _tiles_to_state(jnp.swapaxes(hf_s[0], 0, 1))
    tiles = t // tm_ffn
    conv_p = conv_p[tiles - 1::tiles][None, :, :, :D_FF]
    conv_s = jnp.swapaxes(conv_s.reshape(2, db, D_FF_PAD), 0, 1)[None, :, :, :D_FF]
    return (y_p.reshape(nb, t, D_MODEL),
            bmaj(y_s),
            jnp.transpose(kt_p.reshape(1, nb, N_HEADS, 2, QK_DIM, t), (0, 1, 5, 2, 3, 4)),
            v_p.reshape(1, nb, t, N_HEADS, V_HEAD),
            re_p, im_p, conv_p,
            bmaj(k_s).reshape(1, db, ds, N_HEADS, 2, QK_DIM),
            bmaj(v_s).reshape(1, db, ds, N_HEADS, V_HEAD),
            re_s, im_s, conv_s)
```

```python
import functools
import math

import jax
import jax.numpy as jnp
from jax import lax
from jax.experimental import pallas as pl
from jax.experimental.pallas import tpu as pltpu

F32 = jnp.float32
BF16 = jnp.bfloat16

D_MODEL = 2048
DEPTH = 1
PAGE_SIZE = 128
ATT_W = 1024
SSM_W = 1024
V_HEAD = 128
N_HEADS = 8
QK_DIM = 64
SSM_GROUP = 16
N_SSM_GROUPS = 64
SSM_STATE = 64
D_FF = 5504
CONV_W = 3
PLE_DIM = 256
N_BUCKETS = 32
MAX_DISTANCE = 128
NORM_EPS = 1e-6
SUBLN_EPS = 1e-5
LOG2E = math.log2(math.e)
QK_SCALE = QK_DIM ** -0.5 * LOG2E

LANES = 128
SUBLANES = 8
MXU_WIDTH = 256
VMEM_LIMIT = 56 * 1024 * 1024

GROUPS_PER_BLOCK = LANES // SSM_GROUP
N_GROUP_BLOCKS = N_SSM_GROUPS // GROUPS_PER_BLOCK
STATE_LANES = GROUPS_PER_BLOCK * SSM_STATE

D_FF_PAD = 5632
FFN_TF = 512
CONV_HALO = 16


def _params(sem):
    return pltpu.CompilerParams(dimension_semantics=sem, vmem_limit_bytes=VMEM_LIMIT)


def _rms(x, g, eps):
    return x * lax.rsqrt(jnp.mean(x * x, axis=-1, keepdims=True) + eps) * g


def _in_proj_kernel(x_ref, g_ref, w_ref, wkt_ref, q_ref, k_ref, v_ref, u_ref, xn_ref, *, k_transposed):
    j = pl.program_id(1)

    @pl.when(j == 0)
    def _():
        xn_ref[...] = _rms(x_ref[...], g_ref[...], NORM_EPS).astype(BF16)

    def proj():
        return jnp.dot(xn_ref[...], w_ref[...], preferred_element_type=F32)

    @pl.when(j == 0)
    def _():
        q_ref[...] = (proj() * QK_SCALE).astype(BF16)

    @pl.when(j == 1)
    def _():
        if k_transposed:
            k_ref[0] = lax.dot_general(wkt_ref[...], xn_ref[...], (((1,), (1,)), ((), ())),
                                       preferred_element_type=F32)
        else:
            k_ref[...] = proj()

    @pl.when(j == 2)
    def _():
        v_ref[...] = proj()

    @pl.when(j == 3)
    def _():
        u_ref[...] = proj()


def _in_proj(x, g, w, wkt, tm, rows_per_seq, k_transposed):
    m = x.shape[0]
    out_spec = pl.BlockSpec((tm, ATT_W), lambda i, j: (i, 0))
    if k_transposed:
        tiles = rows_per_seq // tm
        k_spec = pl.BlockSpec((1, ATT_W, tm), lambda i, j: (i // tiles, 0, i % tiles))
        k_shape = jax.ShapeDtypeStruct((m // rows_per_seq, ATT_W, rows_per_seq), F32)
        w_spec = pl.BlockSpec((D_MODEL, ATT_W), lambda i, j: (0, jnp.where(j == 1, 0, j)))
    else:
        k_spec = out_spec
        k_shape = jax.ShapeDtypeStruct((m, ATT_W), F32)
        w_spec = pl.BlockSpec((D_MODEL, ATT_W), lambda i, j: (0, j))
    return pl.pallas_call(
        functools.partial(_in_proj_kernel, k_transposed=k_transposed),
        grid=(m // tm, 4),
        in_specs=[
            pl.BlockSpec((tm, D_MODEL), lambda i, j: (i, 0)),
            pl.BlockSpec((1, D_MODEL), lambda i, j: (0, 0)),
            w_spec,
            pl.BlockSpec((ATT_W, D_MODEL), lambda i, j: (0, 0)),
        ],
        out_specs=[out_spec, k_spec, out_spec, out_spec],
        out_shape=[
            jax.ShapeDtypeStruct((m, ATT_W), BF16),
            k_shape,
            jax.ShapeDtypeStruct((m, ATT_W), F32),
            jax.ShapeDtypeStruct((m, SSM_W), F32),
        ],
        scratch_shapes=[pltpu.VMEM((tm, D_MODEL), BF16)],
        compiler_params=_params(("arbitrary", "arbitrary")),
        name="in_proj",
    )(x, g, w, wkt)


def _lambda(lam_ref, lam_init):
    s1 = jnp.sum(lam_ref[0:1, :] * lam_ref[1:2, :], axis=-1, keepdims=True)
    s2 = jnp.sum(lam_ref[2:3, :] * lam_ref[3:4, :], axis=-1, keepdims=True)
    return jnp.exp(s1) - jnp.exp(s2) + lam_init


def _online_softmax_step(s, v_blk, m_ref, l_ref, acc_ref):
    m_prev = m_ref[...]
    m_new = jnp.maximum(m_prev, jnp.max(s, axis=-1, keepdims=True))
    alpha = jnp.exp2(m_prev - m_new)
    p = jnp.exp2(s - jnp.concatenate([m_new] * (s.shape[1] // LANES), axis=1))
    l_ref[...] = alpha * l_ref[...] + jnp.sum(p, axis=-1, keepdims=True)
    acc_ref[...] = alpha * acc_ref[...] + jnp.dot(
        p.astype(BF16), v_blk, preferred_element_type=F32)
    m_ref[...] = m_new


def _attn_prompt_kernel(lam_ref, q_ref, k_ref, v_ref, band_ref, g_ref, o_ref,
                        kb_ref, vb_ref, qs_ref, bias_ref, m_ref, l_ref, acc_ref, *, tq, nh, lam_init):
    i = pl.program_id(2)
    head_cols = [slice(h * V_HEAD, (h + 1) * V_HEAD) for h in range(nh)]

    @pl.when(i == 0)
    def _():
        kb_ref[...] = k_ref[0].astype(BF16)
        vb_ref[...] = v_ref[0].astype(BF16)
        w = band_ref.shape[1]
        zero = jnp.zeros((w, w), F32)
        masked = jnp.full((w, w), -jnp.inf, F32)
        n = tq // w
        for h in range(nh):
            left, right = band_ref[h, :, 0:w], band_ref[h, :, w:2 * w]
            for r in range(n):
                for c in range(n):
                    rows = slice(r * w, (r + 1) * w)
                    bias_ref[h, rows, c * w:(c + 1) * w] = left if (r == 0 and c == n - 1) else zero
                    bias_ref[h, rows, tq + c * w:tq + (c + 1) * w] = (
                        right if c == r else left if c == r - 1 else masked if c > r else zero)

    for h in range(nh):
        q = q_ref[0, :, head_cols[h]]
        lane = lax.broadcasted_iota(jnp.int32, q.shape, 1)
        zero = jnp.zeros_like(q)
        qs_ref[h, 0:tq, :] = jnp.where(lane < QK_DIM, q, zero)
        qs_ref[h, tq:2 * tq, :] = jnp.where(lane >= QK_DIM, q, zero)
    m_ref[...] = jnp.full(m_ref.shape, -jnp.inf, F32)
    l_ref[...] = jnp.zeros(l_ref.shape, F32)
    acc_ref[...] = jnp.zeros(acc_ref.shape, F32)

    def update(j, width, biased):
        off = pl.multiple_of(j * tq, tq)
        for h in range(nh):
            rows = slice(h * 2 * QK_DIM, (h + 1) * 2 * QK_DIM)
            s = jnp.dot(qs_ref[h], kb_ref[rows, pl.ds(off, width)], preferred_element_type=F32)
            if biased:
                bias = bias_ref[h, :, 2 * tq - width:2 * tq]
                s = (s.reshape(2, tq, width) + bias[None]).reshape(2 * tq, width)
            _online_softmax_step(s, vb_ref[pl.ds(off, width), head_cols[h]],
                                 m_ref.at[h], l_ref.at[h], acc_ref.at[h])

    def far_body(j, carry):
        update(j, tq, False)
        return carry

    lax.fori_loop(0, jnp.maximum(i - 1, 0), far_body, 0)

    @pl.when(i >= 1)
    def _():
        update(i - 1, 2 * tq, True)

    @pl.when(i == 0)
    def _():
        update(0, tq, True)

    lam = _lambda(lam_ref, lam_init)
    for h in range(nh):
        inv_l = 1.0 / l_ref[h]
        o = acc_ref[h, 0:tq, :] * inv_l[0:tq] - lam * (acc_ref[h, tq:2 * tq, :] * inv_l[tq:2 * tq])
        o_ref[0, :, head_cols[h]] = (_rms(o, g_ref[h], SUBLN_EPS) * (1.0 - lam_init)).astype(o_ref.dtype)


def _attn_prompt(q, k, v, band, lam_rows, subln_g, tq, nh, lam_init):
    b, t, _ = q.shape
    w = band.shape[1]
    assert tq % w == 0 and band.shape[2] == 2 * w and N_HEADS % nh == 0
    kern = functools.partial(_attn_prompt_kernel, tq=tq, nh=nh, lam_init=lam_init)
    gw = nh * V_HEAD
    return pl.pallas_call(
        kern,
        grid=(b, N_HEADS // nh, t // tq),
        in_specs=[
            pl.BlockSpec((4, QK_DIM), lambda bi, h, i: (0, 0)),
            pl.BlockSpec((1, tq, gw), lambda bi, h, i: (bi, i, h)),
            pl.BlockSpec((1, gw, t), lambda bi, h, i: (bi, h, 0)),
            pl.BlockSpec((1, t, gw), lambda bi, h, i: (bi, 0, h)),
            pl.BlockSpec((nh, w, 2 * w), lambda bi, h, i: (h, 0, 0)),
            pl.BlockSpec((nh, 1, V_HEAD), lambda bi, h, i: (h, 0, 0)),
        ],
        out_specs=pl.BlockSpec((1, tq, gw), lambda bi, h, i: (bi, i, h)),
        out_shape=jax.ShapeDtypeStruct((b, t, ATT_W), BF16),
        scratch_shapes=[
            pltpu.VMEM((gw, t), BF16),
            pltpu.VMEM((t, gw), BF16),
            pltpu.VMEM((nh, 2 * tq, V_HEAD), BF16),
            pltpu.VMEM((nh, tq, 2 * tq), F32),
            pltpu.VMEM((nh, 2 * tq, LANES), F32),
            pltpu.VMEM((nh, 2 * tq, LANES), F32),
            pltpu.VMEM((nh, 2 * tq, V_HEAD), F32),
        ],
        compiler_params=_params(("arbitrary", "arbitrary", "arbitrary")),
        name="attn_prompt",
    )(lam_rows, q, k, v, band, subln_g)


def _attn_sample_parts(step, n_steps, lam_ref, qbd_ref, k_refs, v_refs, bias_ref, kn_ref, vn_ref, bn_ref,
                       g_ref, o_ref, m_ref, l_ref, acc_ref, *, n_new, lam_init):
    def init():
        @pl.when(step == 0)
        def _():
            m_ref[...] = jnp.full(m_ref.shape, -jnp.inf, F32)
            l_ref[...] = jnp.zeros(l_ref.shape, F32)
            acc_ref[...] = jnp.zeros(acc_ref.shape, F32)

    qbd = qbd_ref[0]
    rows = 2 * n_new

    def update(k_pages, v_page_refs, bias_last):
        parts = [jnp.dot(qbd, kp.astype(BF16), preferred_element_type=F32) for kp in k_pages]
        parts[-1] = parts[-1] + bias_last
        s = jnp.concatenate(parts, axis=1)
        m_prev = m_ref[...]
        m_new = jnp.maximum(m_prev, jnp.max(s, axis=-1, keepdims=True))
        alpha = jnp.exp2(m_prev - m_new)
        p = jnp.exp2(s - jnp.concatenate([m_new] * len(parts), axis=1))
        l_ref[...] = alpha * l_ref[...] + jnp.sum(p, axis=-1, keepdims=True)
        m_ref[...] = m_new
        pb = p.astype(BF16)
        for h in range(N_HEADS):
            v_h = jnp.concatenate(
                [vr[0, pl.ds(h, PAGE_SIZE, stride=N_HEADS), :].astype(BF16) for vr in v_page_refs], axis=0)
            r = slice(h * rows, (h + 1) * rows)
            acc_ref[r, :] = alpha[r] * acc_ref[r, :] + jnp.dot(pb[r], v_h, preferred_element_type=F32)

    is_last = step == n_steps - 1

    def main():
        bias_last = jnp.where(is_last, bias_ref[...], jnp.zeros(bias_ref.shape, F32))
        update([kr[0] for kr in k_refs], v_refs, bias_last)

    def finish():
        @pl.when(is_last)
        def _():
            update([kn_ref[0]], [vn_ref], bn_ref[...])
            lam = _lambda(lam_ref, lam_init)
            acc = acc_ref[...] * (1.0 / l_ref[...])
            for h in range(N_HEADS):
                blk = acc[h * rows:(h + 1) * rows]
                o = blk[0:n_new] - lam * blk[n_new:rows]
                o = _rms(o, g_ref[h:h + 1, :], SUBLN_EPS) * (1.0 - lam_init)
                o_ref[0, :, h * V_HEAD:(h + 1) * V_HEAD] = o

    return init, main, finish


STATE_COLS = 2 * STATE_LANES // LANES


def _state_rows(g, tc, col):
    return pl.ds(g * tc * STATE_COLS + col, tc, stride=STATE_COLS)


def _ssm_kernel(u_ref, h0_ref, wb_ref, cb_ref, lam_ref, d_ref, gw_ref, gbias_ref, og_ref,
                o_ref, hf_ref, hb_ref, gbuf_ref, carry_ref, *, tc, chains):
    c = pl.program_id(1)

    @pl.when(c == 0)
    def _():
        carry_ref[...] = h0_ref[0]

    for g in range(N_GROUP_BLOCKS):
        u = u_ref[0, :, g * LANES:(g + 1) * LANES]
        bu = jnp.dot(u.astype(BF16), wb_ref[g], preferred_element_type=F32)
        for k in range(STATE_COLS):
            hb_ref[_state_rows(g, tc, k), :] = bu[:, k * LANES:(k + 1) * LANES]

    coef = [(lam_ref[g, 0], lam_ref[g, 1]) for g in range(N_GROUP_BLOCKS)]

    def step(g, row, h):
        start = (g * tc + row) * STATE_COLS
        if not isinstance(start, int):
            start = pl.multiple_of(start, STATE_COLS)
        rows = pl.ds(start, STATE_COLS)
        a, b = coef[g]
        h = a * h + b * pltpu.roll(h, STATE_COLS // 2, 0) + hb_ref[rows, :]
        hb_ref[rows, :] = h
        return h

    if chains == 1:
        def body(t, hs):
            return tuple(step(g, t, hs[g]) for g in range(N_GROUP_BLOCKS))

        hs = lax.fori_loop(0, tc, body, tuple(carry_ref[g, 0] for g in range(N_GROUP_BLOCKS)),
                           unroll=SUBLANES)
        for g in range(N_GROUP_BLOCKS):
            carry_ref[g, 0] = hs[g]
    else:
        for g in range(N_GROUP_BLOCKS):
            hs = [carry_ref[g, b] for b in range(chains)]
            for row in range(tc):
                hs[row % chains] = step(g, row, hs[row % chains])
            for b in range(chains):
                carry_ref[g, b] = hs[b]

    ssq = jnp.zeros((tc, 1), F32)
    for g in range(N_GROUP_BLOCKS):
        cols = slice(g * LANES, (g + 1) * LANES)
        h = jnp.concatenate([hb_ref[_state_rows(g, tc, k), :] for k in range(STATE_COLS)], axis=1)
        y = jnp.dot(h.astype(BF16), cb_ref[g], preferred_element_type=F32)
        y = y + d_ref[:, cols] * u_ref[0, :, cols]
        a = jax.nn.gelu(y)
        gate = jnp.dot(a.astype(BF16), gw_ref[g], preferred_element_type=F32) + gbias_ref[:, cols]
        a = a * jax.nn.sigmoid(gate)
        gbuf_ref[:, cols] = a
        ssq = ssq + jnp.sum(a * a, axis=-1, keepdims=True)
    inv = lax.rsqrt(ssq * (1.0 / SSM_W) + NORM_EPS)
    o_ref[0] = (gbuf_ref[...] * inv * og_ref[...]).astype(o_ref.dtype)
    hf_ref[0] = carry_ref[...]


def _ssm(u, h0, wb, cb, lam, d, gw, gbias, og, tc, chains, out_dtype):
    ns, t, _ = u.shape
    kern = functools.partial(_ssm_kernel, tc=tc, chains=chains)
    full = lambda *shape: pl.BlockSpec(shape, lambda s, c: (0,) * len(shape))
    state_shape = (N_GROUP_BLOCKS, chains, STATE_COLS, LANES)
    state_spec = pl.BlockSpec((1,) + state_shape, lambda s, c: (s, 0, 0, 0, 0))
    return pl.pallas_call(
        kern,
        grid=(ns, t // tc),
        in_specs=[
            pl.BlockSpec((1, tc, SSM_W), lambda s, c: (s, c, 0)),
            state_spec,
            full(N_GROUP_BLOCKS, LANES, 2 * STATE_LANES),
            full(N_GROUP_BLOCKS, 2 * STATE_LANES, LANES),
            full(N_GROUP_BLOCKS, 2, STATE_COLS, LANES),
            full(1, SSM_W),
            full(N_GROUP_BLOCKS, LANES, LANES),
            full(1, SSM_W),
            full(1, SSM_W),
        ],
        out_specs=[pl.BlockSpec((1, tc, SSM_W), lambda s, c: (s, c, 0)), state_spec],
        out_shape=[
            jax.ShapeDtypeStruct((ns, t, SSM_W), out_dtype),
            jax.ShapeDtypeStruct((ns,) + state_shape, F32),
        ],
        scratch_shapes=[
            pltpu.VMEM((N_GROUP_BLOCKS * tc * STATE_COLS, LANES), F32),
            pltpu.VMEM((tc, SSM_W), F32),
            pltpu.VMEM(state_shape, F32),
        ],
        compiler_params=_params(("arbitrary", "arbitrary")),
        name="ssm_chains%d" % chains,
    )(u, h0, wb, cb, lam, d, gw, gbias, og)


def _out_proj_kernel(x_ref, oa_ref, os_ref, w_ref, o_ref):
    acc = jnp.dot(oa_ref[...].astype(BF16), w_ref[0], preferred_element_type=F32)
    acc = acc + jnp.dot(os_ref[...].astype(BF16), w_ref[1], preferred_element_type=F32)
    o_ref[...] = x_ref[...] + acc


def _out_proj(x, o_att, o_ssm, w, tm):
    m = x.shape[0]
    return pl.pallas_call(
        _out_proj_kernel,
        grid=(m // tm,),
        in_specs=[
            pl.BlockSpec((tm, D_MODEL), lambda i: (i, 0)),
            pl.BlockSpec((tm, ATT_W), lambda i: (i, 0)),
            pl.BlockSpec((tm, SSM_W), lambda i: (i, 0)),
            pl.BlockSpec((2, ATT_W, D_MODEL), lambda i: (0, 0, 0)),
        ],
        out_specs=pl.BlockSpec((tm, D_MODEL), lambda i: (i, 0)),
        out_shape=jax.ShapeDtypeStruct((m, D_MODEL), F32),
        compiler_params=_params(("arbitrary",)),
        name="out_proj",
    )(x, o_att, o_ssm, w)


N_FFN_IN = 8


def _ffn_parts(i, f, h_ref, g_ref, prev_ref, wg_ref, wv_ref, cw_ref, cb_ref, wdn_ref, o_ref, cn_ref,
               fn_ref, acc_ref, halo_ref, *, tm, shift, tiles_per_seq, nf):
    def prologue():
        @pl.when(f == 0)
        def _():
            fn_ref[...] = _rms(h_ref[...], g_ref[...], NORM_EPS).astype(BF16)
            acc_ref[...] = jnp.zeros(acc_ref.shape, F32)

        @pl.when((i % tiles_per_seq) == 0)
        def _():
            halo_ref[f] = prev_ref[0]

    def main():
        fn = fn_ref[...]
        down = None
        for c0 in range(0, FFN_TF, MXU_WIDTH):
            cols = slice(c0, c0 + MXU_WIDTH)
            gate = jnp.dot(fn, wg_ref[:, cols], preferred_element_type=F32)
            val = jnp.dot(fn, wv_ref[:, cols], preferred_element_type=F32)
            gext = jnp.concatenate([halo_ref[f, :, cols], gate], axis=0)
            g1 = gext[CONV_HALO - shift:CONV_HALO - shift + tm]
            g2 = gext[CONV_HALO - 2 * shift:CONV_HALO - 2 * shift + tm]
            conv = cb_ref[:, cols] + (cw_ref[0:1, cols] * g2 + cw_ref[1:2, cols] * g1 + cw_ref[2:3, cols] * gate)
            hact = (conv * jax.nn.sigmoid(conv)) * val
            part = jnp.dot(hact.astype(BF16), wdn_ref[cols, :], preferred_element_type=F32)
            down = part if down is None else down + part
            halo_ref[f, :, cols] = gate[tm - CONV_HALO:tm]
            cn_ref[0, :, cols] = gate[tm - 2 * shift:tm]
        acc_ref[...] += down

    def epilogue():
        @pl.when(f == nf - 1)
        def _():
            o_ref[...] = h_ref[...] + acc_ref[...]

    return prologue, main, epilogue


def _ffn_kernel(*refs, **static):
    prologue, main, epilogue = _ffn_parts(pl.program_id(0), pl.program_id(1), *refs, **static)
    prologue()
    main()
    epilogue()


def _ffn_attn_kernel(pt_ref, *refs, n_pages_step, steps_per_seq, n_seq, n_new, lam_init, nf, **static):
    del pt_ref
    g = n_pages_step
    ffn_in = refs[:N_FFN_IN]
    lam_ref, qbd_ref = refs[N_FFN_IN:N_FFN_IN + 2]
    k_refs = refs[N_FFN_IN + 2:N_FFN_IN + 2 + g]
    v_refs = refs[N_FFN_IN + 2 + g:N_FFN_IN + 2 + 2 * g]
    rest = refs[N_FFN_IN + 2 + 2 * g:]
    bias_ref, kn_ref, vn_ref, bn_ref, sg_ref, o_ref, cn_ref, oa_ref = rest[:8]
    fn_ref, acc_ref, halo_ref, am_ref, al_ref, aacc_ref = rest[8:]
    i = pl.program_id(0)
    f = pl.program_id(1)
    s = i * nf + f
    active = s < n_seq * steps_per_seq
    prologue, ffn_main, epilogue = _ffn_parts(i, f, *ffn_in, o_ref, cn_ref, fn_ref, acc_ref, halo_ref,
                                              nf=nf, **static)
    init, attn_main, finish = _attn_sample_parts(
        s % steps_per_seq, steps_per_seq, lam_ref, qbd_ref, k_refs, v_refs, bias_ref, kn_ref, vn_ref,
        bn_ref, sg_ref, oa_ref, am_ref, al_ref, aacc_ref, n_new=n_new, lam_init=lam_init)
    prologue()

    @pl.when(active)
    def _():
        init()

    @pl.when(active)
    def _():
        ffn_main()
        attn_main()

    @pl.when(jnp.logical_not(active))
    def _():
        ffn_main()

    epilogue()

    @pl.when(active)
    def _():
        finish()


def _ffn(h, g, prev, wg, wv, cw, cb, wdn, tm, shift, tiles_per_seq, attn=None):
    m = h.shape[0]
    nf = D_FF_PAD // FFN_TF
    static = dict(tm=tm, shift=shift, tiles_per_seq=tiles_per_seq, nf=nf)
    in_specs = [
        pl.BlockSpec((tm, D_MODEL), lambda i, f, *_: (i, 0)),
        pl.BlockSpec((1, D_MODEL), lambda i, f, *_: (0, 0)),
        pl.BlockSpec((1, CONV_HALO, FFN_TF), lambda i, f, *_: (i // tiles_per_seq, 0, f)),
        pl.BlockSpec((D_MODEL, FFN_TF), lambda i, f, *_: (0, f)),
        pl.BlockSpec((D_MODEL, FFN_TF), lambda i, f, *_: (0, f)),
        pl.BlockSpec((CONV_W, FFN_TF), lambda i, f, *_: (0, f)),
        pl.BlockSpec((1, FFN_TF), lambda i, f, *_: (0, f)),
        pl.BlockSpec((FFN_TF, D_MODEL), lambda i, f, *_: (f, 0)),
    ]
    assert len(in_specs) == N_FFN_IN
    out_specs = [
        pl.BlockSpec((tm, D_MODEL), lambda i, f, *_: (i, 0)),
        pl.BlockSpec((1, 2 * shift, FFN_TF), lambda i, f, *_: (i, 0, f)),
    ]
    out_shape = [
        jax.ShapeDtypeStruct((m, D_MODEL), F32),
        jax.ShapeDtypeStruct((m // tm, 2 * shift, D_FF_PAD), F32),
    ]
    scratch = [
        pltpu.VMEM((tm, D_MODEL), BF16),
        pltpu.VMEM((tm, D_MODEL), F32),
        pltpu.VMEM((nf, CONV_HALO, FFN_TF), F32),
    ]
    args = (h, g, prev, wg, wv, cw, cb, wdn)
    grid = (m // tm, nf)
    if attn is None:
        return pl.pallas_call(
            functools.partial(_ffn_kernel, **static),
            grid=grid, in_specs=in_specs, out_specs=out_specs, out_shape=out_shape,
            scratch_shapes=scratch,
            compiler_params=_params(("arbitrary", "arbitrary")),
            name="conv_ffn",
        )(*args)

    (page_table, qbd, k_pool, v_pool, bias_last, k_new, v_new, bias_new, lam_rows, subln_g,
     n_pages_step, lam_init) = attn
    n_seq, n_pages = page_table.shape
    rows = qbd.shape[1]
    n_new = rows // (2 * N_HEADS)
    steps_per_seq = n_pages // n_pages_step
    last = n_seq * steps_per_seq - 1
    assert grid[0] * grid[1] > last

    def seq_of(i, f):
        return jnp.minimum(i * nf + f, last) // steps_per_seq

    def page_spec(p, shape):
        def index(i, f, pt):
            s = jnp.minimum(i * nf + f, last)
            return (pt[s // steps_per_seq, (s % steps_per_seq) * n_pages_step + p], 0, 0)
        return pl.BlockSpec((1,) + shape, index)

    in_specs += (
        [pl.BlockSpec((4, QK_DIM), lambda i, f, pt: (0, 0)),
         pl.BlockSpec((1, rows, ATT_W), lambda i, f, pt: (seq_of(i, f), 0, 0))]
        + [page_spec(p, (ATT_W, PAGE_SIZE)) for p in range(n_pages_step)]
        + [page_spec(p, (PAGE_SIZE * N_HEADS, V_HEAD)) for p in range(n_pages_step)]
        + [pl.BlockSpec((rows, PAGE_SIZE), lambda i, f, pt: (0, 0)),
           pl.BlockSpec((1, ATT_W, PAGE_SIZE), lambda i, f, pt: (seq_of(i, f), 0, 0)),
           pl.BlockSpec((1, PAGE_SIZE * N_HEADS, V_HEAD), lambda i, f, pt: (seq_of(i, f), 0, 0)),
           pl.BlockSpec((rows, PAGE_SIZE), lambda i, f, pt: (0, 0)),
           pl.BlockSpec((N_HEADS, V_HEAD), lambda i, f, pt: (0, 0))])
    out_specs.append(pl.BlockSpec((1, n_new, ATT_W), lambda i, f, pt: (seq_of(i, f), 0, 0)))
    out_shape.append(jax.ShapeDtypeStruct((n_seq, n_new, ATT_W), F32))
    scratch += [pltpu.VMEM((rows, LANES), F32), pltpu.VMEM((rows, LANES), F32),
                pltpu.VMEM((rows, V_HEAD), F32)]
    kern = functools.partial(_ffn_attn_kernel, n_pages_step=n_pages_step, steps_per_seq=steps_per_seq,
                             n_seq=n_seq, n_new=n_new, lam_init=lam_init, **static)
    return pl.pallas_call(
        kern,
        grid_spec=pltpu.PrefetchScalarGridSpec(
            num_scalar_prefetch=1, grid=grid, in_specs=in_specs, out_specs=out_specs,
            scratch_shapes=scratch),
        out_shape=out_shape,
        compiler_params=_params(("arbitrary", "arbitrary")),
        name="conv_ffn_attn",
    )(page_table, *args, lam_rows, qbd, *([k_pool] * n_pages_step), *([v_pool] * n_pages_step),
      bias_last, k_new, v_new, bias_new, subln_g)


def _ple_kernel(h_ref, gp_ref, pe_ref, wg_ref, wp_ref, gf_ref, o_ref):
    h = h_ref[...]
    hn = _rms(h, gp_ref[...], NORM_EPS).astype(BF16)
    gate = jax.nn.sigmoid(jnp.dot(hn, wg_ref[...], preferred_element_type=F32))
    proj = jnp.dot(pe_ref[...].astype(BF16), wp_ref[...], preferred_element_type=F32)
    o_ref[...] = _rms(h + gate * proj, gf_ref[...], NORM_EPS)


def _ple(h, gp, pe, wg, wp, gf, tm):
    m = h.shape[0]
    return pl.pallas_call(
        _ple_kernel,
        grid=(m // tm,),
        in_specs=[
            pl.BlockSpec((tm, D_MODEL), lambda i: (i, 0)),
            pl.BlockSpec((1, D_MODEL), lambda i: (0, 0)),
            pl.BlockSpec((tm, PLE_DIM), lambda i: (i, 0)),
            pl.BlockSpec((D_MODEL, D_MODEL), lambda i: (0, 0)),
            pl.BlockSpec((PLE_DIM, D_MODEL), lambda i: (0, 0)),
            pl.BlockSpec((1, D_MODEL), lambda i: (0, 0)),
        ],
        out_specs=pl.BlockSpec((tm, D_MODEL), lambda i: (i, 0)),
        out_shape=jax.ShapeDtypeStruct((m, D_MODEL), F32),
        compiler_params=_params(("arbitrary",)),
        name="ple_final",
    )(h, gp, pe, wg, wp, gf)


def _t5_bucket(dist):
    max_exact = N_BUCKETS // 2
    n = jnp.maximum(dist, 0)
    nf = jnp.maximum(n, 1).astype(F32)
    large = max_exact + (jnp.log(nf / max_exact) / math.log(MAX_DISTANCE / max_exact)
                         * (N_BUCKETS - max_exact)).astype(jnp.int32)
    large = jnp.minimum(large, N_BUCKETS - 1)
    return jnp.where(n < max_exact, n, large)


def _toeplitz(w, n):
    lead = w.shape[:-1]
    flat = jnp.tile(w, (1,) * len(lead) + (n,))[..., :n * (2 * n - 1)]
    return flat.reshape(lead + (n, 2 * n - 1))[..., :n]


def _prompt_bias_tiles(table, n):
    assert n >= MAX_DISTANCE
    far = table[N_BUCKETS - 1]
    bvec = (table[_t5_bucket(jnp.arange(2 * n, dtype=jnp.int32))] - far[None, :]).T
    bvec = bvec * LOG2E
    d = jnp.arange(2 * n, dtype=jnp.int32)
    d = jnp.where(d >= n, d - 2 * n, d)
    diag = jnp.where(d[None, :] <= 0, bvec[:, jnp.abs(d)], -jnp.inf)
    sub = bvec[:, jnp.clip(n - d, 0, 2 * n - 1)]
    return jnp.concatenate([_toeplitz(sub, n), _toeplitz(diag, n)], axis=-1)


def _sample_bias(table, n_new):
    assert PAGE_SIZE >= MAX_DISTANCE
    far = table[N_BUCKETS - 1]
    t = jnp.arange(n_new, dtype=jnp.int32)
    col = jnp.arange(PAGE_SIZE, dtype=jnp.int32)

    def rows(b):
        b = jnp.transpose(b * LOG2E, (2, 0, 1))
        return jnp.broadcast_to(b[:, None], (N_HEADS, 2, n_new, PAGE_SIZE)).reshape(-1, PAGE_SIZE)

    dist_last = PAGE_SIZE + t[:, None] - col[None, :]
    dist_new = t[:, None] - col[None, :]
    b_new = jnp.where((dist_new >= 0)[..., None], table[_t5_bucket(dist_new)] - far, -jnp.inf)
    return rows(table[_t5_bucket(dist_last)] - far), rows(b_new)


def _block_diag(w):
    nb, gp, a, b = w.shape
    eye = jnp.eye(gp, dtype=w.dtype)
    out = w[:, :, :, None, :] * eye[None, :, None, :, None]
    return out.reshape(nb, gp * a, gp * b)


def _ssm_tables(a_re, a_im, b_re, b_im, c_re, c_im, log_dt):
    g, n, gc = N_SSM_GROUPS, SSM_STATE, SSM_GROUP
    dt = jnp.exp(log_dt)[:, None]
    decay = jnp.exp(a_re * dt)
    ab_re = decay * jnp.cos(a_im * dt)
    ab_im = decay * jnp.sin(a_im * dt)
    den = a_re * a_re + a_im * a_im
    num_re = ab_re - 1.0
    coef_re = (num_re * a_re + ab_im * a_im) / den
    coef_im = (ab_im * a_re - num_re * a_im) / den
    bb_re = coef_re[..., None] * b_re - coef_im[..., None] * b_im
    bb_im = coef_re[..., None] * b_im + coef_im[..., None] * b_re
    blk = lambda w: w.reshape((N_GROUP_BLOCKS, GROUPS_PER_BLOCK) + w.shape[1:])
    to_cn = lambda w: jnp.swapaxes(blk(w), 2, 3)
    wb = jnp.concatenate([_block_diag(to_cn(bb_re)), _block_diag(to_cn(bb_im))], axis=2)
    to_nc = lambda w: jnp.swapaxes(blk(w), 2, 3)
    cb = jnp.concatenate([_block_diag(to_nc(c_re)), _block_diag(to_nc(-c_im))], axis=1)
    lr = ab_re.reshape(N_GROUP_BLOCKS, STATE_COLS // 2, LANES)
    li = ab_im.reshape(N_GROUP_BLOCKS, STATE_COLS // 2, LANES)
    lam = jnp.stack([jnp.concatenate([lr, lr], axis=1), jnp.concatenate([-li, li], axis=1)], axis=1)
    return wb.astype(BF16), cb.astype(BF16), lam.astype(F32)


def _cast_in_kernel(w_ref, wb_ref, wkt_ref):
    w = w_ref[...]
    wb_ref[...] = w.astype(BF16)
    wkt_ref[...] = w[:, ATT_W:2 * ATT_W].T.astype(BF16)


def _cast_w_in(w, tr=256):
    return pl.pallas_call(
        _cast_in_kernel,
        grid=(D_MODEL // tr,),
        in_specs=[pl.BlockSpec((tr, 4 * ATT_W), lambda i: (i, 0))],
        out_specs=[pl.BlockSpec((tr, 4 * ATT_W), lambda i: (i, 0)),
                   pl.BlockSpec((ATT_W, tr), lambda i: (0, i))],
        out_shape=[jax.ShapeDtypeStruct((D_MODEL, 4 * ATT_W), BF16),
                   jax.ShapeDtypeStruct((ATT_W, D_MODEL), BF16)],
        compiler_params=_params(("arbitrary",)),
        name="cast_w_in",
    )(w)


def _cast_up_kernel(w_ref, wg_ref, wv_ref):
    zeros = jnp.zeros((wg_ref.shape[0], D_FF_PAD - D_FF), BF16)
    for n, o_ref in enumerate((wg_ref, wv_ref)):
        o_ref[:, 0:D_FF] = w_ref[:, n * D_FF:(n + 1) * D_FF].astype(BF16)
        o_ref[:, D_FF:D_FF_PAD] = zeros


def _cast_w_up(w, tr=64):
    out = jax.ShapeDtypeStruct((D_MODEL, D_FF_PAD), BF16)
    spec = pl.BlockSpec((tr, D_FF_PAD), lambda i: (i, 0))
    return pl.pallas_call(
        _cast_up_kernel,
        grid=(D_MODEL // tr,),
        in_specs=[pl.BlockSpec((tr, 2 * D_FF), lambda i: (i, 0))],
        out_specs=[spec, spec],
        out_shape=[out, out],
        compiler_params=_params(("arbitrary",)),
        name="cast_w_up",
    )(w)


def _cast_down_kernel(w_ref, o_ref, *, n_valid):
    keep = pl.program_id(0) < n_valid
    o_ref[...] = jnp.where(keep, w_ref[...], 0.0).astype(BF16)


def _cast_w_down(w, tr=LANES):
    assert D_FF % tr == 0 and D_FF_PAD % tr == 0
    n_valid = D_FF // tr
    return pl.pallas_call(
        functools.partial(_cast_down_kernel, n_valid=n_valid),
        grid=(D_FF_PAD // tr,),
        in_specs=[pl.BlockSpec((tr, D_MODEL), lambda i: (jnp.minimum(i, n_valid - 1), 0))],
        out_specs=pl.BlockSpec((tr, D_MODEL), lambda i: (i, 0)),
        out_shape=jax.ShapeDtypeStruct((D_FF_PAD, D_MODEL), BF16),
        compiler_params=_params(("arbitrary",)),
        name="cast_w_down",
    )(w)


def _pad_ff(w, axis):
    pad = [(0, 0)] * w.ndim
    pad[axis] = (0, D_FF_PAD - D_FF)
    return jnp.pad(w, pad)


def _state_to_tiles(re, im):
    shape = (re.shape[0], N_GROUP_BLOCKS, STATE_COLS // 2, LANES)
    return jnp.concatenate([re.reshape(shape), im.reshape(shape)], axis=2)


def _tiles_to_state(tiles):
    half = STATE_COLS // 2
    shape = (1, tiles.shape[0], N_SSM_GROUPS, SSM_STATE)
    return tiles[:, :, :half].reshape(shape), tiles[:, :, half:].reshape(shape)


def kernel(x_prompt, x_sample, cache_k, cache_v, state_ssm_re, state_ssm_im, state_conv, page_table, p_prompt, p_sample, rel_bias, norm_mix_g, w_in, lambda_q1, lambda_k1, lambda_q2, lambda_k2, attn_subln_g, ssm_a_re, ssm_a_im, ssm_b_re, ssm_b_im, ssm_c_re, ssm_c_im, ssm_d, ssm_log_dt, ssm_glu_w, ssm_glu_b, ssm_out_g, w_out, norm_ffn_g, ffn_w_up, ffn_conv_w, ffn_conv_b, ffn_w_down, norm_ple_g, ple_w_gate, ple_w_proj, norm_final_g):
    assert DEPTH == 1
    li = 0
    lam_init = 0.8 - 0.6 * math.exp(-0.3 * li)
    nb, t, _ = x_prompt.shape
    db, ds, _ = x_sample.shape
    assert db == SUBLANES
    m_p = nb * t
    m_s = db * ds

    row = lambda v: v.reshape(1, -1).astype(F32)
    w_in_b, w_kt_b = _cast_w_in(w_in[li])
    w_out_b = w_out[li].astype(BF16).reshape(2, ATT_W, D_MODEL)
    w_ug_b, w_uv_b = _cast_w_up(ffn_w_up[li])
    w_dn_b = _cast_w_down(ffn_w_down[li])
    conv_w = _pad_ff(ffn_conv_w[li], 1)
    conv_b = _pad_ff(ffn_conv_b[li].reshape(1, D_FF), 1)
    w_gate_b = ple_w_gate[li].astype(BF16)
    w_proj_b = ple_w_proj[li].astype(BF16)
    lam_rows = jnp.stack([lambda_q1[li], lambda_k1[li], lambda_q2[li], lambda_k2[li]]).astype(F32)
    subln_g = attn_subln_g[li].astype(F32)
    wb, cb, tbl = _ssm_tables(ssm_a_re[li], ssm_a_im[li], ssm_b_re[li], ssm_b_im[li],
                              ssm_c_re[li], ssm_c_im[li], ssm_log_dt[li])
    glu_w = _block_diag(ssm_glu_w[li].reshape(N_GROUP_BLOCKS, GROUPS_PER_BLOCK, SSM_GROUP, SSM_GROUP)).astype(BF16)
    ssm_args = (wb, cb, tbl, row(ssm_d[li]), glu_w, row(ssm_glu_b[li]), row(ssm_out_g[li]))

    tq = 512
    xp = x_prompt.reshape(m_p, D_MODEL)
    q_p, kt_p, v_p, u_p = _in_proj(xp, row(norm_mix_g[li]), w_in_b, w_kt_b, tm=512,
                                   rows_per_seq=t, k_transposed=True)
    shp = lambda a: a.reshape(nb, t, -1)
    band = _prompt_bias_tiles(rel_bias.astype(F32), MAX_DISTANCE)
    o_att_p = _attn_prompt(shp(q_p), kt_p, shp(v_p), band, lam_rows,
                           subln_g.reshape(N_HEADS, 1, V_HEAD), tq, 2, lam_init)
    zero_state = jnp.zeros((nb, N_GROUP_BLOCKS, 1, STATE_COLS, LANES), F32)
    o_ssm_p, hf_p = _ssm(shp(u_p), zero_state, *ssm_args, tc=256, chains=1, out_dtype=BF16)
    h1_p = _out_proj(xp, o_att_p.reshape(m_p, ATT_W), o_ssm_p.reshape(m_p, SSM_W), w_out_b, tm=512)

    tmaj = lambda a: jnp.swapaxes(a, 0, 1).reshape(m_s, -1)
    bmaj = lambda a: jnp.swapaxes(a.reshape(ds, db, -1), 0, 1)
    xs = tmaj(x_sample)
    q_s, k_s, v_s, u_s = _in_proj(xs, row(norm_mix_g[li]), w_in_b, w_kt_b, tm=m_s,
                                  rows_per_seq=m_s, k_transposed=False)
    q_b = bmaj(q_s)
    q_rep = jnp.tile(q_b, (1, 2 * N_HEADS, 1)).reshape(db, 2 * N_HEADS, ds, ATT_W)
    sel = (jnp.arange(ATT_W)[None, :] // QK_DIM) == jnp.arange(2 * N_HEADS)[:, None]
    qbd = jnp.where(sel[None, :, None, :], q_rep, jnp.zeros_like(q_rep)).reshape(db, 2 * N_HEADS * ds, ATT_W)
    pad_rows = lambda a: jnp.pad(bmaj(a), ((0, 0), (0, PAGE_SIZE - ds), (0, 0)))
    bias_past, bias_new = _sample_bias(rel_bias.astype(F32), ds)
    n_pool = cache_k.shape[1]
    k_pool_t = jnp.transpose(cache_k[li], (0, 2, 3, 4, 1)).reshape(n_pool, ATT_W, PAGE_SIZE)
    k_new_t = jnp.swapaxes(pad_rows(k_s), 1, 2)
    v_pool_r = cache_v[li].reshape(n_pool, PAGE_SIZE * N_HEADS, V_HEAD)
    v_new_r = pad_rows(v_s).reshape(db, PAGE_SIZE * N_HEADS, V_HEAD)
    tm_ffn = 512
    zero_prev = jnp.zeros((nb, CONV_HALO, D_FF_PAD), F32)
    attn_s = (page_table, qbd, k_pool_t, v_pool_r, bias_past, k_new_t, v_new_r, bias_new,
              lam_rows, subln_g, 8, lam_init)
    h2_p, conv_p, o_att_s = _ffn(h1_p, row(norm_ffn_g[li]), zero_prev, w_ug_b, w_uv_b, conv_w, conv_b,
                                 w_dn_b, tm=tm_ffn, shift=1, tiles_per_seq=t // tm_ffn, attn=attn_s)
    y_p = _ple(h2_p, row(norm_ple_g[li]), p_prompt[li].reshape(m_p, PLE_DIM), w_gate_b, w_proj_b,
               row(norm_final_g), tm=512)
    h0_s = jnp.swapaxes(_state_to_tiles(state_ssm_re[li], state_ssm_im[li]), 0, 1)[None]
    o_ssm_s, hf_s = _ssm(u_s[None], h0_s, *ssm_args, tc=m_s, chains=db, out_dtype=F32)
    h1_s = _out_proj(xs, tmaj(o_att_s), o_ssm_s[0], w_out_b, tm=m_s)
    prev_s = _pad_ff(jnp.swapaxes(state_conv[li], 0, 1).reshape(1, 2 * db, D_FF), 2)
    h2_s, conv_s = _ffn(h1_s, row(norm_ffn_g[li]), prev_s, w_ug_b, w_uv_b, conv_w, conv_b, w_dn_b,
                        tm=m_s, shift=db, tiles_per_seq=1)
    y_s = _ple(h2_s, row(norm_ple_g[li]), tmaj(p_sample[li]), w_gate_b, w_proj_b,
               row(norm_final_g), tm=m_s)

    re_p, im_p = _tiles_to_state(hf_p[:, :, 0])
    re_s, im_s = _tiles_to_state(jnp.swapaxes(hf_s[0], 0, 1))
    tiles = t // tm_ffn
    conv_p = conv_p[tiles - 1::tiles][None, :, :, :D_FF]
    conv_s = jnp.swapaxes(conv_s.reshape(2, db, D_FF_PAD), 0, 1)[None, :, :, :D_FF]
    return (y_p.reshape(nb, t, D_MODEL),
            bmaj(y_s),
            jnp.transpose(kt_p.reshape(1, nb, N_HEADS, 2, QK_DIM, t), (0, 1, 5, 2, 3, 4)),
            v_p.reshape(1, nb, t, N_HEADS, V_HEAD),
            re_p, im_p, conv_p,
            bmaj(k_s).reshape(1, db, ds, N_HEADS, 2, QK_DIM),
            bmaj(v_s).reshape(1, db, ds, N_HEADS, V_HEAD),
            re_s, im_s, conv_s)
```

```python
import functools
import math

import jax
import jax.numpy as jnp
from jax import lax
from jax.experimental import pallas as pl
from jax.experimental.pallas import tpu as pltpu

F32 = jnp.float32
BF16 = jnp.bfloat16

D_MODEL = 2048
DEPTH = 1
PAGE_SIZE = 128
ATT_W = 1024
SSM_W = 1024
V_HEAD = 128
N_HEADS = 8
QK_DIM = 64
SSM_GROUP = 16
N_SSM_GROUPS = 64
SSM_STATE = 64
D_FF = 5504
CONV_W = 3
PLE_DIM = 256
N_BUCKETS = 32
MAX_DISTANCE = 128
NORM_EPS = 1e-6
SUBLN_EPS = 1e-5
LOG2E = math.log2(math.e)
QK_SCALE = QK_DIM ** -0.5 * LOG2E

LANES = 128
SUBLANES = 8
MXU_WIDTH = 256
VMEM_LIMIT = 56 * 1024 * 1024

GROUPS_PER_BLOCK = LANES // SSM_GROUP
N_GROUP_BLOCKS = N_SSM_GROUPS // GROUPS_PER_BLOCK
STATE_LANES = GROUPS_PER_BLOCK * SSM_STATE

D_FF_PAD = 5632
FFN_TF = 512
CONV_HALO = 16


def _params(sem):
    return pltpu.CompilerParams(dimension_semantics=sem, vmem_limit_bytes=VMEM_LIMIT)


def _rms(x, g, eps):
    return x * lax.rsqrt(jnp.mean(x * x, axis=-1, keepdims=True) + eps) * g


def _in_proj_kernel(x_ref, g_ref, w_ref, wkt_ref, q_ref, k_ref, v_ref, u_ref, *, k_transposed):
    xn = _rms(x_ref[...], g_ref[...], NORM_EPS).astype(BF16)

    def proj(n):
        return jnp.dot(xn, w_ref[:, n * ATT_W:(n + 1) * ATT_W], preferred_element_type=F32)

    q_ref[...] = (proj(0) * QK_SCALE).astype(BF16)
    if k_transposed:
        k_ref[0] = lax.dot_general(wkt_ref[...], xn, (((1,), (1,)), ((), ())),
                                   preferred_element_type=F32)
    else:
        k_ref[...] = proj(1)
    v_ref[...] = proj(2)
    u_ref[...] = proj(3)


def _resident(shape):
    return pl.BlockSpec(shape, lambda *_: (0,) * len(shape), pipeline_mode=pl.Buffered(1))


def _in_proj(x, g, w, wkt, tm, rows_per_seq, k_transposed):
    m = x.shape[0]
    out_spec = pl.BlockSpec((tm, ATT_W), lambda i: (i, 0))
    if k_transposed:
        tiles = rows_per_seq // tm
        k_spec = pl.BlockSpec((1, ATT_W, tm), lambda i: (i // tiles, 0, i % tiles))
        k_shape = jax.ShapeDtypeStruct((m // rows_per_seq, ATT_W, rows_per_seq), F32)
    else:
        k_spec = out_spec
        k_shape = jax.ShapeDtypeStruct((m, ATT_W), F32)
    return pl.pallas_call(
        functools.partial(_in_proj_kernel, k_transposed=k_transposed),
        grid=(m // tm,),
        in_specs=[
            pl.BlockSpec((tm, D_MODEL), lambda i: (i, 0)),
            _resident((1, D_MODEL)),
            _resident((D_MODEL, 4 * ATT_W)),
            _resident((ATT_W, D_MODEL)),
        ],
        out_specs=[out_spec, k_spec, out_spec, out_spec],
        out_shape=[
            jax.ShapeDtypeStruct((m, ATT_W), BF16),
            k_shape,
            jax.ShapeDtypeStruct((m, ATT_W), F32),
            jax.ShapeDtypeStruct((m, SSM_W), F32),
        ],
        compiler_params=_params(("arbitrary",)),
        name="in_proj",
    )(x, g, w, wkt)


def _lambda(lam_ref, lam_init):
    s1 = jnp.sum(lam_ref[0:1, :] * lam_ref[1:2, :], axis=-1, keepdims=True)
    s2 = jnp.sum(lam_ref[2:3, :] * lam_ref[3:4, :], axis=-1, keepdims=True)
    return jnp.exp(s1) - jnp.exp(s2) + lam_init


def _online_softmax_step(s, v_blk, m_ref, l_ref, acc_ref):
    m_prev = m_ref[...]
    m_new = jnp.maximum(m_prev, jnp.max(s, axis=-1, keepdims=True))
    alpha = jnp.exp2(m_prev - m_new)
    p = jnp.exp2(s - jnp.concatenate([m_new] * (s.shape[1] // LANES), axis=1))
    l_ref[...] = alpha * l_ref[...] + jnp.sum(p, axis=-1, keepdims=True)
    acc_ref[...] = alpha * acc_ref[...] + jnp.dot(
        p.astype(BF16), v_blk, preferred_element_type=F32)
    m_ref[...] = m_new


def _attn_prompt_kernel(lam_ref, q_ref, k_ref, v_ref, band_ref, g_ref, o_ref,
                        kb_ref, vb_ref, qs_ref, bias_ref, m_ref, l_ref, acc_ref, *, tq, nh, lam_init):
    i = pl.program_id(2)
    head_cols = [slice(h * V_HEAD, (h + 1) * V_HEAD) for h in range(nh)]

    @pl.when(i == 0)
    def _():
        kb_ref[...] = k_ref[0].astype(BF16)
        vb_ref[...] = v_ref[0].astype(BF16)
        w = band_ref.shape[1]
        zero = jnp.zeros((w, w), F32)
        masked = jnp.full((w, w), -jnp.inf, F32)
        n = tq // w
        for h in range(nh):
            left, right = band_ref[h, :, 0:w], band_ref[h, :, w:2 * w]
            for r in range(n):
                for c in range(n):
                    rows = slice(r * w, (r + 1) * w)
                    bias_ref[h, rows, c * w:(c + 1) * w] = left if (r == 0 and c == n - 1) else zero
                    bias_ref[h, rows, tq + c * w:tq + (c + 1) * w] = (
                        right if c == r else left if c == r - 1 else masked if c > r else zero)

    for h in range(nh):
        q = q_ref[0, :, head_cols[h]]
        lane = lax.broadcasted_iota(jnp.int32, q.shape, 1)
        zero = jnp.zeros_like(q)
        qs_ref[h, 0:tq, :] = jnp.where(lane < QK_DIM, q, zero)
        qs_ref[h, tq:2 * tq, :] = jnp.where(lane >= QK_DIM, q, zero)
    m_ref[...] = jnp.full(m_ref.shape, -jnp.inf, F32)
    l_ref[...] = jnp.zeros(l_ref.shape, F32)
    acc_ref[...] = jnp.zeros(acc_ref.shape, F32)

    def update(j, width, biased):
        off = pl.multiple_of(j * tq, tq)
        for h in range(nh):
            rows = slice(h * 2 * QK_DIM, (h + 1) * 2 * QK_DIM)
            s = jnp.dot(qs_ref[h], kb_ref[rows, pl.ds(off, width)], preferred_element_type=F32)
            if biased:
                bias = bias_ref[h, :, 2 * tq - width:2 * tq]
                s = (s.reshape(2, tq, width) + bias[None]).reshape(2 * tq, width)
            _online_softmax_step(s, vb_ref[pl.ds(off, width), head_cols[h]],
                                 m_ref.at[h], l_ref.at[h], acc_ref.at[h])

    def far_body(j, carry):
        update(j, tq, False)
        return carry

    lax.fori_loop(0, jnp.maximum(i - 1, 0), far_body, 0)

    @pl.when(i >= 1)
    def _():
        update(i - 1, 2 * tq, True)

    @pl.when(i == 0)
    def _():
        update(0, tq, True)

    lam = _lambda(lam_ref, lam_init)
    for h in range(nh):
        inv_l = 1.0 / l_ref[h]
        o = acc_ref[h, 0:tq, :] * inv_l[0:tq] - lam * (acc_ref[h, tq:2 * tq, :] * inv_l[tq:2 * tq])
        o_ref[0, :, head_cols[h]] = (_rms(o, g_ref[h], SUBLN_EPS) * (1.0 - lam_init)).astype(o_ref.dtype)


def _attn_prompt(q, k, v, band, lam_rows, subln_g, tq, nh, lam_init):
    b, t, _ = q.shape
    w = band.shape[1]
    assert tq % w == 0 and band.shape[2] == 2 * w and N_HEADS % nh == 0
    kern = functools.partial(_attn_prompt_kernel, tq=tq, nh=nh, lam_init=lam_init)
    gw = nh * V_HEAD
    return pl.pallas_call(
        kern,
        grid=(b, N_HEADS // nh, t // tq),
        in_specs=[
            pl.BlockSpec((4, QK_DIM), lambda bi, h, i: (0, 0)),
            pl.BlockSpec((1, tq, gw), lambda bi, h, i: (bi, i, h)),
            pl.BlockSpec((1, gw, t), lambda bi, h, i: (bi, h, 0)),
            pl.BlockSpec((1, t, gw), lambda bi, h, i: (bi, 0, h)),
            pl.BlockSpec((nh, w, 2 * w), lambda bi, h, i: (h, 0, 0)),
            pl.BlockSpec((nh, 1, V_HEAD), lambda bi, h, i: (h, 0, 0)),
        ],
        out_specs=pl.BlockSpec((1, tq, gw), lambda bi, h, i: (bi, i, h)),
        out_shape=jax.ShapeDtypeStruct((b, t, ATT_W), BF16),
        scratch_shapes=[
            pltpu.VMEM((gw, t), BF16),
            pltpu.VMEM((t, gw), BF16),
            pltpu.VMEM((nh, 2 * tq, V_HEAD), BF16),
            pltpu.VMEM((nh, tq, 2 * tq), F32),
            pltpu.VMEM((nh, 2 * tq, LANES), F32),
            pltpu.VMEM((nh, 2 * tq, LANES), F32),
            pltpu.VMEM((nh, 2 * tq, V_HEAD), F32),
        ],
        compiler_params=_params(("arbitrary", "arbitrary", "arbitrary")),
        name="attn_prompt",
    )(lam_rows, q, k, v, band, subln_g)


def _attn_sample_parts(step, n_steps, lam_ref, qbd_ref, k_refs, v_refs, bias_ref, kn_ref, vn_ref, bn_ref,
                       g_ref, o_ref, m_ref, l_ref, acc_ref, *, n_new, lam_init):
    def init():
        @pl.when(step == 0)
        def _():
            m_ref[...] = jnp.full(m_ref.shape, -jnp.inf, F32)
            l_ref[...] = jnp.zeros(l_ref.shape, F32)
            acc_ref[...] = jnp.zeros(acc_ref.shape, F32)

    qbd = qbd_ref[0]
    rows = 2 * n_new

    def update(k_pages, v_page_refs, bias_last):
        parts = [jnp.dot(qbd, kp.astype(BF16), preferred_element_type=F32) for kp in k_pages]
        parts[-1] = parts[-1] + bias_last
        s = jnp.concatenate(parts, axis=1)
        m_prev = m_ref[...]
        m_new = jnp.maximum(m_prev, jnp.max(s, axis=-1, keepdims=True))
        alpha = jnp.exp2(m_prev - m_new)
        p = jnp.exp2(s - jnp.concatenate([m_new] * len(parts), axis=1))
        l_ref[...] = alpha * l_ref[...] + jnp.sum(p, axis=-1, keepdims=True)
        m_ref[...] = m_new
        pb = p.astype(BF16)
        for h in range(N_HEADS):
            v_h = jnp.concatenate(
                [vr[0, pl.ds(h, PAGE_SIZE, stride=N_HEADS), :].astype(BF16) for vr in v_page_refs], axis=0)
            r = slice(h * rows, (h + 1) * rows)
            acc_ref[r, :] = alpha[r] * acc_ref[r, :] + jnp.dot(pb[r], v_h, preferred_element_type=F32)

    is_last = step == n_steps - 1

    def main():
        bias_last = jnp.where(is_last, bias_ref[...], jnp.zeros(bias_ref.shape, F32))
        update([kr[0] for kr in k_refs], v_refs, bias_last)

    def finish():
        @pl.when(is_last)
        def _():
            update([kn_ref[0]], [vn_ref], bn_ref[...])
            lam = _lambda(lam_ref, lam_init)
            acc = acc_ref[...] * (1.0 / l_ref[...])
            for h in range(N_HEADS):
                blk = acc[h * rows:(h + 1) * rows]
                o = blk[0:n_new] - lam * blk[n_new:rows]
                o = _rms(o, g_ref[h:h + 1, :], SUBLN_EPS) * (1.0 - lam_init)
                o_ref[0, :, h * V_HEAD:(h + 1) * V_HEAD] = o

    return init, main, finish


STATE_COLS = 2 * STATE_LANES // LANES


def _state_rows(g, tc, col):
    return pl.ds(g * tc * STATE_COLS + col, tc, stride=STATE_COLS)


def _ssm_kernel(u_ref, h0_ref, wb_ref, cb_ref, lam_ref, d_ref, gw_ref, gbias_ref, og_ref,
                o_ref, hf_ref, hb_ref, gbuf_ref, carry_ref, *, tc, chains):
    c = pl.program_id(1)

    @pl.when(c == 0)
    def _():
        carry_ref[...] = h0_ref[0]

    for g in range(N_GROUP_BLOCKS):
        u = u_ref[0, :, g * LANES:(g + 1) * LANES]
        bu = jnp.dot(u.astype(BF16), wb_ref[g], preferred_element_type=F32)
        for k in range(STATE_COLS):
            hb_ref[_state_rows(g, tc, k), :] = bu[:, k * LANES:(k + 1) * LANES]

    coef = [(lam_ref[g, 0], lam_ref[g, 1]) for g in range(N_GROUP_BLOCKS)]

    def step(g, row, h):
        start = (g * tc + row) * STATE_COLS
        if not isinstance(start, int):
            start = pl.multiple_of(start, STATE_COLS)
        rows = pl.ds(start, STATE_COLS)
        a, b = coef[g]
        h = a * h + b * pltpu.roll(h, STATE_COLS // 2, 0) + hb_ref[rows, :]
        hb_ref[rows, :] = h
        return h

    if chains == 1:
        def body(t, hs):
            return tuple(step(g, t, hs[g]) for g in range(N_GROUP_BLOCKS))

        hs = lax.fori_loop(0, tc, body, tuple(carry_ref[g, 0] for g in range(N_GROUP_BLOCKS)),
                           unroll=SUBLANES)
        for g in range(N_GROUP_BLOCKS):
            carry_ref[g, 0] = hs[g]
    else:
        for g in range(N_GROUP_BLOCKS):
            hs = [carry_ref[g, b] for b in range(chains)]
            for row in range(tc):
                hs[row % chains] = step(g, row, hs[row % chains])
            for b in range(chains):
                carry_ref[g, b] = hs[b]

    ssq = jnp.zeros((tc, 1), F32)
    for g in range(N_GROUP_BLOCKS):
        cols = slice(g * LANES, (g + 1) * LANES)
        h = jnp.concatenate([hb_ref[_state_rows(g, tc, k), :] for k in range(STATE_COLS)], axis=1)
        y = jnp.dot(h.astype(BF16), cb_ref[g], preferred_element_type=F32)
        y = y + d_ref[:, cols] * u_ref[0, :, cols]
        a = jax.nn.gelu(y)
        gate = jnp.dot(a.astype(BF16), gw_ref[g], preferred_element_type=F32) + gbias_ref[:, cols]
        a = a * jax.nn.sigmoid(gate)
        gbuf_ref[:, cols] = a
        ssq = ssq + jnp.sum(a * a, axis=-1, keepdims=True)
    inv = lax.rsqrt(ssq * (1.0 / SSM_W) + NORM_EPS)
    o_ref[0] = (gbuf_ref[...] * inv * og_ref[...]).astype(o_ref.dtype)
    hf_ref[0] = carry_ref[...]


def _ssm(u, h0, wb, cb, lam, d, gw, gbias, og, tc, chains, out_dtype):
    ns, t, _ = u.shape
    kern = functools.partial(_ssm_kernel, tc=tc, chains=chains)
    full = lambda *shape: pl.BlockSpec(shape, lambda s, c: (0,) * len(shape))
    state_shape = (N_GROUP_BLOCKS, chains, STATE_COLS, LANES)
    state_spec = pl.BlockSpec((1,) + state_shape, lambda s, c: (s, 0, 0, 0, 0))
    return pl.pallas_call(
        kern,
        grid=(ns, t // tc),
        in_specs=[
            pl.BlockSpec((1, tc, SSM_W), lambda s, c: (s, c, 0)),
            state_spec,
            full(N_GROUP_BLOCKS, LANES, 2 * STATE_LANES),
            full(N_GROUP_BLOCKS, 2 * STATE_LANES, LANES),
            full(N_GROUP_BLOCKS, 2, STATE_COLS, LANES),
            full(1, SSM_W),
            full(N_GROUP_BLOCKS, LANES, LANES),
            full(1, SSM_W),
            full(1, SSM_W),
        ],
        out_specs=[pl.BlockSpec((1, tc, SSM_W), lambda s, c: (s, c, 0)), state_spec],
        out_shape=[
            jax.ShapeDtypeStruct((ns, t, SSM_W), out_dtype),
            jax.ShapeDtypeStruct((ns,) + state_shape, F32),
        ],
        scratch_shapes=[
            pltpu.VMEM((N_GROUP_BLOCKS * tc * STATE_COLS, LANES), F32),
            pltpu.VMEM((tc, SSM_W), F32),
            pltpu.VMEM(state_shape, F32),
        ],
        compiler_params=_params(("arbitrary", "arbitrary")),
        name="ssm_chains%d" % chains,
    )(u, h0, wb, cb, lam, d, gw, gbias, og)


def _out_proj_kernel(x_ref, oa_ref, os_ref, w_ref, o_ref):
    acc = jnp.dot(oa_ref[...].astype(BF16), w_ref[0], preferred_element_type=F32)
    acc = acc + jnp.dot(os_ref[...].astype(BF16), w_ref[1], preferred_element_type=F32)
    o_ref[...] = x_ref[...] + acc


def _out_proj(x, o_att, o_ssm, w, tm):
    m = x.shape[0]
    return pl.pallas_call(
        _out_proj_kernel,
        grid=(m // tm,),
        in_specs=[
            pl.BlockSpec((tm, D_MODEL), lambda i: (i, 0)),
            pl.BlockSpec((tm, ATT_W), lambda i: (i, 0)),
            pl.BlockSpec((tm, SSM_W), lambda i: (i, 0)),
            pl.BlockSpec((2, ATT_W, D_MODEL), lambda i: (0, 0, 0)),
        ],
        out_specs=pl.BlockSpec((tm, D_MODEL), lambda i: (i, 0)),
        out_shape=jax.ShapeDtypeStruct((m, D_MODEL), F32),
        compiler_params=_params(("arbitrary",)),
        name="out_proj",
    )(x, o_att, o_ssm, w)


N_FFN_IN = 8


def _ffn_parts(i, f, h_ref, g_ref, prev_ref, wg_ref, wv_ref, cw_ref, cb_ref, wdn_ref, o_ref, cn_ref,
               fn_ref, acc_ref, halo_ref, *, tm, shift, tiles_per_seq, nf):
    def prologue():
        @pl.when(f == 0)
        def _():
            fn_ref[...] = _rms(h_ref[...], g_ref[...], NORM_EPS).astype(BF16)
            acc_ref[...] = jnp.zeros(acc_ref.shape, F32)

        @pl.when((i % tiles_per_seq) == 0)
        def _():
            halo_ref[f] = prev_ref[0]

    def main():
        fn = fn_ref[...]
        down = None
        for c0 in range(0, FFN_TF, MXU_WIDTH):
            cols = slice(c0, c0 + MXU_WIDTH)
            gate = jnp.dot(fn, wg_ref[:, cols], preferred_element_type=F32)
            val = jnp.dot(fn, wv_ref[:, cols], preferred_element_type=F32)
            gext = jnp.concatenate([halo_ref[f, :, cols], gate], axis=0)
            g1 = gext[CONV_HALO - shift:CONV_HALO - shift + tm]
            g2 = gext[CONV_HALO - 2 * shift:CONV_HALO - 2 * shift + tm]
            conv = cb_ref[:, cols] + (cw_ref[0:1, cols] * g2 + cw_ref[1:2, cols] * g1 + cw_ref[2:3, cols] * gate)
            hact = (conv * jax.nn.sigmoid(conv)) * val
            part = jnp.dot(hact.astype(BF16), wdn_ref[cols, :], preferred_element_type=F32)
            down = part if down is None else down + part
            halo_ref[f, :, cols] = gate[tm - CONV_HALO:tm]
            cn_ref[0, :, cols] = gate[tm - 2 * shift:tm]
        acc_ref[...] += down

    def epilogue():
        @pl.when(f == nf - 1)
        def _():
            o_ref[...] = h_ref[...] + acc_ref[...]

    return prologue, main, epilogue


def _ffn_kernel(*refs, **static):
    prologue, main, epilogue = _ffn_parts(pl.program_id(0), pl.program_id(1), *refs, **static)
    prologue()
    main()
    epilogue()


def _ffn_attn_kernel(pt_ref, *refs, n_pages_step, steps_per_seq, n_seq, n_new, lam_init, nf, **static):
    del pt_ref
    g = n_pages_step
    ffn_in = refs[:N_FFN_IN]
    lam_ref, qbd_ref = refs[N_FFN_IN:N_FFN_IN + 2]
    k_refs = refs[N_FFN_IN + 2:N_FFN_IN + 2 + g]
    v_refs = refs[N_FFN_IN + 2 + g:N_FFN_IN + 2 + 2 * g]
    rest = refs[N_FFN_IN + 2 + 2 * g:]
    bias_ref, kn_ref, vn_ref, bn_ref, sg_ref, o_ref, cn_ref, oa_ref = rest[:8]
    fn_ref, acc_ref, halo_ref, am_ref, al_ref, aacc_ref = rest[8:]
    i = pl.program_id(0)
    f = pl.program_id(1)
    s = i * nf + f
    active = s < n_seq * steps_per_seq
    prologue, ffn_main, epilogue = _ffn_parts(i, f, *ffn_in, o_ref, cn_ref, fn_ref, acc_ref, halo_ref,
                                              nf=nf, **static)
    init, attn_main, finish = _attn_sample_parts(
        s % steps_per_seq, steps_per_seq, lam_ref, qbd_ref, k_refs, v_refs, bias_ref, kn_ref, vn_ref,
        bn_ref, sg_ref, oa_ref, am_ref, al_ref, aacc_ref, n_new=n_new, lam_init=lam_init)
    prologue()

    @pl.when(active)
    def _():
        init()

    @pl.when(active)
    def _():
        ffn_main()
        attn_main()

    @pl.when(jnp.logical_not(active))
    def _():
        ffn_main()

    epilogue()

    @pl.when(active)
    def _():
        finish()


def _ffn(h, g, prev, wg, wv, cw, cb, wdn, tm, shift, tiles_per_seq, attn=None):
    m = h.shape[0]
    nf = D_FF_PAD // FFN_TF
    static = dict(tm=tm, shift=shift, tiles_per_seq=tiles_per_seq, nf=nf)
    in_specs = [
        pl.BlockSpec((tm, D_MODEL), lambda i, f, *_: (i, 0)),
        pl.BlockSpec((1, D_MODEL), lambda i, f, *_: (0, 0)),
        pl.BlockSpec((1, CONV_HALO, FFN_TF), lambda i, f, *_: (i // tiles_per_seq, 0, f)),
        pl.BlockSpec((D_MODEL, FFN_TF), lambda i, f, *_: (0, f)),
        pl.BlockSpec((D_MODEL, FFN_TF), lambda i, f, *_: (0, f)),
        pl.BlockSpec((CONV_W, FFN_TF), lambda i, f, *_: (0, f)),
        pl.BlockSpec((1, FFN_TF), lambda i, f, *_: (0, f)),
        pl.BlockSpec((FFN_TF, D_MODEL), lambda i, f, *_: (f, 0)),
    ]
    assert len(in_specs) == N_FFN_IN
    out_specs = [
        pl.BlockSpec((tm, D_MODEL), lambda i, f, *_: (i, 0)),
        pl.BlockSpec((1, 2 * shift, FFN_TF), lambda i, f, *_: (i, 0, f)),
    ]
    out_shape = [
        jax.ShapeDtypeStruct((m, D_MODEL), F32),
        jax.ShapeDtypeStruct((m // tm, 2 * shift, D_FF_PAD), F32),
    ]
    scratch = [
        pltpu.VMEM((tm, D_MODEL), BF16),
        pltpu.VMEM((tm, D_MODEL), F32),
        pltpu.VMEM((nf, CONV_HALO, FFN_TF), F32),
    ]
    args = (h, g, prev, wg, wv, cw, cb, wdn)
    grid = (m // tm, nf)
    if attn is None:
        return pl.pallas_call(
            functools.partial(_ffn_kernel, **static),
            grid=grid, in_specs=in_specs, out_specs=out_specs, out_shape=out_shape,
            scratch_shapes=scratch,
            compiler_params=_params(("arbitrary", "arbitrary")),
            name="conv_ffn",
        )(*args)

    (page_table, qbd, k_pool, v_pool, bias_last, k_new, v_new, bias_new, lam_rows, subln_g,
     n_pages_step, lam_init) = attn
    n_seq, n_pages = page_table.shape
    rows = qbd.shape[1]
    n_new = rows // (2 * N_HEADS)
    steps_per_seq = n_pages // n_pages_step
    last = n_seq * steps_per_seq - 1
    assert grid[0] * grid[1] > last

    def seq_of(i, f):
        return jnp.minimum(i * nf + f, last) // steps_per_seq

    def page_spec(p, shape):
        def index(i, f, pt):
            s = jnp.minimum(i * nf + f, last)
            return (pt[s // steps_per_seq, (s % steps_per_seq) * n_pages_step + p], 0, 0)
        return pl.BlockSpec((1,) + shape, index)

    in_specs += (
        [pl.BlockSpec((4, QK_DIM), lambda i, f, pt: (0, 0)),
         pl.BlockSpec((1, rows, ATT_W), lambda i, f, pt: (seq_of(i, f), 0, 0))]
        + [page_spec(p, (ATT_W, PAGE_SIZE)) for p in range(n_pages_step)]
        + [page_spec(p, (PAGE_SIZE * N_HEADS, V_HEAD)) for p in range(n_pages_step)]
        + [pl.BlockSpec((rows, PAGE_SIZE), lambda i, f, pt: (0, 0)),
           pl.BlockSpec((1, ATT_W, PAGE_SIZE), lambda i, f, pt: (seq_of(i, f), 0, 0)),
           pl.BlockSpec((1, PAGE_SIZE * N_HEADS, V_HEAD), lambda i, f, pt: (seq_of(i, f), 0, 0)),
           pl.BlockSpec((rows, PAGE_SIZE), lambda i, f, pt: (0, 0)),
           pl.BlockSpec((N_HEADS, V_HEAD), lambda i, f, pt: (0, 0))])
    out_specs.append(pl.BlockSpec((1, n_new, ATT_W), lambda i, f, pt: (seq_of(i, f), 0, 0)))
    out_shape.append(jax.ShapeDtypeStruct((n_seq, n_new, ATT_W), F32))
    scratch += [pltpu.VMEM((rows, LANES), F32), pltpu.VMEM((rows, LANES), F32),
                pltpu.VMEM((rows, V_HEAD), F32)]
    kern = functools.partial(_ffn_attn_kernel, n_pages_step=n_pages_step, steps_per_seq=steps_per_seq,
                             n_seq=n_seq, n_new=n_new, lam_init=lam_init, **static)
    return pl.pallas_call(
        kern,
        grid_spec=pltpu.PrefetchScalarGridSpec(
            num_scalar_prefetch=1, grid=grid, in_specs=in_specs, out_specs=out_specs,
            scratch_shapes=scratch),
        out_shape=out_shape,
        compiler_params=_params(("arbitrary", "arbitrary")),
        name="conv_ffn_attn",
    )(page_table, *args, lam_rows, qbd, *([k_pool] * n_pages_step), *([v_pool] * n_pages_step),
      bias_last, k_new, v_new, bias_new, subln_g)


def _ple_kernel(h_ref, gp_ref, pe_ref, wg_ref, wp_ref, gf_ref, o_ref):
    h = h_ref[...]
    hn = _rms(h, gp_ref[...], NORM_EPS).astype(BF16)
    gate = jax.nn.sigmoid(jnp.dot(hn, wg_ref[...], preferred_element_type=F32))
    proj = jnp.dot(pe_ref[...].astype(BF16), wp_ref[...], preferred_element_type=F32)
    o_ref[...] = _rms(h + gate * proj, gf_ref[...], NORM_EPS)


def _ple(h, gp, pe, wg, wp, gf, tm):
    m = h.shape[0]
    return pl.pallas_call(
        _ple_kernel,
        grid=(m // tm,),
        in_specs=[
            pl.BlockSpec((tm, D_MODEL), lambda i: (i, 0)),
            pl.BlockSpec((1, D_MODEL), lambda i: (0, 0)),
            pl.BlockSpec((tm, PLE_DIM), lambda i: (i, 0)),
            pl.BlockSpec((D_MODEL, D_MODEL), lambda i: (0, 0)),
            pl.BlockSpec((PLE_DIM, D_MODEL), lambda i: (0, 0)),
            pl.BlockSpec((1, D_MODEL), lambda i: (0, 0)),
        ],
        out_specs=pl.BlockSpec((tm, D_MODEL), lambda i: (i, 0)),
        out_shape=jax.ShapeDtypeStruct((m, D_MODEL), F32),
        compiler_params=_params(("arbitrary",)),
        name="ple_final",
    )(h, gp, pe, wg, wp, gf)


def _t5_bucket(dist):
    max_exact = N_BUCKETS // 2
    n = jnp.maximum(dist, 0)
    nf = jnp.maximum(n, 1).astype(F32)
    large = max_exact + (jnp.log(nf / max_exact) / math.log(MAX_DISTANCE / max_exact)
                         * (N_BUCKETS - max_exact)).astype(jnp.int32)
    large = jnp.minimum(large, N_BUCKETS - 1)
    return jnp.where(n < max_exact, n, large)


def _toeplitz(w, n):
    lead = w.shape[:-1]
    flat = jnp.tile(w, (1,) * len(lead) + (n,))[..., :n * (2 * n - 1)]
    return flat.reshape(lead + (n, 2 * n - 1))[..., :n]


def _prompt_bias_tiles(table, n):
    assert n >= MAX_DISTANCE
    far = table[N_BUCKETS - 1]
    bvec = (table[_t5_bucket(jnp.arange(2 * n, dtype=jnp.int32))] - far[None, :]).T
    bvec = bvec * LOG2E
    d = jnp.arange(2 * n, dtype=jnp.int32)
    d = jnp.where(d >= n, d - 2 * n, d)
    diag = jnp.where(d[None, :] <= 0, bvec[:, jnp.abs(d)], -jnp.inf)
    sub = bvec[:, jnp.clip(n - d, 0, 2 * n - 1)]
    return jnp.concatenate([_toeplitz(sub, n), _toeplitz(diag, n)], axis=-1)


def _sample_bias(table, n_new):
    assert PAGE_SIZE >= MAX_DISTANCE
    far = table[N_BUCKETS - 1]
    t = jnp.arange(n_new, dtype=jnp.int32)
    col = jnp.arange(PAGE_SIZE, dtype=jnp.int32)

    def rows(b):
        b = jnp.transpose(b * LOG2E, (2, 0, 1))
        return jnp.broadcast_to(b[:, None], (N_HEADS, 2, n_new, PAGE_SIZE)).reshape(-1, PAGE_SIZE)

    dist_last = PAGE_SIZE + t[:, None] - col[None, :]
    dist_new = t[:, None] - col[None, :]
    b_new = jnp.where((dist_new >= 0)[..., None], table[_t5_bucket(dist_new)] - far, -jnp.inf)
    return rows(table[_t5_bucket(dist_last)] - far), rows(b_new)


def _block_diag(w):
    nb, gp, a, b = w.shape
    eye = jnp.eye(gp, dtype=w.dtype)
    out = w[:, :, :, None, :] * eye[None, :, None, :, None]
    return out.reshape(nb, gp * a, gp * b)


def _ssm_tables(a_re, a_im, b_re, b_im, c_re, c_im, log_dt):
    g, n, gc = N_SSM_GROUPS, SSM_STATE, SSM_GROUP
    dt = jnp.exp(log_dt)[:, None]
    decay = jnp.exp(a_re * dt)
    ab_re = decay * jnp.cos(a_im * dt)
    ab_im = decay * jnp.sin(a_im * dt)
    den = a_re * a_re + a_im * a_im
    num_re = ab_re - 1.0
    coef_re = (num_re * a_re + ab_im * a_im) / den
    coef_im = (ab_im * a_re - num_re * a_im) / den
    bb_re = coef_re[..., None] * b_re - coef_im[..., None] * b_im
    bb_im = coef_re[..., None] * b_im + coef_im[..., None] * b_re
    blk = lambda w: w.reshape((N_GROUP_BLOCKS, GROUPS_PER_BLOCK) + w.shape[1:])
    to_cn = lambda w: jnp.swapaxes(blk(w), 2, 3)
    wb = jnp.concatenate([_block_diag(to_cn(bb_re)), _block_diag(to_cn(bb_im))], axis=2)
    to_nc = lambda w: jnp.swapaxes(blk(w), 2, 3)
    cb = jnp.concatenate([_block_diag(to_nc(c_re)), _block_diag(to_nc(-c_im))], axis=1)
    lr = ab_re.reshape(N_GROUP_BLOCKS, STATE_COLS // 2, LANES)
    li = ab_im.reshape(N_GROUP_BLOCKS, STATE_COLS // 2, LANES)
    lam = jnp.stack([jnp.concatenate([lr, lr], axis=1), jnp.concatenate([-li, li], axis=1)], axis=1)
    return wb.astype(BF16), cb.astype(BF16), lam.astype(F32)


def _cast_in_kernel(w_ref, wb_ref, wkt_ref):
    w = w_ref[...]
    wb_ref[...] = w.astype(BF16)
    wkt_ref[...] = w[:, ATT_W:2 * ATT_W].T.astype(BF16)


def _cast_w_in(w, tr=256):
    return pl.pallas_call(
        _cast_in_kernel,
        grid=(D_MODEL // tr,),
        in_specs=[pl.BlockSpec((tr, 4 * ATT_W), lambda i: (i, 0))],
        out_specs=[pl.BlockSpec((tr, 4 * ATT_W), lambda i: (i, 0)),
                   pl.BlockSpec((ATT_W, tr), lambda i: (0, i))],
        out_shape=[jax.ShapeDtypeStruct((D_MODEL, 4 * ATT_W), BF16),
                   jax.ShapeDtypeStruct((ATT_W, D_MODEL), BF16)],
        compiler_params=_params(("arbitrary",)),
        name="cast_w_in",
    )(w)


def _cast_up_kernel(w_ref, wg_ref, wv_ref):
    zeros = jnp.zeros((wg_ref.shape[0], D_FF_PAD - D_FF), BF16)
    for n, o_ref in enumerate((wg_ref, wv_ref)):
        o_ref[:, 0:D_FF] = w_ref[:, n * D_FF:(n + 1) * D_FF].astype(BF16)
        o_ref[:, D_FF:D_FF_PAD] = zeros


def _cast_w_up(w, tr=128):
    out = jax.ShapeDtypeStruct((D_MODEL, D_FF_PAD), BF16)
    spec = pl.BlockSpec((tr, D_FF_PAD), lambda i: (i, 0))
    return pl.pallas_call(
        _cast_up_kernel,
        grid=(D_MODEL // tr,),
        in_specs=[pl.BlockSpec((tr, 2 * D_FF), lambda i: (i, 0))],
        out_specs=[spec, spec],
        out_shape=[out, out],
        compiler_params=_params(("arbitrary",)),
        name="cast_w_up",
    )(w)


def _cast_down_kernel(w_ref, o_ref):
    o_ref[0:D_FF, :] = w_ref[...].astype(BF16)
    o_ref[D_FF:D_FF_PAD, :] = jnp.zeros((D_FF_PAD - D_FF, o_ref.shape[1]), BF16)


def _cast_w_down(w, tc=MXU_WIDTH):
    return pl.pallas_call(
        _cast_down_kernel,
        grid=(D_MODEL // tc,),
        in_specs=[pl.BlockSpec((D_FF, tc), lambda i: (0, i))],
        out_specs=pl.BlockSpec((D_FF_PAD, tc), lambda i: (0, i)),
        out_shape=jax.ShapeDtypeStruct((D_FF_PAD, D_MODEL), BF16),
        compiler_params=_params(("arbitrary",)),
        name="cast_w_down",
    )(w)


def _pad_ff(w, axis):
    pad = [(0, 0)] * w.ndim
    pad[axis] = (0, D_FF_PAD - D_FF)
    return jnp.pad(w, pad)


def _state_to_tiles(re, im):
    shape = (re.shape[0], N_GROUP_BLOCKS, STATE_COLS // 2, LANES)
    return jnp.concatenate([re.reshape(shape), im.reshape(shape)], axis=2)


def _tiles_to_state(tiles):
    half = STATE_COLS // 2
    shape = (1, tiles.shape[0], N_SSM_GROUPS, SSM_STATE)
    return tiles[:, :, :half].reshape(shape), tiles[:, :, half:].reshape(shape)


def kernel(x_prompt, x_sample, cache_k, cache_v, state_ssm_re, state_ssm_im, state_conv, page_table, p_prompt, p_sample, rel_bias, norm_mix_g, w_in, lambda_q1, lambda_k1, lambda_q2, lambda_k2, attn_subln_g, ssm_a_re, ssm_a_im, ssm_b_re, ssm_b_im, ssm_c_re, ssm_c_im, ssm_d, ssm_log_dt, ssm_glu_w, ssm_glu_b, ssm_out_g, w_out, norm_ffn_g, ffn_w_up, ffn_conv_w, ffn_conv_b, ffn_w_down, norm_ple_g, ple_w_gate, ple_w_proj, norm_final_g):
    assert DEPTH == 1
    li = 0
    lam_init = 0.8 - 0.6 * math.exp(-0.3 * li)
    nb, t, _ = x_prompt.shape
    db, ds, _ = x_sample.shape
    assert db == SUBLANES
    m_p = nb * t
    m_s = db * ds

    row = lambda v: v.reshape(1, -1).astype(F32)
    w_in_b, w_kt_b = _cast_w_in(w_in[li])
    w_out_b = w_out[li].astype(BF16).reshape(2, ATT_W, D_MODEL)
    w_ug_b, w_uv_b = _cast_w_up(ffn_w_up[li])
    w_dn_b = _cast_w_down(ffn_w_down[li])
    conv_w = _pad_ff(ffn_conv_w[li], 1)
    conv_b = _pad_ff(ffn_conv_b[li].reshape(1, D_FF), 1)
    w_gate_b = ple_w_gate[li].astype(BF16)
    w_proj_b = ple_w_proj[li].astype(BF16)
    lam_rows = jnp.stack([lambda_q1[li], lambda_k1[li], lambda_q2[li], lambda_k2[li]]).astype(F32)
    subln_g = attn_subln_g[li].astype(F32)
    wb, cb, tbl = _ssm_tables(ssm_a_re[li], ssm_a_im[li], ssm_b_re[li], ssm_b_im[li],
                              ssm_c_re[li], ssm_c_im[li], ssm_log_dt[li])
    glu_w = _block_diag(ssm_glu_w[li].reshape(N_GROUP_BLOCKS, GROUPS_PER_BLOCK, SSM_GROUP, SSM_GROUP)).astype(BF16)
    ssm_args = (wb, cb, tbl, row(ssm_d[li]), glu_w, row(ssm_glu_b[li]), row(ssm_out_g[li]))

    tq = 512
    xp = x_prompt.reshape(m_p, D_MODEL)
    q_p, kt_p, v_p, u_p = _in_proj(xp, row(norm_mix_g[li]), w_in_b, w_kt_b, tm=512,
                                   rows_per_seq=t, k_transposed=True)
    shp = lambda a: a.reshape(nb, t, -1)
    band = _prompt_bias_tiles(rel_bias.astype(F32), MAX_DISTANCE)
    o_att_p = _attn_prompt(shp(q_p), kt_p, shp(v_p), band, lam_rows,
                           subln_g.reshape(N_HEADS, 1, V_HEAD), tq, 2, lam_init)
    zero_state = jnp.zeros((nb, N_GROUP_BLOCKS, 1, STATE_COLS, LANES), F32)
    o_ssm_p, hf_p = _ssm(shp(u_p), zero_state, *ssm_args, tc=256, chains=1, out_dtype=BF16)
    h1_p = _out_proj(xp, o_att_p.reshape(m_p, ATT_W), o_ssm_p.reshape(m_p, SSM_W), w_out_b, tm=512)

    tmaj = lambda a: jnp.swapaxes(a, 0, 1).reshape(m_s, -1)
    bmaj = lambda a: jnp.swapaxes(a.reshape(ds, db, -1), 0, 1)
    xs = tmaj(x_sample)
    q_s, k_s, v_s, u_s = _in_proj(xs, row(norm_mix_g[li]), w_in_b, w_kt_b, tm=m_s,
                                  rows_per_seq=m_s, k_transposed=False)
    q_b = bmaj(q_s)
    q_rep = jnp.tile(q_b, (1, 2 * N_HEADS, 1)).reshape(db, 2 * N_HEADS, ds, ATT_W)
    sel = (jnp.arange(ATT_W)[None, :] // QK_DIM) == jnp.arange(2 * N_HEADS)[:, None]
    qbd = jnp.where(sel[None, :, None, :], q_rep, jnp.zeros_like(q_rep)).reshape(db, 2 * N_HEADS * ds, ATT_W)
    pad_rows = lambda a: jnp.pad(bmaj(a), ((0, 0), (0, PAGE_SIZE - ds), (0, 0)))
    bias_past, bias_new = _sample_bias(rel_bias.astype(F32), ds)
    n_pool = cache_k.shape[1]
    k_pool_t = jnp.transpose(cache_k[li], (0, 2, 3, 4, 1)).reshape(n_pool, ATT_W, PAGE_SIZE)
    k_new_t = jnp.swapaxes(pad_rows(k_s), 1, 2)
    v_pool_r = cache_v[li].reshape(n_pool, PAGE_SIZE * N_HEADS, V_HEAD)
    v_new_r = pad_rows(v_s).reshape(db, PAGE_SIZE * N_HEADS, V_HEAD)
    tm_ffn = 512
    zero_prev = jnp.zeros((nb, CONV_HALO, D_FF_PAD), F32)
    attn_s = (page_table, qbd, k_pool_t, v_pool_r, bias_past, k_new_t, v_new_r, bias_new,
              lam_rows, subln_g, 8, lam_init)
    h2_p, conv_p, o_att_s = _ffn(h1_p, row(norm_ffn_g[li]), zero_prev, w_ug_b, w_uv_b, conv_w, conv_b,
                                 w_dn_b, tm=tm_ffn, shift=1, tiles_per_seq=t // tm_ffn, attn=attn_s)
    y_p = _ple(h2_p, row(norm_ple_g[li]), p_prompt[li].reshape(m_p, PLE_DIM), w_gate_b, w_proj_b,
               row(norm_final_g), tm=512)
    h0_s = jnp.swapaxes(_state_to_tiles(state_ssm_re[li], state_ssm_im[li]), 0, 1)[None]
    o_ssm_s, hf_s = _ssm(u_s[None], h0_s, *ssm_args, tc=m_s, chains=db, out_dtype=F32)
    h1_s = _out_proj(xs, tmaj(o_att_s), o_ssm_s[0], w_out_b, tm=m_s)
    prev_s = _pad_ff(jnp.swapaxes(state_conv[li], 0, 1).reshape(1, 2 * db, D_FF), 2)
    h2_s, conv_s = _ffn(h1_s, row(norm_ffn_g[li]), prev_s, w_ug_b, w_uv_b, conv_w, conv_b, w_dn_b,
                        tm=m_s, shift=db, tiles_per_seq=1)
    y_s = _ple(h2_s, row(norm_ple_g[li]), tmaj(p_sample[li]), w_gate_b, w_proj_b,
               row(norm_final_g), tm=m_s)

    re_p, im_p = _tiles_to_state(hf_p[:, :, 0])
    re_s, im_s = _tiles_to_state(jnp.swapaxes(hf_s[0], 0, 1))
    tiles = t // tm_ffn
    conv_p = conv_p[tiles - 1::tiles][None, :, :, :D_FF]
    conv_s = jnp.swapaxes(conv_s.reshape(2, db, D_FF_PAD), 0, 1)[None, :, :, :D_FF]
    return (y_p.reshape(nb, t, D_MODEL),
            bmaj(y_s),
            jnp.transpose(kt_p.reshape(1, nb, N_HEADS, 2, QK_DIM, t), (0, 1, 5, 2, 3, 4)),
            v_p.reshape(1, nb, t, N_HEADS, V_HEAD),
            re_p, im_p, conv_p,
            bmaj(k_s).reshape(1, db, ds, N_HEADS, 2, QK_DIM),
            bmaj(v_s).reshape(1, db, ds, N_HEADS, V_HEAD),
            re_s, im_s, conv_s)
```

```python
import functools
import math

import jax
import jax.numpy as jnp
from jax import lax
from jax.experimental import pallas as pl
from jax.experimental.pallas import tpu as pltpu

F32 = jnp.float32
BF16 = jnp.bfloat16

D_MODEL = 2048
DEPTH = 1
PAGE_SIZE = 128
ATT_W = 1024
SSM_W = 1024
V_HEAD = 128
N_HEADS = 8
QK_DIM = 64
SSM_GROUP = 16
N_SSM_GROUPS = 64
SSM_STATE = 64
D_FF = 5504
CONV_W = 3
PLE_DIM = 256
N_BUCKETS = 32
MAX_DISTANCE = 128
NORM_EPS = 1e-6
SUBLN_EPS = 1e-5
LOG2E = math.log2(math.e)
QK_SCALE = QK_DIM ** -0.5 * LOG2E

LANES = 128
SUBLANES = 8
MXU_WIDTH = 256
VMEM_LIMIT = 56 * 1024 * 1024

GROUPS_PER_BLOCK = LANES // SSM_GROUP
N_GROUP_BLOCKS = N_SSM_GROUPS // GROUPS_PER_BLOCK
STATE_LANES = GROUPS_PER_BLOCK * SSM_STATE

D_FF_PAD = 5632
FFN_TF = 512
CONV_HALO = 16


def _params(sem):
    return pltpu.CompilerParams(dimension_semantics=sem, vmem_limit_bytes=VMEM_LIMIT)


def _rms(x, g, eps):
    return x * lax.rsqrt(jnp.mean(x * x, axis=-1, keepdims=True) + eps) * g


def _in_proj_kernel(x_ref, g_ref, w_ref, wkt_ref, q_ref, k_ref, v_ref, u_ref, *, k_transposed):
    xn = _rms(x_ref[...], g_ref[...], NORM_EPS).astype(BF16)

    def proj(n):
        return jnp.dot(xn, w_ref[:, n * ATT_W:(n + 1) * ATT_W], preferred_element_type=F32)

    q_ref[...] = (proj(0) * QK_SCALE).astype(BF16)
    if k_transposed:
        k_ref[0] = lax.dot_general(wkt_ref[...], xn, (((1,), (1,)), ((), ())),
                                   preferred_element_type=F32)
    else:
        k_ref[...] = proj(1)
    v_ref[...] = proj(2)
    u_ref[...] = proj(3)


def _resident(shape):
    return pl.BlockSpec(shape, lambda *_: (0,) * len(shape), pipeline_mode=pl.Buffered(1))


def _in_proj(x, g, w, wkt, tm, rows_per_seq, k_transposed):
    m = x.shape[0]
    out_spec = pl.BlockSpec((tm, ATT_W), lambda i: (i, 0))
    if k_transposed:
        tiles = rows_per_seq // tm
        k_spec = pl.BlockSpec((1, ATT_W, tm), lambda i: (i // tiles, 0, i % tiles))
        k_shape = jax.ShapeDtypeStruct((m // rows_per_seq, ATT_W, rows_per_seq), F32)
    else:
        k_spec = out_spec
        k_shape = jax.ShapeDtypeStruct((m, ATT_W), F32)
    return pl.pallas_call(
        functools.partial(_in_proj_kernel, k_transposed=k_transposed),
        grid=(m // tm,),
        in_specs=[
            pl.BlockSpec((tm, D_MODEL), lambda i: (i, 0)),
            _resident((1, D_MODEL)),
            _resident((D_MODEL, 4 * ATT_W)),
            _resident((ATT_W, D_MODEL)),
        ],
        out_specs=[out_spec, k_spec, out_spec, out_spec],
        out_shape=[
            jax.ShapeDtypeStruct((m, ATT_W), BF16),
            k_shape,
            jax.ShapeDtypeStruct((m, ATT_W), F32),
            jax.ShapeDtypeStruct((m, SSM_W), F32),
        ],
        compiler_params=_params(("arbitrary",)),
        name="in_proj",
    )(x, g, w, wkt)


def _lambda(lam_ref, lam_init):
    s1 = jnp.sum(lam_ref[0:1, :] * lam_ref[1:2, :], axis=-1, keepdims=True)
    s2 = jnp.sum(lam_ref[2:3, :] * lam_ref[3:4, :], axis=-1, keepdims=True)
    return jnp.exp(s1) - jnp.exp(s2) + lam_init


def _online_softmax_step(s, v_blk, m_ref, l_ref, acc_ref):
    m_prev = m_ref[...]
    m_new = jnp.maximum(m_prev, jnp.max(s, axis=-1, keepdims=True))
    alpha = jnp.exp2(m_prev - m_new)
    p = jnp.exp2(s - jnp.concatenate([m_new] * (s.shape[1] // LANES), axis=1))
    l_ref[...] = alpha * l_ref[...] + jnp.sum(p, axis=-1, keepdims=True)
    acc_ref[...] = alpha * acc_ref[...] + jnp.dot(
        p.astype(BF16), v_blk, preferred_element_type=F32)
    m_ref[...] = m_new


def _attn_prompt_kernel(lam_ref, q_ref, k_ref, v_ref, band_ref, g_ref, o_ref,
                        kb_ref, vb_ref, qs_ref, bias_ref, m_ref, l_ref, acc_ref, *, tq, nh, lam_init):
    i = pl.program_id(2)
    head_cols = [slice(h * V_HEAD, (h + 1) * V_HEAD) for h in range(nh)]

    @pl.when(i == 0)
    def _():
        kb_ref[...] = k_ref[0].astype(BF16)
        vb_ref[...] = v_ref[0].astype(BF16)
        w = band_ref.shape[1]
        zero = jnp.zeros((w, w), F32)
        masked = jnp.full((w, w), -jnp.inf, F32)
        n = tq // w
        for h in range(nh):
            left, right = band_ref[h, :, 0:w], band_ref[h, :, w:2 * w]
            for r in range(n):
                for c in range(n):
                    rows = slice(r * w, (r + 1) * w)
                    bias_ref[h, rows, c * w:(c + 1) * w] = left if (r == 0 and c == n - 1) else zero
                    bias_ref[h, rows, tq + c * w:tq + (c + 1) * w] = (
                        right if c == r else left if c == r - 1 else masked if c > r else zero)

    for h in range(nh):
        q = q_ref[0, :, head_cols[h]]
        lane = lax.broadcasted_iota(jnp.int32, q.shape, 1)
        zero = jnp.zeros_like(q)
        qs_ref[h, 0:tq, :] = jnp.where(lane < QK_DIM, q, zero)
        qs_ref[h, tq:2 * tq, :] = jnp.where(lane >= QK_DIM, q, zero)
    m_ref[...] = jnp.full(m_ref.shape, -jnp.inf, F32)
    l_ref[...] = jnp.zeros(l_ref.shape, F32)
    acc_ref[...] = jnp.zeros(acc_ref.shape, F32)

    def update(j, width, biased):
        off = pl.multiple_of(j * tq, tq)
        for h in range(nh):
            rows = slice(h * 2 * QK_DIM, (h + 1) * 2 * QK_DIM)
            s = jnp.dot(qs_ref[h], kb_ref[rows, pl.ds(off, width)], preferred_element_type=F32)
            if biased:
                bias = bias_ref[h, :, 2 * tq - width:2 * tq]
                s = (s.reshape(2, tq, width) + bias[None]).reshape(2 * tq, width)
            _online_softmax_step(s, vb_ref[pl.ds(off, width), head_cols[h]],
                                 m_ref.at[h], l_ref.at[h], acc_ref.at[h])

    n_far = jnp.maximum(i - 1, 0)

    def far_body(t, carry):
        update(2 * t, 2 * tq, False)
        return carry

    lax.fori_loop(0, n_far // 2, far_body, 0)

    @pl.when(n_far % 2 == 1)
    def _():
        update(n_far - 1, tq, False)

    @pl.when(i >= 1)
    def _():
        update(i - 1, 2 * tq, True)

    @pl.when(i == 0)
    def _():
        update(0, tq, True)

    lam = _lambda(lam_ref, lam_init)
    for h in range(nh):
        inv_l = 1.0 / l_ref[h]
        o = acc_ref[h, 0:tq, :] * inv_l[0:tq] - lam * (acc_ref[h, tq:2 * tq, :] * inv_l[tq:2 * tq])
        o_ref[0, :, head_cols[h]] = (_rms(o, g_ref[h], SUBLN_EPS) * (1.0 - lam_init)).astype(o_ref.dtype)


def _attn_prompt(q, k, v, band, lam_rows, subln_g, tq, nh, lam_init):
    b, t, _ = q.shape
    w = band.shape[1]
    assert tq % w == 0 and band.shape[2] == 2 * w and N_HEADS % nh == 0
    kern = functools.partial(_attn_prompt_kernel, tq=tq, nh=nh, lam_init=lam_init)
    gw = nh * V_HEAD
    return pl.pallas_call(
        kern,
        grid=(b, N_HEADS // nh, t // tq),
        in_specs=[
            pl.BlockSpec((4, QK_DIM), lambda bi, h, i: (0, 0)),
            pl.BlockSpec((1, tq, gw), lambda bi, h, i: (bi, i, h)),
            pl.BlockSpec((1, gw, t), lambda bi, h, i: (bi, h, 0)),
            pl.BlockSpec((1, t, gw), lambda bi, h, i: (bi, 0, h)),
            pl.BlockSpec((nh, w, 2 * w), lambda bi, h, i: (h, 0, 0)),
            pl.BlockSpec((nh, 1, V_HEAD), lambda bi, h, i: (h, 0, 0)),
        ],
        out_specs=pl.BlockSpec((1, tq, gw), lambda bi, h, i: (bi, i, h)),
        out_shape=jax.ShapeDtypeStruct((b, t, ATT_W), BF16),
        scratch_shapes=[
            pltpu.VMEM((gw, t), BF16),
            pltpu.VMEM((t, gw), BF16),
            pltpu.VMEM((nh, 2 * tq, V_HEAD), BF16),
            pltpu.VMEM((nh, tq, 2 * tq), F32),
            pltpu.VMEM((nh, 2 * tq, LANES), F32),
            pltpu.VMEM((nh, 2 * tq, LANES), F32),
            pltpu.VMEM((nh, 2 * tq, V_HEAD), F32),
        ],
        compiler_params=_params(("arbitrary", "arbitrary", "arbitrary")),
        name="attn_prompt",
    )(lam_rows, q, k, v, band, subln_g)


def _attn_sample_parts(step, n_steps, lam_ref, qbd_ref, k_refs, v_refs, bias_ref, kn_ref, vn_ref, bn_ref,
                       g_ref, o_ref, m_ref, l_ref, acc_ref, *, n_new, lam_init):
    def init():
        @pl.when(step == 0)
        def _():
            m_ref[...] = jnp.full(m_ref.shape, -jnp.inf, F32)
            l_ref[...] = jnp.zeros(l_ref.shape, F32)
            acc_ref[...] = jnp.zeros(acc_ref.shape, F32)

    qbd = qbd_ref[0]
    rows = 2 * n_new

    def update(k_pages, v_page_refs, bias_last):
        parts = [jnp.dot(qbd, kp.astype(BF16), preferred_element_type=F32) for kp in k_pages]
        parts[-1] = parts[-1] + bias_last
        s = jnp.concatenate(parts, axis=1)
        m_prev = m_ref[...]
        m_new = jnp.maximum(m_prev, jnp.max(s, axis=-1, keepdims=True))
        alpha = jnp.exp2(m_prev - m_new)
        p = jnp.exp2(s - jnp.concatenate([m_new] * len(parts), axis=1))
        l_ref[...] = alpha * l_ref[...] + jnp.sum(p, axis=-1, keepdims=True)
        m_ref[...] = m_new
        pb = p.astype(BF16)
        for h in range(N_HEADS):
            v_h = jnp.concatenate(
                [vr[0, pl.ds(h, PAGE_SIZE, stride=N_HEADS), :].astype(BF16) for vr in v_page_refs], axis=0)
            r = slice(h * rows, (h + 1) * rows)
            acc_ref[r, :] = alpha[r] * acc_ref[r, :] + jnp.dot(pb[r], v_h, preferred_element_type=F32)

    is_last = step == n_steps - 1

    def main():
        bias_last = jnp.where(is_last, bias_ref[...], jnp.zeros(bias_ref.shape, F32))
        update([kr[0] for kr in k_refs], v_refs, bias_last)

    def finish():
        @pl.when(is_last)
        def _():
            update([kn_ref[0]], [vn_ref], bn_ref[...])
            lam = _lambda(lam_ref, lam_init)
            acc = acc_ref[...] * (1.0 / l_ref[...])
            for h in range(N_HEADS):
                blk = acc[h * rows:(h + 1) * rows]
                o = blk[0:n_new] - lam * blk[n_new:rows]
                o = _rms(o, g_ref[h:h + 1, :], SUBLN_EPS) * (1.0 - lam_init)
                o_ref[0, :, h * V_HEAD:(h + 1) * V_HEAD] = o

    return init, main, finish


STATE_COLS = 2 * STATE_LANES // LANES


def _state_rows(g, tc, col):
    return pl.ds(g * tc * STATE_COLS + col, tc, stride=STATE_COLS)


def _ssm_kernel(u_ref, h0_ref, wb_ref, cb_ref, lam_ref, d_ref, gw_ref, gbias_ref, og_ref,
                o_ref, hf_ref, hb_ref, gbuf_ref, carry_ref, *, tc, chains):
    c = pl.program_id(1)

    @pl.when(c == 0)
    def _():
        carry_ref[...] = h0_ref[0]

    for g in range(N_GROUP_BLOCKS):
        u = u_ref[0, :, g * LANES:(g + 1) * LANES]
        bu = jnp.dot(u.astype(BF16), wb_ref[g], preferred_element_type=F32)
        for k in range(STATE_COLS):
            hb_ref[_state_rows(g, tc, k), :] = bu[:, k * LANES:(k + 1) * LANES]

    coef = [(lam_ref[g, 0], lam_ref[g, 1]) for g in range(N_GROUP_BLOCKS)]

    def step(g, row, h):
        start = (g * tc + row) * STATE_COLS
        if not isinstance(start, int):
            start = pl.multiple_of(start, STATE_COLS)
        rows = pl.ds(start, STATE_COLS)
        a, b = coef[g]
        h = a * h + b * pltpu.roll(h, STATE_COLS // 2, 0) + hb_ref[rows, :]
        hb_ref[rows, :] = h
        return h

    if chains == 1:
        def body(t, hs):
            return tuple(step(g, t, hs[g]) for g in range(N_GROUP_BLOCKS))

        hs = lax.fori_loop(0, tc, body, tuple(carry_ref[g, 0] for g in range(N_GROUP_BLOCKS)),
                           unroll=SUBLANES)
        for g in range(N_GROUP_BLOCKS):
            carry_ref[g, 0] = hs[g]
    else:
        for g in range(N_GROUP_BLOCKS):
            hs = [carry_ref[g, b] for b in range(chains)]
            for row in range(tc):
                hs[row % chains] = step(g, row, hs[row % chains])
            for b in range(chains):
                carry_ref[g, b] = hs[b]

    ssq = jnp.zeros((tc, 1), F32)
    for g in range(N_GROUP_BLOCKS):
        cols = slice(g * LANES, (g + 1) * LANES)
        h = jnp.concatenate([hb_ref[_state_rows(g, tc, k), :] for k in range(STATE_COLS)], axis=1)
        y = jnp.dot(h.astype(BF16), cb_ref[g], preferred_element_type=F32)
        y = y + d_ref[:, cols] * u_ref[0, :, cols]
        a = jax.nn.gelu(y)
        gate = jnp.dot(a.astype(BF16), gw_ref[g], preferred_element_type=F32) + gbias_ref[:, cols]
        a = a * jax.nn.sigmoid(gate)
        gbuf_ref[:, cols] = a
        ssq = ssq + jnp.sum(a * a, axis=-1, keepdims=True)
    inv = lax.rsqrt(ssq * (1.0 / SSM_W) + NORM_EPS)
    o_ref[0] = (gbuf_ref[...] * inv * og_ref[...]).astype(o_ref.dtype)
    hf_ref[0] = carry_ref[...]


def _ssm(u, h0, wb, cb, lam, d, gw, gbias, og, tc, chains, out_dtype):
    ns, t, _ = u.shape
    kern = functools.partial(_ssm_kernel, tc=tc, chains=chains)
    full = lambda *shape: pl.BlockSpec(shape, lambda s, c: (0,) * len(shape))
    state_shape = (N_GROUP_BLOCKS, chains, STATE_COLS, LANES)
    state_spec = pl.BlockSpec((1,) + state_shape, lambda s, c: (s, 0, 0, 0, 0))
    return pl.pallas_call(
        kern,
        grid=(ns, t // tc),
        in_specs=[
            pl.BlockSpec((1, tc, SSM_W), lambda s, c: (s, c, 0)),
            state_spec,
            full(N_GROUP_BLOCKS, LANES, 2 * STATE_LANES),
            full(N_GROUP_BLOCKS, 2 * STATE_LANES, LANES),
            full(N_GROUP_BLOCKS, 2, STATE_COLS, LANES),
            full(1, SSM_W),
            full(N_GROUP_BLOCKS, LANES, LANES),
            full(1, SSM_W),
            full(1, SSM_W),
        ],
        out_specs=[pl.BlockSpec((1, tc, SSM_W), lambda s, c: (s, c, 0)), state_spec],
        out_shape=[
            jax.ShapeDtypeStruct((ns, t, SSM_W), out_dtype),
            jax.ShapeDtypeStruct((ns,) + state_shape, F32),
        ],
        scratch_shapes=[
            pltpu.VMEM((N_GROUP_BLOCKS * tc * STATE_COLS, LANES), F32),
            pltpu.VMEM((tc, SSM_W), F32),
            pltpu.VMEM(state_shape, F32),
        ],
        compiler_params=_params(("arbitrary", "arbitrary")),
        name="ssm_chains%d" % chains,
    )(u, h0, wb, cb, lam, d, gw, gbias, og)


def _out_proj_kernel(x_ref, oa_ref, os_ref, w_ref, o_ref):
    acc = jnp.dot(oa_ref[...].astype(BF16), w_ref[0], preferred_element_type=F32)
    acc = acc + jnp.dot(os_ref[...].astype(BF16), w_ref[1], preferred_element_type=F32)
    o_ref[...] = x_ref[...] + acc


def _out_proj(x, o_att, o_ssm, w, tm):
    m = x.shape[0]
    return pl.pallas_call(
        _out_proj_kernel,
        grid=(m // tm,),
        in_specs=[
            pl.BlockSpec((tm, D_MODEL), lambda i: (i, 0)),
            pl.BlockSpec((tm, ATT_W), lambda i: (i, 0)),
            pl.BlockSpec((tm, SSM_W), lambda i: (i, 0)),
            pl.BlockSpec((2, ATT_W, D_MODEL), lambda i: (0, 0, 0)),
        ],
        out_specs=pl.BlockSpec((tm, D_MODEL), lambda i: (i, 0)),
        out_shape=jax.ShapeDtypeStruct((m, D_MODEL), F32),
        compiler_params=_params(("arbitrary",)),
        name="out_proj",
    )(x, o_att, o_ssm, w)


N_FFN_IN = 8


def _ffn_parts(i, f, h_ref, g_ref, prev_ref, wg_ref, wv_ref, cw_ref, cb_ref, wdn_ref, o_ref, cn_ref,
               fn_ref, acc_ref, halo_ref, *, tm, shift, tiles_per_seq, nf):
    def prologue():
        @pl.when(f == 0)
        def _():
            fn_ref[...] = _rms(h_ref[...], g_ref[...], NORM_EPS).astype(BF16)
            acc_ref[...] = jnp.zeros(acc_ref.shape, F32)

        @pl.when((i % tiles_per_seq) == 0)
        def _():
            halo_ref[f] = prev_ref[0]

    def main():
        fn = fn_ref[...]
        down = None
        for c0 in range(0, FFN_TF, MXU_WIDTH):
            cols = slice(c0, c0 + MXU_WIDTH)
            gate = jnp.dot(fn, wg_ref[:, cols], preferred_element_type=F32)
            val = jnp.dot(fn, wv_ref[:, cols], preferred_element_type=F32)
            gext = jnp.concatenate([halo_ref[f, :, cols], gate], axis=0)
            g1 = gext[CONV_HALO - shift:CONV_HALO - shift + tm]
            g2 = gext[CONV_HALO - 2 * shift:CONV_HALO - 2 * shift + tm]
            conv = cb_ref[:, cols] + (cw_ref[0:1, cols] * g2 + cw_ref[1:2, cols] * g1 + cw_ref[2:3, cols] * gate)
            hact = (conv * jax.nn.sigmoid(conv)) * val
            part = jnp.dot(hact.astype(BF16), wdn_ref[cols, :], preferred_element_type=F32)
            down = part if down is None else down + part
            halo_ref[f, :, cols] = gate[tm - CONV_HALO:tm]
            cn_ref[0, :, cols] = gate[tm - 2 * shift:tm]
        acc_ref[...] += down

    def epilogue():
        @pl.when(f == nf - 1)
        def _():
            o_ref[...] = h_ref[...] + acc_ref[...]

    return prologue, main, epilogue


def _ffn_kernel(*refs, **static):
    prologue, main, epilogue = _ffn_parts(pl.program_id(0), pl.program_id(1), *refs, **static)
    prologue()
    main()
    epilogue()


def _ffn_attn_kernel(pt_ref, *refs, n_pages_step, steps_per_seq, n_seq, n_new, lam_init, nf, **static):
    del pt_ref
    g = n_pages_step
    ffn_in = refs[:N_FFN_IN]
    lam_ref, qbd_ref = refs[N_FFN_IN:N_FFN_IN + 2]
    k_refs = refs[N_FFN_IN + 2:N_FFN_IN + 2 + g]
    v_refs = refs[N_FFN_IN + 2 + g:N_FFN_IN + 2 + 2 * g]
    rest = refs[N_FFN_IN + 2 + 2 * g:]
    bias_ref, kn_ref, vn_ref, bn_ref, sg_ref, o_ref, cn_ref, oa_ref = rest[:8]
    fn_ref, acc_ref, halo_ref, am_ref, al_ref, aacc_ref = rest[8:]
    i = pl.program_id(0)
    f = pl.program_id(1)
    s = i * nf + f
    active = s < n_seq * steps_per_seq
    prologue, ffn_main, epilogue = _ffn_parts(i, f, *ffn_in, o_ref, cn_ref, fn_ref, acc_ref, halo_ref,
                                              nf=nf, **static)
    init, attn_main, finish = _attn_sample_parts(
        s % steps_per_seq, steps_per_seq, lam_ref, qbd_ref, k_refs, v_refs, bias_ref, kn_ref, vn_ref,
        bn_ref, sg_ref, oa_ref, am_ref, al_ref, aacc_ref, n_new=n_new, lam_init=lam_init)
    prologue()

    @pl.when(active)
    def _():
        init()

    @pl.when(active)
    def _():
        attn_main()
        ffn_main()

    @pl.when(jnp.logical_not(active))
    def _():
        ffn_main()

    epilogue()

    @pl.when(active)
    def _():
        finish()


def _ffn(h, g, prev, wg, wv, cw, cb, wdn, tm, shift, tiles_per_seq, attn=None):
    m = h.shape[0]
    nf = D_FF_PAD // FFN_TF
    static = dict(tm=tm, shift=shift, tiles_per_seq=tiles_per_seq, nf=nf)
    in_specs = [
        pl.BlockSpec((tm, D_MODEL), lambda i, f, *_: (i, 0)),
        pl.BlockSpec((1, D_MODEL), lambda i, f, *_: (0, 0)),
        pl.BlockSpec((1, CONV_HALO, FFN_TF), lambda i, f, *_: (i // tiles_per_seq, 0, f)),
        pl.BlockSpec((D_MODEL, FFN_TF), lambda i, f, *_: (0, f)),
        pl.BlockSpec((D_MODEL, FFN_TF), lambda i, f, *_: (0, f)),
        pl.BlockSpec((CONV_W, FFN_TF), lambda i, f, *_: (0, f)),
        pl.BlockSpec((1, FFN_TF), lambda i, f, *_: (0, f)),
        pl.BlockSpec((FFN_TF, D_MODEL), lambda i, f, *_: (f, 0)),
    ]
    assert len(in_specs) == N_FFN_IN
    out_specs = [
        pl.BlockSpec((tm, D_MODEL), lambda i, f, *_: (i, 0)),
        pl.BlockSpec((1, 2 * shift, FFN_TF), lambda i, f, *_: (i, 0, f)),
    ]
    out_shape = [
        jax.ShapeDtypeStruct((m, D_MODEL), F32),
        jax.ShapeDtypeStruct((m // tm, 2 * shift, D_FF_PAD), F32),
    ]
    scratch = [
        pltpu.VMEM((tm, D_MODEL), BF16),
        pltpu.VMEM((tm, D_MODEL), F32),
        pltpu.VMEM((nf, CONV_HALO, FFN_TF), F32),
    ]
    args = (h, g, prev, wg, wv, cw, cb, wdn)
    grid = (m // tm, nf)
    if attn is None:
        return pl.pallas_call(
            functools.partial(_ffn_kernel, **static),
            grid=grid, in_specs=in_specs, out_specs=out_specs, out_shape=out_shape,
            scratch_shapes=scratch,
            compiler_params=_params(("arbitrary", "arbitrary")),
            name="conv_ffn",
        )(*args)

    (page_table, qbd, k_pool, v_pool, bias_last, k_new, v_new, bias_new, lam_rows, subln_g,
     n_pages_step, lam_init) = attn
    n_seq, n_pages = page_table.shape
    rows = qbd.shape[1]
    n_new = rows // (2 * N_HEADS)
    steps_per_seq = n_pages // n_pages_step
    last = n_seq * steps_per_seq - 1
    assert grid[0] * grid[1] > last

    def seq_of(i, f):
        return jnp.minimum(i * nf + f, last) // steps_per_seq

    def page_spec(p, shape):
        def index(i, f, pt):
            s = jnp.minimum(i * nf + f, last)
            return (pt[s // steps_per_seq, (s % steps_per_seq) * n_pages_step + p], 0, 0)
        return pl.BlockSpec((1,) + shape, index)

    in_specs += (
        [pl.BlockSpec((4, QK_DIM), lambda i, f, pt: (0, 0)),
         pl.BlockSpec((1, rows, ATT_W), lambda i, f, pt: (seq_of(i, f), 0, 0))]
        + [page_spec(p, (ATT_W, PAGE_SIZE)) for p in range(n_pages_step)]
        + [page_spec(p, (PAGE_SIZE * N_HEADS, V_HEAD)) for p in range(n_pages_step)]
        + [pl.BlockSpec((rows, PAGE_SIZE), lambda i, f, pt: (0, 0)),
           pl.BlockSpec((1, ATT_W, PAGE_SIZE), lambda i, f, pt: (seq_of(i, f), 0, 0)),
           pl.BlockSpec((1, PAGE_SIZE * N_HEADS, V_HEAD), lambda i, f, pt: (seq_of(i, f), 0, 0)),
           pl.BlockSpec((rows, PAGE_SIZE), lambda i, f, pt: (0, 0)),
           pl.BlockSpec((N_HEADS, V_HEAD), lambda i, f, pt: (0, 0))])
    out_specs.append(pl.BlockSpec((1, n_new, ATT_W), lambda i, f, pt: (seq_of(i, f), 0, 0)))
    out_shape.append(jax.ShapeDtypeStruct((n_seq, n_new, ATT_W), F32))
    scratch += [pltpu.VMEM((rows, LANES), F32), pltpu.VMEM((rows, LANES), F32),
                pltpu.VMEM((rows, V_HEAD), F32)]
    kern = functools.partial(_ffn_attn_kernel, n_pages_step=n_pages_step, steps_per_seq=steps_per_seq,
                             n_seq=n_seq, n_new=n_new, lam_init=lam_init, **static)
    return pl.pallas_call(
        kern,
        grid_spec=pltpu.PrefetchScalarGridSpec(
            num_scalar_prefetch=1, grid=grid, in_specs=in_specs, out_specs=out_specs,
            scratch_shapes=scratch),
        out_shape=out_shape,
        compiler_params=_params(("arbitrary", "arbitrary")),
        name="conv_ffn_attn",
    )(page_table, *args, lam_rows, qbd, *([k_pool] * n_pages_step), *([v_pool] * n_pages_step),
      bias_last, k_new, v_new, bias_new, subln_g)


def _ple_kernel(h_ref, gp_ref, pe_ref, wg_ref, wp_ref, gf_ref, o_ref):
    h = h_ref[...]
    hn = _rms(h, gp_ref[...], NORM_EPS).astype(BF16)
    gate = jax.nn.sigmoid(jnp.dot(hn, wg_ref[...], preferred_element_type=F32))
    proj = jnp.dot(pe_ref[...].astype(BF16), wp_ref[...], preferred_element_type=F32)
    o_ref[...] = _rms(h + gate * proj, gf_ref[...], NORM_EPS)


def _ple(h, gp, pe, wg, wp, gf, tm):
    m = h.shape[0]
    return pl.pallas_call(
        _ple_kernel,
        grid=(m // tm,),
        in_specs=[
            pl.BlockSpec((tm, D_MODEL), lambda i: (i, 0)),
            pl.BlockSpec((1, D_MODEL), lambda i: (0, 0)),
            pl.BlockSpec((tm, PLE_DIM), lambda i: (i, 0)),
            pl.BlockSpec((D_MODEL, D_MODEL), lambda i: (0, 0)),
            pl.BlockSpec((PLE_DIM, D_MODEL), lambda i: (0, 0)),
            pl.BlockSpec((1, D_MODEL), lambda i: (0, 0)),
        ],
        out_specs=pl.BlockSpec((tm, D_MODEL), lambda i: (i, 0)),
        out_shape=jax.ShapeDtypeStruct((m, D_MODEL), F32),
        compiler_params=_params(("arbitrary",)),
        name="ple_final",
    )(h, gp, pe, wg, wp, gf)


def _t5_bucket(dist):
    max_exact = N_BUCKETS // 2
    n = jnp.maximum(dist, 0)
    nf = jnp.maximum(n, 1).astype(F32)
    large = max_exact + (jnp.log(nf / max_exact) / math.log(MAX_DISTANCE / max_exact)
                         * (N_BUCKETS - max_exact)).astype(jnp.int32)
    large = jnp.minimum(large, N_BUCKETS - 1)
    return jnp.where(n < max_exact, n, large)


def _toeplitz(w, n):
    lead = w.shape[:-1]
    flat = jnp.tile(w, (1,) * len(lead) + (n,))[..., :n * (2 * n - 1)]
    return flat.reshape(lead + (n, 2 * n - 1))[..., :n]


def _prompt_bias_tiles(table, n):
    assert n >= MAX_DISTANCE
    far = table[N_BUCKETS - 1]
    bvec = (table[_t5_bucket(jnp.arange(2 * n, dtype=jnp.int32))] - far[None, :]).T
    bvec = bvec * LOG2E
    d = jnp.arange(2 * n, dtype=jnp.int32)
    d = jnp.where(d >= n, d - 2 * n, d)
    diag = jnp.where(d[None, :] <= 0, bvec[:, jnp.abs(d)], -jnp.inf)
    sub = bvec[:, jnp.clip(n - d, 0, 2 * n - 1)]
    return jnp.concatenate([_toeplitz(sub, n), _toeplitz(diag, n)], axis=-1)


def _sample_bias(table, n_new):
    assert PAGE_SIZE >= MAX_DISTANCE
    far = table[N_BUCKETS - 1]
    t = jnp.arange(n_new, dtype=jnp.int32)
    col = jnp.arange(PAGE_SIZE, dtype=jnp.int32)

    def rows(b):
        b = jnp.transpose(b * LOG2E, (2, 0, 1))
        return jnp.broadcast_to(b[:, None], (N_HEADS, 2, n_new, PAGE_SIZE)).reshape(-1, PAGE_SIZE)

    dist_last = PAGE_SIZE + t[:, None] - col[None, :]
    dist_new = t[:, None] - col[None, :]
    b_new = jnp.where((dist_new >= 0)[..., None], table[_t5_bucket(dist_new)] - far, -jnp.inf)
    return rows(table[_t5_bucket(dist_last)] - far), rows(b_new)


def _block_diag(w):
    nb, gp, a, b = w.shape
    eye = jnp.eye(gp, dtype=w.dtype)
    out = w[:, :, :, None, :] * eye[None, :, None, :, None]
    return out.reshape(nb, gp * a, gp * b)


def _ssm_tables(a_re, a_im, b_re, b_im, c_re, c_im, log_dt):
    g, n, gc = N_SSM_GROUPS, SSM_STATE, SSM_GROUP
    dt = jnp.exp(log_dt)[:, None]
    decay = jnp.exp(a_re * dt)
    ab_re = decay * jnp.cos(a_im * dt)
    ab_im = decay * jnp.sin(a_im * dt)
    den = a_re * a_re + a_im * a_im
    num_re = ab_re - 1.0
    coef_re = (num_re * a_re + ab_im * a_im) / den
    coef_im = (ab_im * a_re - num_re * a_im) / den
    bb_re = coef_re[..., None] * b_re - coef_im[..., None] * b_im
    bb_im = coef_re[..., None] * b_im + coef_im[..., None] * b_re
    blk = lambda w: w.reshape((N_GROUP_BLOCKS, GROUPS_PER_BLOCK) + w.shape[1:])
    to_cn = lambda w: jnp.swapaxes(blk(w), 2, 3)
    wb = jnp.concatenate([_block_diag(to_cn(bb_re)), _block_diag(to_cn(bb_im))], axis=2)
    to_nc = lambda w: jnp.swapaxes(blk(w), 2, 3)
    cb = jnp.concatenate([_block_diag(to_nc(c_re)), _block_diag(to_nc(-c_im))], axis=1)
    lr = ab_re.reshape(N_GROUP_BLOCKS, STATE_COLS // 2, LANES)
    li = ab_im.reshape(N_GROUP_BLOCKS, STATE_COLS // 2, LANES)
    lam = jnp.stack([jnp.concatenate([lr, lr], axis=1), jnp.concatenate([-li, li], axis=1)], axis=1)
    return wb.astype(BF16), cb.astype(BF16), lam.astype(F32)


def _cast_in_kernel(w_ref, wb_ref, wkt_ref):
    w = w_ref[...]
    wb_ref[...] = w.astype(BF16)
    wkt_ref[...] = w[:, ATT_W:2 * ATT_W].T.astype(BF16)


def _cast_w_in(w, tr=256):
    return pl.pallas_call(
        _cast_in_kernel,
        grid=(D_MODEL // tr,),
        in_specs=[pl.BlockSpec((tr, 4 * ATT_W), lambda i: (i, 0))],
        out_specs=[pl.BlockSpec((tr, 4 * ATT_W), lambda i: (i, 0)),
                   pl.BlockSpec((ATT_W, tr), lambda i: (0, i))],
        out_shape=[jax.ShapeDtypeStruct((D_MODEL, 4 * ATT_W), BF16),
                   jax.ShapeDtypeStruct((ATT_W, D_MODEL), BF16)],
        compiler_params=_params(("arbitrary",)),
        name="cast_w_in",
    )(w)


def _cast_up_kernel(w_ref, wg_ref, wv_ref):
    zeros = jnp.zeros((wg_ref.shape[0], D_FF_PAD - D_FF), BF16)
    for n, o_ref in enumerate((wg_ref, wv_ref)):
        o_ref[:, 0:D_FF] = w_ref[:, n * D_FF:(n + 1) * D_FF].astype(BF16)
        o_ref[:, D_FF:D_FF_PAD] = zeros


def _cast_w_up(w, tr=128):
    out = jax.ShapeDtypeStruct((D_MODEL, D_FF_PAD), BF16)
    spec = pl.BlockSpec((tr, D_FF_PAD), lambda i: (i, 0))
    return pl.pallas_call(
        _cast_up_kernel,
        grid=(D_MODEL // tr,),
        in_specs=[pl.BlockSpec((tr, 2 * D_FF), lambda i: (i, 0))],
        out_specs=[spec, spec],
        out_shape=[out, out],
        compiler_params=_params(("arbitrary",)),
        name="cast_w_up",
    )(w)


def _cast_down_kernel(w_ref, o_ref):
    o_ref[0:D_FF, :] = w_ref[...].astype(BF16)
    o_ref[D_FF:D_FF_PAD, :] = jnp.zeros((D_FF_PAD - D_FF, o_ref.shape[1]), BF16)


def _cast_w_down(w, tc=MXU_WIDTH):
    return pl.pallas_call(
        _cast_down_kernel,
        grid=(D_MODEL // tc,),
        in_specs=[pl.BlockSpec((D_FF, tc), lambda i: (0, i))],
        out_specs=pl.BlockSpec((D_FF_PAD, tc), lambda i: (0, i)),
        out_shape=jax.ShapeDtypeStruct((D_FF_PAD, D_MODEL), BF16),
        compiler_params=_params(("arbitrary",)),
        name="cast_w_down",
    )(w)


def _pad_ff(w, axis):
    pad = [(0, 0)] * w.ndim
    pad[axis] = (0, D_FF_PAD - D_FF)
    return jnp.pad(w, pad)


def _state_to_tiles(re, im):
    shape = (re.shape[0], N_GROUP_BLOCKS, STATE_COLS // 2, LANES)
    return jnp.concatenate([re.reshape(shape), im.reshape(shape)], axis=2)


def _tiles_to_state(tiles):
    half = STATE_COLS // 2
    shape = (1, tiles.shape[0], N_SSM_GROUPS, SSM_STATE)
    return tiles[:, :, :half].reshape(shape), tiles[:, :, half:].reshape(shape)


def kernel(x_prompt, x_sample, cache_k, cache_v, state_ssm_re, state_ssm_im, state_conv, page_table, p_prompt, p_sample, rel_bias, norm_mix_g, w_in, lambda_q1, lambda_k1, lambda_q2, lambda_k2, attn_subln_g, ssm_a_re, ssm_a_im, ssm_b_re, ssm_b_im, ssm_c_re, ssm_c_im, ssm_d, ssm_log_dt, ssm_glu_w, ssm_glu_b, ssm_out_g, w_out, norm_ffn_g, ffn_w_up, ffn_conv_w, ffn_conv_b, ffn_w_down, norm_ple_g, ple_w_gate, ple_w_proj, norm_final_g):
    assert DEPTH == 1
    li = 0
    lam_init = 0.8 - 0.6 * math.exp(-0.3 * li)
    nb, t, _ = x_prompt.shape
    db, ds, _ = x_sample.shape
    assert db == SUBLANES
    m_p = nb * t
    m_s = db * ds

    row = lambda v: v.reshape(1, -1).astype(F32)
    w_in_b, w_kt_b = _cast_w_in(w_in[li])
    w_out_b = w_out[li].astype(BF16).reshape(2, ATT_W, D_MODEL)
    w_ug_b, w_uv_b = _cast_w_up(ffn_w_up[li])
    w_dn_b = _cast_w_down(ffn_w_down[li])
    conv_w = _pad_ff(ffn_conv_w[li], 1)
    conv_b = _pad_ff(ffn_conv_b[li].reshape(1, D_FF), 1)
    w_gate_b = ple_w_gate[li].astype(BF16)
    w_proj_b = ple_w_proj[li].astype(BF16)
    lam_rows = jnp.stack([lambda_q1[li], lambda_k1[li], lambda_q2[li], lambda_k2[li]]).astype(F32)
    subln_g = attn_subln_g[li].astype(F32)
    wb, cb, tbl = _ssm_tables(ssm_a_re[li], ssm_a_im[li], ssm_b_re[li], ssm_b_im[li],
                              ssm_c_re[li], ssm_c_im[li], ssm_log_dt[li])
    glu_w = _block_diag(ssm_glu_w[li].reshape(N_GROUP_BLOCKS, GROUPS_PER_BLOCK, SSM_GROUP, SSM_GROUP)).astype(BF16)
    ssm_args = (wb, cb, tbl, row(ssm_d[li]), glu_w, row(ssm_glu_b[li]), row(ssm_out_g[li]))

    tq = 512
    xp = x_prompt.reshape(m_p, D_MODEL)
    q_p, kt_p, v_p, u_p = _in_proj(xp, row(norm_mix_g[li]), w_in_b, w_kt_b, tm=512,
                                   rows_per_seq=t, k_transposed=True)
    shp = lambda a: a.reshape(nb, t, -1)
    band = _prompt_bias_tiles(rel_bias.astype(F32), MAX_DISTANCE)
    o_att_p = _attn_prompt(shp(q_p), kt_p, shp(v_p), band, lam_rows,
                           subln_g.reshape(N_HEADS, 1, V_HEAD), tq, 2, lam_init)
    zero_state = jnp.zeros((nb, N_GROUP_BLOCKS, 1, STATE_COLS, LANES), F32)
    o_ssm_p, hf_p = _ssm(shp(u_p), zero_state, *ssm_args, tc=256, chains=1, out_dtype=BF16)
    h1_p = _out_proj(xp, o_att_p.reshape(m_p, ATT_W), o_ssm_p.reshape(m_p, SSM_W), w_out_b, tm=512)

    tmaj = lambda a: jnp.swapaxes(a, 0, 1).reshape(m_s, -1)
    bmaj = lambda a: jnp.swapaxes(a.reshape(ds, db, -1), 0, 1)
    xs = tmaj(x_sample)
    q_s, k_s, v_s, u_s = _in_proj(xs, row(norm_mix_g[li]), w_in_b, w_kt_b, tm=m_s,
                                  rows_per_seq=m_s, k_transposed=False)
    q_b = bmaj(q_s)
    q_rep = jnp.tile(q_b, (1, 2 * N_HEADS, 1)).reshape(db, 2 * N_HEADS, ds, ATT_W)
    sel = (jnp.arange(ATT_W)[None, :] // QK_DIM) == jnp.arange(2 * N_HEADS)[:, None]
    qbd = jnp.where(sel[None, :, None, :], q_rep, jnp.zeros_like(q_rep)).reshape(db, 2 * N_HEADS * ds, ATT_W)
    pad_rows = lambda a: jnp.pad(bmaj(a), ((0, 0), (0, PAGE_SIZE - ds), (0, 0)))
    bias_past, bias_new = _sample_bias(rel_bias.astype(F32), ds)
    n_pool = cache_k.shape[1]
    k_pool_t = jnp.transpose(cache_k[li], (0, 2, 3, 4, 1)).reshape(n_pool, ATT_W, PAGE_SIZE)
    k_new_t = jnp.swapaxes(pad_rows(k_s), 1, 2)
    v_pool_r = cache_v[li].reshape(n_pool, PAGE_SIZE * N_HEADS, V_HEAD)
    v_new_r = pad_rows(v_s).reshape(db, PAGE_SIZE * N_HEADS, V_HEAD)
    tm_ffn = 512
    zero_prev = jnp.zeros((nb, CONV_HALO, D_FF_PAD), F32)
    attn_s = (page_table, qbd, k_pool_t, v_pool_r, bias_past, k_new_t, v_new_r, bias_new,
              lam_rows, subln_g, 8, lam_init)
    h2_p, conv_p, o_att_s = _ffn(h1_p, row(norm_ffn_g[li]), zero_prev, w_ug_b, w_uv_b, conv_w, conv_b,
                                 w_dn_b, tm=tm_ffn, shift=1, tiles_per_seq=t // tm_ffn, attn=attn_s)
    y_p = _ple(h2_p, row(norm_ple_g[li]), p_prompt[li].reshape(m_p, PLE_DIM), w_gate_b, w_proj_b,
               row(norm_final_g), tm=512)
    h0_s = jnp.swapaxes(_state_to_tiles(state_ssm_re[li], state_ssm_im[li]), 0, 1)[None]
    o_ssm_s, hf_s = _ssm(u_s[None], h0_s, *ssm_args, tc=m_s, chains=db, out_dtype=F32)
    h1_s = _out_proj(xs, tmaj(o_att_s), o_ssm_s[0], w_out_b, tm=m_s)
    prev_s = _pad_ff(jnp.swapaxes(state_conv[li], 0, 1).reshape(1, 2 * db, D_FF), 2)
    h2_s, conv_s = _ffn(h1_s, row(norm_ffn_g[li]), prev_s, w_ug_b, w_uv_b, conv_w, conv_b, w_dn_b,
                        tm=m_s, shift=db, tiles_per_seq=1)
    y_s = _ple(h2_s, row(norm_ple_g[li]), tmaj(p_sample[li]), w_gate_b, w_proj_b,
               row(norm_final_g), tm=m_s)

    re_p, im_p = _tiles_to_state(hf_p[:, :, 0])
    re_s, im_s = _tiles_to_state(jnp.swapaxes(hf_s[0], 0, 1))
    tiles = t // tm_ffn
    conv_p = conv_p[tiles - 1::tiles][None, :, :, :D_FF]
    conv_s = jnp.swapaxes(conv_s.reshape(2, db, D_FF_PAD), 0, 1)[None, :, :, :D_FF]
    return (y_p.reshape(nb, t, D_MODEL),
            bmaj(y_s),
            jnp.transpose(kt_p.reshape(1, nb, N_HEADS, 2, QK_DIM, t), (0, 1, 5, 2, 3, 4)),
            v_p.reshape(1, nb, t, N_HEADS, V_HEAD),
            re_p, im_p, conv_p,
            bmaj(k_s).reshape(1, db, ds, N_HEADS, 2, QK_DIM),
            bmaj(v_s).reshape(1, db, ds, N_HEADS, V_HEAD),
            re_s, im_s, conv_s)
```

```python
import functools
import math

import jax
import jax.numpy as jnp
from jax import lax
from jax.experimental import pallas as pl
from jax.experimental.pallas import tpu as pltpu

F32 = jnp.float32
BF16 = jnp.bfloat16

D_MODEL = 2048
DEPTH = 1
PAGE_SIZE = 128
ATT_W = 1024
SSM_W = 1024
V_HEAD = 128
N_HEADS = 8
QK_DIM = 64
SSM_GROUP = 16
N_SSM_GROUPS = 64
SSM_STATE = 64
D_FF = 5504
CONV_W = 3
PLE_DIM = 256
N_BUCKETS = 32
MAX_DISTANCE = 128
NORM_EPS = 1e-6
SUBLN_EPS = 1e-5
LOG2E = math.log2(math.e)
QK_SCALE = QK_DIM ** -0.5 * LOG2E

LANES = 128
SUBLANES = 8
MXU_WIDTH = 256
VMEM_LIMIT = 56 * 1024 * 1024

GROUPS_PER_BLOCK = LANES // SSM_GROUP
N_GROUP_BLOCKS = N_SSM_GROUPS // GROUPS_PER_BLOCK
STATE_LANES = GROUPS_PER_BLOCK * SSM_STATE

D_FF_PAD = 5632
FFN_TF = 512
ROW_TILE = 512
ATTN_TQ = 512
ATTN_HEADS_PER_STEP = 2
SSM_ROWS = 256
PAGES_PER_STEP = 8
CONV_HALO = 16


def _params(sem):
    return pltpu.CompilerParams(dimension_semantics=sem, vmem_limit_bytes=VMEM_LIMIT)


def _rms(x, g, eps):
    return x * lax.rsqrt(jnp.mean(x * x, axis=-1, keepdims=True) + eps) * g


def _in_proj_kernel(x_ref, g_ref, w_ref, wkt_ref, *refs, k_transposed, n_cast):
    q_ref, k_ref, v_ref, u_ref = refs[n_cast:n_cast + 4]
    for src, dst in zip(refs[:n_cast], refs[n_cast + 4:]):
        dst[...] = src[...].astype(BF16)
    xn = _rms(x_ref[...], g_ref[...], NORM_EPS).astype(BF16)

    def proj(n):
        return jnp.dot(xn, w_ref[:, n * ATT_W:(n + 1) * ATT_W], preferred_element_type=F32)

    q_ref[...] = (proj(0) * QK_SCALE).astype(BF16)
    if k_transposed:
        k_ref[0] = lax.dot_general(wkt_ref[...], xn, (((1,), (1,)), ((), ())),
                                   preferred_element_type=F32)
    else:
        k_ref[...] = proj(1)
    v_ref[...] = proj(2)
    u_ref[...] = proj(3)


def _resident(shape):
    return pl.BlockSpec(shape, lambda *_: (0,) * len(shape), pipeline_mode=pl.Buffered(1))


def _in_proj(x, g, w, wkt, tm, rows_per_seq, k_transposed, cast=()):
    m = x.shape[0]
    steps = m // tm
    cast_specs = [pl.BlockSpec((c.shape[0] // steps, c.shape[1]), lambda i: (i, 0)) for c in cast]
    assert all(c.shape[0] % (steps * 2 * SUBLANES) == 0 for c in cast)
    out_spec = pl.BlockSpec((tm, ATT_W), lambda i: (i, 0))
    if k_transposed:
        tiles = rows_per_seq // tm
        k_spec = pl.BlockSpec((1, ATT_W, tm), lambda i: (i // tiles, 0, i % tiles))
        k_shape = jax.ShapeDtypeStruct((m // rows_per_seq, ATT_W, rows_per_seq), F32)
    else:
        k_spec = out_spec
        k_shape = jax.ShapeDtypeStruct((m, ATT_W), F32)
    return pl.pallas_call(
        functools.partial(_in_proj_kernel, k_transposed=k_transposed, n_cast=len(cast)),
        grid=(steps,),
        in_specs=[
            pl.BlockSpec((tm, D_MODEL), lambda i: (i, 0)),
            _resident((1, D_MODEL)),
            _resident((D_MODEL, 4 * ATT_W)),
            _resident((ATT_W, D_MODEL)),
        ] + cast_specs,
        out_specs=[out_spec, k_spec, out_spec, out_spec] + cast_specs,
        out_shape=[
            jax.ShapeDtypeStruct((m, ATT_W), BF16),
            k_shape,
            jax.ShapeDtypeStruct((m, ATT_W), F32),
            jax.ShapeDtypeStruct((m, SSM_W), F32),
        ] + [jax.ShapeDtypeStruct(c.shape, BF16) for c in cast],
        compiler_params=_params(("arbitrary",)),
        name="in_proj",
    )(x, g, w, wkt, *cast)


def _lambda(lam_ref, lam_init):
    s1 = jnp.sum(lam_ref[0:1, :] * lam_ref[1:2, :], axis=-1, keepdims=True)
    s2 = jnp.sum(lam_ref[2:3, :] * lam_ref[3:4, :], axis=-1, keepdims=True)
    return jnp.exp(s1) - jnp.exp(s2) + lam_init


def _online_softmax_step(s, v_blk, m_ref, l_ref, acc_ref):
    m_prev = m_ref[...]
    m_new = jnp.maximum(m_prev, jnp.max(s, axis=-1, keepdims=True))
    alpha = jnp.exp2(m_prev - m_new)
    p = jnp.exp2(s - jnp.concatenate([m_new] * (s.shape[1] // LANES), axis=1))
    l_ref[...] = alpha * l_ref[...] + jnp.sum(p, axis=-1, keepdims=True)
    acc_ref[...] = alpha * acc_ref[...] + jnp.dot(
        p.astype(BF16), v_blk, preferred_element_type=F32)
    m_ref[...] = m_new


def _attn_prompt_kernel(lam_ref, q_ref, k_ref, v_ref, band_ref, g_ref, wup_ref, o_ref, wg_ref, wv_ref,
                        kb_ref, vb_ref, qs_ref, bias_ref, m_ref, l_ref, acc_ref, *, tq, nh, lam_init):
    _cast_up_tile(wup_ref, wg_ref, wv_ref)
    i = pl.program_id(2)
    head_cols = [slice(h * V_HEAD, (h + 1) * V_HEAD) for h in range(nh)]

    @pl.when(i == 0)
    def _():
        kb_ref[...] = k_ref[0].astype(BF16)
        vb_ref[...] = v_ref[0].astype(BF16)
        w = band_ref.shape[1]
        zero = jnp.zeros((w, w), F32)
        masked = jnp.full((w, w), -jnp.inf, F32)
        n = tq // w
        for h in range(nh):
            left, right = band_ref[h, :, 0:w], band_ref[h, :, w:2 * w]
            for r in range(n):
                for c in range(n):
                    rows = slice(r * w, (r + 1) * w)
                    bias_ref[h, rows, c * w:(c + 1) * w] = left if (r == 0 and c == n - 1) else zero
                    bias_ref[h, rows, tq + c * w:tq + (c + 1) * w] = (
                        right if c == r else left if c == r - 1 else masked if c > r else zero)

    for h in range(nh):
        q = q_ref[0, :, head_cols[h]]
        lane = lax.broadcasted_iota(jnp.int32, q.shape, 1)
        zero = jnp.zeros_like(q)
        qs_ref[h, 0:tq, :] = jnp.where(lane < QK_DIM, q, zero)
        qs_ref[h, tq:2 * tq, :] = jnp.where(lane >= QK_DIM, q, zero)
    m_ref[...] = jnp.full(m_ref.shape, -jnp.inf, F32)
    l_ref[...] = jnp.zeros(l_ref.shape, F32)
    acc_ref[...] = jnp.zeros(acc_ref.shape, F32)

    def update(j, width, biased):
        off = pl.multiple_of(j * tq, tq)
        for h in range(nh):
            rows = slice(h * 2 * QK_DIM, (h + 1) * 2 * QK_DIM)
            s = jnp.dot(qs_ref[h], kb_ref[rows, pl.ds(off, width)], preferred_element_type=F32)
            if biased:
                bias = bias_ref[h, :, 2 * tq - width:2 * tq]
                s = (s.reshape(2, tq, width) + bias[None]).reshape(2 * tq, width)
            _online_softmax_step(s, vb_ref[pl.ds(off, width), head_cols[h]],
                                 m_ref.at[h], l_ref.at[h], acc_ref.at[h])

    n_far = jnp.maximum(i - 1, 0)

    def far_body(t, carry):
        update(2 * t, 2 * tq, False)
        return carry

    lax.fori_loop(0, n_far // 2, far_body, 0)

    @pl.when(n_far % 2 == 1)
    def _():
        update(n_far - 1, tq, False)

    @pl.when(i >= 1)
    def _():
        update(i - 1, 2 * tq, True)

    @pl.when(i == 0)
    def _():
        update(0, tq, True)

    lam = _lambda(lam_ref, lam_init)
    for h in range(nh):
        inv_l = 1.0 / l_ref[h]
        o = acc_ref[h, 0:tq, :] * inv_l[0:tq] - lam * (acc_ref[h, tq:2 * tq, :] * inv_l[tq:2 * tq])
        o_ref[0, :, head_cols[h]] = (_rms(o, g_ref[h], SUBLN_EPS) * (1.0 - lam_init)).astype(o_ref.dtype)


def _attn_prompt(q, k, v, band, lam_rows, subln_g, w_up, tq, nh, lam_init):
    b, t, _ = q.shape
    w = band.shape[1]
    assert tq % w == 0 and band.shape[2] == 2 * w and N_HEADS % nh == 0
    kern = functools.partial(_attn_prompt_kernel, tq=tq, nh=nh, lam_init=lam_init)
    gw = nh * V_HEAD
    n_hg, n_i = N_HEADS // nh, t // tq
    wr = D_MODEL // (b * n_hg * n_i)
    assert wr * b * n_hg * n_i == D_MODEL and wr % (2 * SUBLANES) == 0
    w_index = lambda bi, h, i: ((bi * n_hg + h) * n_i + i, 0)
    w_out = jax.ShapeDtypeStruct((D_MODEL, D_FF_PAD), BF16)
    return pl.pallas_call(
        kern,
        grid=(b, n_hg, n_i),
        in_specs=[
            pl.BlockSpec((4, QK_DIM), lambda bi, h, i: (0, 0)),
            pl.BlockSpec((1, tq, gw), lambda bi, h, i: (bi, i, h)),
            pl.BlockSpec((1, gw, t), lambda bi, h, i: (bi, h, 0)),
            pl.BlockSpec((1, t, gw), lambda bi, h, i: (bi, 0, h)),
            pl.BlockSpec((nh, w, 2 * w), lambda bi, h, i: (h, 0, 0)),
            pl.BlockSpec((nh, 1, V_HEAD), lambda bi, h, i: (h, 0, 0)),
            pl.BlockSpec((wr, 2 * D_FF), w_index),
        ],
        out_specs=[pl.BlockSpec((1, tq, gw), lambda bi, h, i: (bi, i, h)),
                   pl.BlockSpec((wr, D_FF_PAD), w_index),
                   pl.BlockSpec((wr, D_FF_PAD), w_index)],
        out_shape=[jax.ShapeDtypeStruct((b, t, ATT_W), BF16), w_out, w_out],
        scratch_shapes=[
            pltpu.VMEM((gw, t), BF16),
            pltpu.VMEM((t, gw), BF16),
            pltpu.VMEM((nh, 2 * tq, V_HEAD), BF16),
            pltpu.VMEM((nh, tq, 2 * tq), F32),
            pltpu.VMEM((nh, 2 * tq, LANES), F32),
            pltpu.VMEM((nh, 2 * tq, LANES), F32),
            pltpu.VMEM((nh, 2 * tq, V_HEAD), F32),
        ],
        compiler_params=_params(("arbitrary", "arbitrary", "arbitrary")),
        name="attn_prompt",
    )(lam_rows, q, k, v, band, subln_g, w_up)


def _attn_sample_parts(step, n_steps, lam_ref, qbd_ref, k_refs, v_refs, bias_ref, kn_ref, vn_ref, bn_ref,
                       g_ref, o_ref, m_ref, l_ref, acc_ref, *, n_new, lam_init):
    def init():
        @pl.when(step == 0)
        def _():
            m_ref[...] = jnp.full(m_ref.shape, -jnp.inf, F32)
            l_ref[...] = jnp.zeros(l_ref.shape, F32)
            acc_ref[...] = jnp.zeros(acc_ref.shape, F32)

    qbd = qbd_ref[0]
    rows = 2 * n_new

    def update(k_pages, v_page_refs, bias_last):
        parts = [jnp.dot(qbd, kp.astype(BF16), preferred_element_type=F32) for kp in k_pages]
        parts[-1] = parts[-1] + bias_last
        s = jnp.concatenate(parts, axis=1)
        m_prev = m_ref[...]
        m_new = jnp.maximum(m_prev, jnp.max(s, axis=-1, keepdims=True))
        alpha = jnp.exp2(m_prev - m_new)
        p = jnp.exp2(s - jnp.concatenate([m_new] * len(parts), axis=1))
        l_ref[...] = alpha * l_ref[...] + jnp.sum(p, axis=-1, keepdims=True)
        m_ref[...] = m_new
        pb = p.astype(BF16)
        for h in range(N_HEADS):
            v_h = jnp.concatenate(
                [vr[0, pl.ds(h, PAGE_SIZE, stride=N_HEADS), :].astype(BF16) for vr in v_page_refs], axis=0)
            r = slice(h * rows, (h + 1) * rows)
            acc_ref[r, :] = alpha[r] * acc_ref[r, :] + jnp.dot(pb[r], v_h, preferred_element_type=F32)

    is_last = step == n_steps - 1

    def main():
        bias_last = jnp.where(is_last, bias_ref[...], jnp.zeros(bias_ref.shape, F32))
        update([kr[0] for kr in k_refs], v_refs, bias_last)

    def finish():
        @pl.when(is_last)
        def _():
            update([kn_ref[0]], [vn_ref], bn_ref[...])
            lam = _lambda(lam_ref, lam_init)
            acc = acc_ref[...] * (1.0 / l_ref[...])
            for h in range(N_HEADS):
                blk = acc[h * rows:(h + 1) * rows]
                o = blk[0:n_new] - lam * blk[n_new:rows]
                o = _rms(o, g_ref[h:h + 1, :], SUBLN_EPS) * (1.0 - lam_init)
                o_ref[0, :, h * V_HEAD:(h + 1) * V_HEAD] = o

    return init, main, finish


STATE_COLS = 2 * STATE_LANES // LANES


def _state_rows(g, tc, col):
    return pl.ds(g * tc * STATE_COLS + col, tc, stride=STATE_COLS)


def _ssm_kernel(u_ref, h0_ref, wb_ref, cb_ref, lam_ref, d_ref, gw_ref, gbias_ref, og_ref,
                o_ref, hf_ref, hb_ref, gbuf_ref, carry_ref, *, tc, chains):
    c = pl.program_id(1)

    @pl.when(c == 0)
    def _():
        carry_ref[...] = h0_ref[0]

    for g in range(N_GROUP_BLOCKS):
        u = u_ref[0, :, g * LANES:(g + 1) * LANES]
        bu = jnp.dot(u.astype(BF16), wb_ref[g], preferred_element_type=F32)
        for k in range(STATE_COLS):
            hb_ref[_state_rows(g, tc, k), :] = bu[:, k * LANES:(k + 1) * LANES]

    coef = [(lam_ref[g, 0], lam_ref[g, 1]) for g in range(N_GROUP_BLOCKS)]

    def step(g, row, h):
        start = (g * tc + row) * STATE_COLS
        if not isinstance(start, int):
            start = pl.multiple_of(start, STATE_COLS)
        rows = pl.ds(start, STATE_COLS)
        a, b = coef[g]
        h = a * h + b * pltpu.roll(h, STATE_COLS // 2, 0) + hb_ref[rows, :]
        hb_ref[rows, :] = h
        return h

    if chains == 1:
        def body(t, hs):
            return tuple(step(g, t, hs[g]) for g in range(N_GROUP_BLOCKS))

        hs = lax.fori_loop(0, tc, body, tuple(carry_ref[g, 0] for g in range(N_GROUP_BLOCKS)),
                           unroll=SUBLANES)
        for g in range(N_GROUP_BLOCKS):
            carry_ref[g, 0] = hs[g]
    else:
        for g in range(N_GROUP_BLOCKS):
            hs = [carry_ref[g, b] for b in range(chains)]
            for row in range(tc):
                hs[row % chains] = step(g, row, hs[row % chains])
            for b in range(chains):
                carry_ref[g, b] = hs[b]

    ssq = jnp.zeros((tc, 1), F32)
    for g in range(N_GROUP_BLOCKS):
        cols = slice(g * LANES, (g + 1) * LANES)
        h = jnp.concatenate([hb_ref[_state_rows(g, tc, k), :] for k in range(STATE_COLS)], axis=1)
        y = jnp.dot(h.astype(BF16), cb_ref[g], preferred_element_type=F32)
        y = y + d_ref[:, cols] * u_ref[0, :, cols]
        a = jax.nn.gelu(y)
        gate = jnp.dot(a.astype(BF16), gw_ref[g], preferred_element_type=F32) + gbias_ref[:, cols]
        a = a * jax.nn.sigmoid(gate)
        gbuf_ref[:, cols] = a
        ssq = ssq + jnp.sum(a * a, axis=-1, keepdims=True)
    inv = lax.rsqrt(ssq * (1.0 / SSM_W) + NORM_EPS)
    o_ref[0] = (gbuf_ref[...] * inv * og_ref[...]).astype(o_ref.dtype)
    hf_ref[0] = carry_ref[...]


def _ssm(u, h0, wb, cb, lam, d, gw, gbias, og, tc, chains, out_dtype):
    ns, t, _ = u.shape
    kern = functools.partial(_ssm_kernel, tc=tc, chains=chains)
    full = lambda *shape: pl.BlockSpec(shape, lambda s, c: (0,) * len(shape))
    state_shape = (N_GROUP_BLOCKS, chains, STATE_COLS, LANES)
    state_spec = pl.BlockSpec((1,) + state_shape, lambda s, c: (s, 0, 0, 0, 0))
    return pl.pallas_call(
        kern,
        grid=(ns, t // tc),
        in_specs=[
            pl.BlockSpec((1, tc, SSM_W), lambda s, c: (s, c, 0)),
            state_spec,
            full(N_GROUP_BLOCKS, LANES, 2 * STATE_LANES),
            full(N_GROUP_BLOCKS, 2 * STATE_LANES, LANES),
            full(N_GROUP_BLOCKS, 2, STATE_COLS, LANES),
            full(1, SSM_W),
            full(N_GROUP_BLOCKS, LANES, LANES),
            full(1, SSM_W),
            full(1, SSM_W),
        ],
        out_specs=[pl.BlockSpec((1, tc, SSM_W), lambda s, c: (s, c, 0)), state_spec],
        out_shape=[
            jax.ShapeDtypeStruct((ns, t, SSM_W), out_dtype),
            jax.ShapeDtypeStruct((ns,) + state_shape, F32),
        ],
        scratch_shapes=[
            pltpu.VMEM((N_GROUP_BLOCKS * tc * STATE_COLS, LANES), F32),
            pltpu.VMEM((tc, SSM_W), F32),
            pltpu.VMEM(state_shape, F32),
        ],
        compiler_params=_params(("arbitrary", "arbitrary")),
        name="ssm_chains%d" % chains,
    )(u, h0, wb, cb, lam, d, gw, gbias, og)


def _out_proj_kernel(x_ref, oa_ref, os_ref, w_ref, *refs):
    o_ref = refs[-1] if len(refs) == 1 else refs[1]
    acc = jnp.dot(oa_ref[...].astype(BF16), w_ref[0], preferred_element_type=F32)
    acc = acc + jnp.dot(os_ref[...].astype(BF16), w_ref[1], preferred_element_type=F32)
    o_ref[...] = x_ref[...] + acc
    if len(refs) == 3:
        _cast_down_tile(refs[0], refs[2])


def _out_proj(x, o_att, o_ssm, w, tm, w_down=None):
    m = x.shape[0]
    steps = m // tm
    in_specs = [
        pl.BlockSpec((tm, D_MODEL), lambda i: (i, 0)),
        pl.BlockSpec((tm, ATT_W), lambda i: (i, 0)),
        pl.BlockSpec((tm, SSM_W), lambda i: (i, 0)),
        pl.BlockSpec((2, ATT_W, D_MODEL), lambda i: (0, 0, 0)),
    ]
    out_specs = [pl.BlockSpec((tm, D_MODEL), lambda i: (i, 0))]
    out_shape = [jax.ShapeDtypeStruct((m, D_MODEL), F32)]
    args = [x, o_att, o_ssm, w]
    if w_down is not None:
        tc = D_MODEL // steps
        in_specs.append(pl.BlockSpec((D_FF, tc), lambda i: (0, i)))
        out_specs.append(pl.BlockSpec((D_FF_PAD, tc), lambda i: (0, i)))
        out_shape.append(jax.ShapeDtypeStruct((D_FF_PAD, D_MODEL), BF16))
        args.append(w_down)
    out = pl.pallas_call(
        _out_proj_kernel,
        grid=(steps,),
        in_specs=in_specs,
        out_specs=out_specs,
        out_shape=out_shape,
        compiler_params=_params(("arbitrary",)),
        name="out_proj",
    )(*args)
    return out if w_down is not None else out[0]


N_FFN_IN = 8


def _ffn_parts(i, f, h_ref, g_ref, prev_ref, wg_ref, wv_ref, cw_ref, cb_ref, wdn_ref, o_ref, cn_ref,
               fn_ref, acc_ref, halo_ref, *, tm, shift, tiles_per_seq, nf):
    def prologue():
        @pl.when(f == 0)
        def _():
            fn_ref[...] = _rms(h_ref[...], g_ref[...], NORM_EPS).astype(BF16)
            acc_ref[...] = jnp.zeros(acc_ref.shape, F32)

        @pl.when((i % tiles_per_seq) == 0)
        def _():
            halo_ref[f] = prev_ref[0]

    def main():
        fn = fn_ref[...]
        down = None
        for c0 in range(0, FFN_TF, MXU_WIDTH):
            cols = slice(c0, c0 + MXU_WIDTH)
            gate = jnp.dot(fn, wg_ref[:, cols], preferred_element_type=F32)
            val = jnp.dot(fn, wv_ref[:, cols], preferred_element_type=F32)
            gext = jnp.concatenate([halo_ref[f, :, cols], gate], axis=0)
            g1 = gext[CONV_HALO - shift:CONV_HALO - shift + tm]
            g2 = gext[CONV_HALO - 2 * shift:CONV_HALO - 2 * shift + tm]
            conv = cb_ref[:, cols] + (cw_ref[0:1, cols] * g2 + cw_ref[1:2, cols] * g1 + cw_ref[2:3, cols] * gate)
            hact = (conv * jax.nn.sigmoid(conv)) * val
            part = jnp.dot(hact.astype(BF16), wdn_ref[cols, :], preferred_element_type=F32)
            down = part if down is None else down + part
            halo_ref[f, :, cols] = gate[tm - CONV_HALO:tm]
            cn_ref[0, :, cols] = gate[tm - 2 * shift:tm]
        acc_ref[...] += down

    def epilogue():
        @pl.when(f == nf - 1)
        def _():
            o_ref[...] = h_ref[...] + acc_ref[...]

    return prologue, main, epilogue


def _ffn_kernel(*refs, **static):
    prologue, main, epilogue = _ffn_parts(pl.program_id(0), pl.program_id(1), *refs, **static)
    prologue()
    main()
    epilogue()


def _ffn_attn_kernel(pt_ref, *refs, n_pages_step, steps_per_seq, n_seq, n_new, lam_init, nf, **static):
    del pt_ref
    g = n_pages_step
    ffn_in = refs[:N_FFN_IN]
    lam_ref, qbd_ref = refs[N_FFN_IN:N_FFN_IN + 2]
    k_refs = refs[N_FFN_IN + 2:N_FFN_IN + 2 + g]
    v_refs = refs[N_FFN_IN + 2 + g:N_FFN_IN + 2 + 2 * g]
    rest = refs[N_FFN_IN + 2 + 2 * g:]
    bias_ref, kn_ref, vn_ref, bn_ref, sg_ref, o_ref, cn_ref, oa_ref = rest[:8]
    fn_ref, acc_ref, halo_ref, am_ref, al_ref, aacc_ref = rest[8:]
    i = pl.program_id(0)
    f = pl.program_id(1)
    s = i * nf + f
    active = s < n_seq * steps_per_seq
    prologue, ffn_main, epilogue = _ffn_parts(i, f, *ffn_in, o_ref, cn_ref, fn_ref, acc_ref, halo_ref,
                                              nf=nf, **static)
    init, attn_main, finish = _attn_sample_parts(
        s % steps_per_seq, steps_per_seq, lam_ref, qbd_ref, k_refs, v_refs, bias_ref, kn_ref, vn_ref,
        bn_ref, sg_ref, oa_ref, am_ref, al_ref, aacc_ref, n_new=n_new, lam_init=lam_init)
    prologue()

    @pl.when(active)
    def _():
        init()

    @pl.when(active)
    def _():
        ffn_main()
        attn_main()

    @pl.when(jnp.logical_not(active))
    def _():
        ffn_main()

    epilogue()

    @pl.when(active)
    def _():
        finish()


def _ffn(h, g, prev, wg, wv, cw, cb, wdn, tm, shift, tiles_per_seq, attn=None):
    m = h.shape[0]
    nf = D_FF_PAD // FFN_TF
    static = dict(tm=tm, shift=shift, tiles_per_seq=tiles_per_seq, nf=nf)
    in_specs = [
        pl.BlockSpec((tm, D_MODEL), lambda i, f, *_: (i, 0)),
        pl.BlockSpec((1, D_MODEL), lambda i, f, *_: (0, 0)),
        pl.BlockSpec((1, CONV_HALO, FFN_TF), lambda i, f, *_: (i // tiles_per_seq, 0, f)),
        pl.BlockSpec((D_MODEL, FFN_TF), lambda i, f, *_: (0, f)),
        pl.BlockSpec((D_MODEL, FFN_TF), lambda i, f, *_: (0, f)),
        pl.BlockSpec((CONV_W, FFN_TF), lambda i, f, *_: (0, f)),
        pl.BlockSpec((1, FFN_TF), lambda i, f, *_: (0, f)),
        pl.BlockSpec((FFN_TF, D_MODEL), lambda i, f, *_: (f, 0)),
    ]
    assert len(in_specs) == N_FFN_IN
    out_specs = [
        pl.BlockSpec((tm, D_MODEL), lambda i, f, *_: (i, 0)),
        pl.BlockSpec((1, 2 * shift, FFN_TF), lambda i, f, *_: (i, 0, f)),
    ]
    out_shape = [
        jax.ShapeDtypeStruct((m, D_MODEL), F32),
        jax.ShapeDtypeStruct((m // tm, 2 * shift, D_FF_PAD), F32),
    ]
    scratch = [
        pltpu.VMEM((tm, D_MODEL), BF16),
        pltpu.VMEM((tm, D_MODEL), F32),
        pltpu.VMEM((nf, CONV_HALO, FFN_TF), F32),
    ]
    args = (h, g, prev, wg, wv, cw, cb, wdn)
    grid = (m // tm, nf)
    if attn is None:
        return pl.pallas_call(
            functools.partial(_ffn_kernel, **static),
            grid=grid, in_specs=in_specs, out_specs=out_specs, out_shape=out_shape,
            scratch_shapes=scratch,
            compiler_params=_params(("arbitrary", "arbitrary")),
            name="conv_ffn",
        )(*args)

    (page_table, qbd, k_pool, v_pool, bias_last, k_new, v_new, bias_new, lam_rows, subln_g,
     n_pages_step, lam_init) = attn
    n_seq, n_pages = page_table.shape
    rows = qbd.shape[1]
    n_new = rows // (2 * N_HEADS)
    steps_per_seq = n_pages // n_pages_step
    last = n_seq * steps_per_seq - 1
    assert grid[0] * grid[1] > last

    def seq_of(i, f):
        return jnp.minimum(i * nf + f, last) // steps_per_seq

    def page_spec(p, shape):
        def index(i, f, pt):
            s = jnp.minimum(i * nf + f, last)
            return (pt[s // steps_per_seq, (s % steps_per_seq) * n_pages_step + p], 0, 0)
        return pl.BlockSpec((1,) + shape, index)

    in_specs += (
        [pl.BlockSpec((4, QK_DIM), lambda i, f, pt: (0, 0)),
         pl.BlockSpec((1, rows, ATT_W), lambda i, f, pt: (seq_of(i, f), 0, 0))]
        + [page_spec(p, (ATT_W, PAGE_SIZE)) for p in range(n_pages_step)]
        + [page_spec(p, (PAGE_SIZE * N_HEADS, V_HEAD)) for p in range(n_pages_step)]
        + [pl.BlockSpec((rows, PAGE_SIZE), lambda i, f, pt: (0, 0)),
           pl.BlockSpec((1, ATT_W, PAGE_SIZE), lambda i, f, pt: (seq_of(i, f), 0, 0)),
           pl.BlockSpec((1, PAGE_SIZE * N_HEADS, V_HEAD), lambda i, f, pt: (seq_of(i, f), 0, 0)),
           pl.BlockSpec((rows, PAGE_SIZE), lambda i, f, pt: (0, 0)),
           pl.BlockSpec((N_HEADS, V_HEAD), lambda i, f, pt: (0, 0))])
    out_specs.append(pl.BlockSpec((1, n_new, ATT_W), lambda i, f, pt: (seq_of(i, f), 0, 0)))
    out_shape.append(jax.ShapeDtypeStruct((n_seq, n_new, ATT_W), F32))
    scratch += [pltpu.VMEM((rows, LANES), F32), pltpu.VMEM((rows, LANES), F32),
                pltpu.VMEM((rows, V_HEAD), F32)]
    kern = functools.partial(_ffn_attn_kernel, n_pages_step=n_pages_step, steps_per_seq=steps_per_seq,
                             n_seq=n_seq, n_new=n_new, lam_init=lam_init, **static)
    return pl.pallas_call(
        kern,
        grid_spec=pltpu.PrefetchScalarGridSpec(
            num_scalar_prefetch=1, grid=grid, in_specs=in_specs, out_specs=out_specs,
            scratch_shapes=scratch),
        out_shape=out_shape,
        compiler_params=_params(("arbitrary", "arbitrary")),
        name="conv_ffn_attn",
    )(page_table, *args, lam_rows, qbd, *([k_pool] * n_pages_step), *([v_pool] * n_pages_step),
      bias_last, k_new, v_new, bias_new, subln_g)


def _ple_kernel(h_ref, gp_ref, pe_ref, wg_ref, wp_ref, gf_ref, o_ref):
    h = h_ref[...]
    hn = _rms(h, gp_ref[...], NORM_EPS).astype(BF16)
    gate = jax.nn.sigmoid(jnp.dot(hn, wg_ref[...], preferred_element_type=F32))
    proj = jnp.dot(pe_ref[...].astype(BF16), wp_ref[...], preferred_element_type=F32)
    o_ref[...] = _rms(h + gate * proj, gf_ref[...], NORM_EPS)


def _ple(h, gp, pe, wg, wp, gf, tm):
    m = h.shape[0]
    return pl.pallas_call(
        _ple_kernel,
        grid=(m // tm,),
        in_specs=[
            pl.BlockSpec((tm, D_MODEL), lambda i: (i, 0)),
            pl.BlockSpec((1, D_MODEL), lambda i: (0, 0)),
            pl.BlockSpec((tm, PLE_DIM), lambda i: (i, 0)),
            pl.BlockSpec((D_MODEL, D_MODEL), lambda i: (0, 0)),
            pl.BlockSpec((PLE_DIM, D_MODEL), lambda i: (0, 0)),
            pl.BlockSpec((1, D_MODEL), lambda i: (0, 0)),
        ],
        out_specs=pl.BlockSpec((tm, D_MODEL), lambda i: (i, 0)),
        out_shape=jax.ShapeDtypeStruct((m, D_MODEL), F32),
        compiler_params=_params(("arbitrary",)),
        name="ple_final",
    )(h, gp, pe, wg, wp, gf)


def _t5_bucket(dist):
    max_exact = N_BUCKETS // 2
    n = jnp.maximum(dist, 0)
    nf = jnp.maximum(n, 1).astype(F32)
    large = max_exact + (jnp.log(nf / max_exact) / math.log(MAX_DISTANCE / max_exact)
                         * (N_BUCKETS - max_exact)).astype(jnp.int32)
    large = jnp.minimum(large, N_BUCKETS - 1)
    return jnp.where(n < max_exact, n, large)


def _toeplitz(w, n):
    lead = w.shape[:-1]
    flat = jnp.tile(w, (1,) * len(lead) + (n,))[..., :n * (2 * n - 1)]
    return flat.reshape(lead + (n, 2 * n - 1))[..., :n]


def _prompt_bias_tiles(table, n):
    assert n >= MAX_DISTANCE
    far = table[N_BUCKETS - 1]
    bvec = (table[_t5_bucket(jnp.arange(2 * n, dtype=jnp.int32))] - far[None, :]).T
    bvec = bvec * LOG2E
    d = jnp.arange(2 * n, dtype=jnp.int32)
    d = jnp.where(d >= n, d - 2 * n, d)
    diag = jnp.where(d[None, :] <= 0, bvec[:, jnp.abs(d)], -jnp.inf)
    sub = bvec[:, jnp.clip(n - d, 0, 2 * n - 1)]
    return jnp.concatenate([_toeplitz(sub, n), _toeplitz(diag, n)], axis=-1)


def _sample_bias(table, n_new):
    assert PAGE_SIZE >= MAX_DISTANCE
    far = table[N_BUCKETS - 1]
    t = jnp.arange(n_new, dtype=jnp.int32)
    col = jnp.arange(PAGE_SIZE, dtype=jnp.int32)

    def rows(b):
        b = jnp.transpose(b * LOG2E, (2, 0, 1))
        return jnp.broadcast_to(b[:, None], (N_HEADS, 2, n_new, PAGE_SIZE)).reshape(-1, PAGE_SIZE)

    dist_last = PAGE_SIZE + t[:, None] - col[None, :]
    dist_new = t[:, None] - col[None, :]
    b_new = jnp.where((dist_new >= 0)[..., None], table[_t5_bucket(dist_new)] - far, -jnp.inf)
    return rows(table[_t5_bucket(dist_last)] - far), rows(b_new)


def _block_diag(w):
    nb, gp, a, b = w.shape
    eye = jnp.eye(gp, dtype=w.dtype)
    out = w[:, :, :, None, :] * eye[None, :, None, :, None]
    return out.reshape(nb, gp * a, gp * b)


def _ssm_tables(a_re, a_im, b_re, b_im, c_re, c_im, log_dt):
    g, n, gc = N_SSM_GROUPS, SSM_STATE, SSM_GROUP
    dt = jnp.exp(log_dt)[:, None]
    decay = jnp.exp(a_re * dt)
    ab_re = decay * jnp.cos(a_im * dt)
    ab_im = decay * jnp.sin(a_im * dt)
    den = a_re * a_re + a_im * a_im
    num_re = ab_re - 1.0
    coef_re = (num_re * a_re + ab_im * a_im) / den
    coef_im = (ab_im * a_re - num_re * a_im) / den
    bb_re = coef_re[..., None] * b_re - coef_im[..., None] * b_im
    bb_im = coef_re[..., None] * b_im + coef_im[..., None] * b_re
    blk = lambda w: w.reshape((N_GROUP_BLOCKS, GROUPS_PER_BLOCK) + w.shape[1:])
    to_cn = lambda w: jnp.swapaxes(blk(w), 2, 3)
    wb = jnp.concatenate([_block_diag(to_cn(bb_re)), _block_diag(to_cn(bb_im))], axis=2)
    to_nc = lambda w: jnp.swapaxes(blk(w), 2, 3)
    cb = jnp.concatenate([_block_diag(to_nc(c_re)), _block_diag(to_nc(-c_im))], axis=1)
    lr = ab_re.reshape(N_GROUP_BLOCKS, STATE_COLS // 2, LANES)
    li = ab_im.reshape(N_GROUP_BLOCKS, STATE_COLS // 2, LANES)
    lam = jnp.stack([jnp.concatenate([lr, lr], axis=1), jnp.concatenate([-li, li], axis=1)], axis=1)
    return wb.astype(BF16), cb.astype(BF16), lam.astype(F32)


def _cast_in_kernel(w_ref, wb_ref, wkt_ref):
    w = w_ref[...]
    wb_ref[...] = w.astype(BF16)
    wkt_ref[...] = w[:, ATT_W:2 * ATT_W].T.astype(BF16)


def _cast_w_in(w, tr=256):
    return pl.pallas_call(
        _cast_in_kernel,
        grid=(D_MODEL // tr,),
        in_specs=[pl.BlockSpec((tr, 4 * ATT_W), lambda i: (i, 0))],
        out_specs=[pl.BlockSpec((tr, 4 * ATT_W), lambda i: (i, 0)),
                   pl.BlockSpec((ATT_W, tr), lambda i: (0, i))],
        out_shape=[jax.ShapeDtypeStruct((D_MODEL, 4 * ATT_W), BF16),
                   jax.ShapeDtypeStruct((ATT_W, D_MODEL), BF16)],
        compiler_params=_params(("arbitrary",)),
        name="cast_w_in",
    )(w)


def _cast_up_tile(w_ref, wg_ref, wv_ref):
    zeros = jnp.zeros((wg_ref.shape[0], D_FF_PAD - D_FF), BF16)
    for n, o_ref in enumerate((wg_ref, wv_ref)):
        o_ref[:, 0:D_FF] = w_ref[:, n * D_FF:(n + 1) * D_FF].astype(BF16)
        o_ref[:, D_FF:D_FF_PAD] = zeros


def _cast_down_tile(w_ref, o_ref):
    o_ref[0:D_FF, :] = w_ref[...].astype(BF16)
    o_ref[D_FF:D_FF_PAD, :] = jnp.zeros((D_FF_PAD - D_FF, o_ref.shape[1]), BF16)


def _pad_ff(w, axis):
    pad = [(0, 0)] * w.ndim
    pad[axis] = (0, D_FF_PAD - D_FF)
    return jnp.pad(w, pad)


def _state_to_tiles(re, im):
    shape = (re.shape[0], N_GROUP_BLOCKS, STATE_COLS // 2, LANES)
    return jnp.concatenate([re.reshape(shape), im.reshape(shape)], axis=2)


def _tiles_to_state(tiles):
    half = STATE_COLS // 2
    shape = (1, tiles.shape[0], N_SSM_GROUPS, SSM_STATE)
    return tiles[:, :, :half].reshape(shape), tiles[:, :, half:].reshape(shape)


def kernel(x_prompt, x_sample, cache_k, cache_v, state_ssm_re, state_ssm_im, state_conv, page_table, p_prompt, p_sample, rel_bias, norm_mix_g, w_in, lambda_q1, lambda_k1, lambda_q2, lambda_k2, attn_subln_g, ssm_a_re, ssm_a_im, ssm_b_re, ssm_b_im, ssm_c_re, ssm_c_im, ssm_d, ssm_log_dt, ssm_glu_w, ssm_glu_b, ssm_out_g, w_out, norm_ffn_g, ffn_w_up, ffn_conv_w, ffn_conv_b, ffn_w_down, norm_ple_g, ple_w_gate, ple_w_proj, norm_final_g):
    assert DEPTH == 1
    li = 0
    lam_init = 0.8 - 0.6 * math.exp(-0.3 * li)
    nb, t, _ = x_prompt.shape
    db, ds, _ = x_sample.shape
    assert db == SUBLANES
    m_p = nb * t
    m_s = db * ds

    row = lambda v: v.reshape(1, -1).astype(F32)
    w_in_b, w_kt_b = _cast_w_in(w_in[li])
    conv_w = _pad_ff(ffn_conv_w[li], 1)
    conv_b = _pad_ff(ffn_conv_b[li].reshape(1, D_FF), 1)
    w_proj_b = ple_w_proj[li].astype(BF16)
    lam_rows = jnp.stack([lambda_q1[li], lambda_k1[li], lambda_q2[li], lambda_k2[li]]).astype(F32)
    subln_g = attn_subln_g[li].astype(F32)
    wb, cb, tbl = _ssm_tables(ssm_a_re[li], ssm_a_im[li], ssm_b_re[li], ssm_b_im[li],
                              ssm_c_re[li], ssm_c_im[li], ssm_log_dt[li])
    glu_w = _block_diag(ssm_glu_w[li].reshape(N_GROUP_BLOCKS, GROUPS_PER_BLOCK, SSM_GROUP, SSM_GROUP)).astype(BF16)
    ssm_args = (wb, cb, tbl, row(ssm_d[li]), glu_w, row(ssm_glu_b[li]), row(ssm_out_g[li]))

    xp = x_prompt.reshape(m_p, D_MODEL)
    q_p, kt_p, v_p, u_p, w_out_b, w_gate_b = _in_proj(
        xp, row(norm_mix_g[li]), w_in_b, w_kt_b, tm=ROW_TILE, rows_per_seq=t, k_transposed=True,
        cast=(w_out[li], ple_w_gate[li]))
    w_out_b = w_out_b.reshape(2, ATT_W, D_MODEL)
    shp = lambda a: a.reshape(nb, t, -1)
    band = _prompt_bias_tiles(rel_bias.astype(F32), MAX_DISTANCE)
    o_att_p, w_ug_b, w_uv_b = _attn_prompt(shp(q_p), kt_p, shp(v_p), band, lam_rows,
                                           subln_g.reshape(N_HEADS, 1, V_HEAD), ffn_w_up[li],
                                           ATTN_TQ, ATTN_HEADS_PER_STEP, lam_init)
    zero_state = jnp.zeros((nb, N_GROUP_BLOCKS, 1, STATE_COLS, LANES), F32)
    o_ssm_p, hf_p = _ssm(shp(u_p), zero_state, *ssm_args, tc=SSM_ROWS, chains=1, out_dtype=BF16)
    h1_p, w_dn_b = _out_proj(xp, o_att_p.reshape(m_p, ATT_W), o_ssm_p.reshape(m_p, SSM_W), w_out_b,
                             tm=ROW_TILE, w_down=ffn_w_down[li])

    tmaj = lambda a: jnp.swapaxes(a, 0, 1).reshape(m_s, -1)
    bmaj = lambda a: jnp.swapaxes(a.reshape(ds, db, -1), 0, 1)
    xs = tmaj(x_sample)
    q_s, k_s, v_s, u_s = _in_proj(xs, row(norm_mix_g[li]), w_in_b, w_kt_b, tm=m_s,
                                  rows_per_seq=m_s, k_transposed=False)
    q_b = bmaj(q_s)
    q_rep = jnp.tile(q_b, (1, 2 * N_HEADS, 1)).reshape(db, 2 * N_HEADS, ds, ATT_W)
    sel = (jnp.arange(ATT_W)[None, :] // QK_DIM) == jnp.arange(2 * N_HEADS)[:, None]
    qbd = jnp.where(sel[None, :, None, :], q_rep, jnp.zeros_like(q_rep)).reshape(db, 2 * N_HEADS * ds, ATT_W)
    pad_rows = lambda a: jnp.pad(bmaj(a), ((0, 0), (0, PAGE_SIZE - ds), (0, 0)))
    bias_past, bias_new = _sample_bias(rel_bias.astype(F32), ds)
    n_pool = cache_k.shape[1]
    k_pool_t = jnp.transpose(cache_k[li], (0, 2, 3, 4, 1)).reshape(n_pool, ATT_W, PAGE_SIZE)
    k_new_t = jnp.swapaxes(pad_rows(k_s), 1, 2)
    v_pool_r = cache_v[li].reshape(n_pool, PAGE_SIZE * N_HEADS, V_HEAD)
    v_new_r = pad_rows(v_s).reshape(db, PAGE_SIZE * N_HEADS, V_HEAD)
    tm_ffn = ROW_TILE
    zero_prev = jnp.zeros((nb, CONV_HALO, D_FF_PAD), F32)
    attn_s = (page_table, qbd, k_pool_t, v_pool_r, bias_past, k_new_t, v_new_r, bias_new,
              lam_rows, subln_g, PAGES_PER_STEP, lam_init)
    h2_p, conv_p, o_att_s = _ffn(h1_p, row(norm_ffn_g[li]), zero_prev, w_ug_b, w_uv_b, conv_w, conv_b,
                                 w_dn_b, tm=tm_ffn, shift=1, tiles_per_seq=t // tm_ffn, attn=attn_s)
    y_p = _ple(h2_p, row(norm_ple_g[li]), p_prompt[li].reshape(m_p, PLE_DIM), w_gate_b, w_proj_b,
               row(norm_final_g), tm=ROW_TILE)
    h0_s = jnp.swapaxes(_state_to_tiles(state_ssm_re[li], state_ssm_im[li]), 0, 1)[None]
    o_ssm_s, hf_s = _ssm(u_s[None], h0_s, *ssm_args, tc=m_s, chains=db, out_dtype=F32)
    h1_s = _out_proj(xs, tmaj(o_att_s), o_ssm_s[0], w_out_b, tm=m_s)
    prev_s = _pad_ff(jnp.swapaxes(state_conv[li], 0, 1).reshape(1, 2 * db, D_FF), 2)
    h2_s, conv_s = _ffn(h1_s, row(norm_ffn_g[li]), prev_s, w_ug_b, w_uv_b, conv_w, conv_b, w_dn_b,
                        tm=m_s, shift=db, tiles_per_seq=1)
    y_s = _ple(h2_s, row(norm_ple_g[li]), tmaj(p_sample[li]), w_gate_b, w_proj_b,
               row(norm_final_g), tm=m_s)

    re_p, im_p = _tiles_to_state(hf_p[:, :, 0])
    re_s, im_s = _tiles_to_state(jnp.swapaxes(hf_s[0], 0, 1))
    tiles = t // tm_ffn
    conv_p = conv_p[tiles - 1::tiles][None, :, :, :D_FF]
    conv_s = jnp.swapaxes(conv_s.reshape(2, db, D_FF_PAD), 0, 1)[None, :, :, :D_FF]
    return (y_p.reshape(nb, t, D_MODEL),
            bmaj(y_s),
            jnp.transpose(kt_p.reshape(1, nb, N_HEADS, 2, QK_DIM, t), (0, 1, 5, 2, 3, 4)),
            v_p.reshape(1, nb, t, N_HEADS, V_HEAD),
            re_p, im_p, conv_p,
            bmaj(k_s).reshape(1, db, ds, N_HEADS, 2, QK_DIM),
            bmaj(v_s).reshape(1, db, ds, N_HEADS, V_HEAD),
            re_s, im_s, conv_s)
```

```python
import functools
import math

import jax
import jax.numpy as jnp
from jax import lax
from jax.experimental import pallas as pl
from jax.experimental.pallas import tpu as pltpu

F32 = jnp.float32
BF16 = jnp.bfloat16

D_MODEL = 2048
DEPTH = 1
PAGE_SIZE = 128
ATT_W = 1024
SSM_W = 1024
V_HEAD = 128
N_HEADS = 8
QK_DIM = 64
SSM_GROUP = 16
N_SSM_GROUPS = 64
SSM_STATE = 64
D_FF = 5504
CONV_W = 3
PLE_DIM = 256
N_BUCKETS = 32
MAX_DISTANCE = 128
NORM_EPS = 1e-6
SUBLN_EPS = 1e-5
LOG2E = math.log2(math.e)
QK_SCALE = QK_DIM ** -0.5 * LOG2E

LANES = 128
SUBLANES = 8
MXU_WIDTH = 256
VMEM_LIMIT = 56 * 1024 * 1024

GROUPS_PER_BLOCK = LANES // SSM_GROUP
N_GROUP_BLOCKS = N_SSM_GROUPS // GROUPS_PER_BLOCK
STATE_LANES = GROUPS_PER_BLOCK * SSM_STATE

D_FF_PAD = 5632
FFN_TF = 512
ROW_TILE = 512
ATTN_TQ = 512
ATTN_HEADS_PER_STEP = 2
SSM_ROWS = 256
PAGES_PER_STEP = 8
CONV_HALO = 16


def _params(sem):
    return pltpu.CompilerParams(dimension_semantics=sem, vmem_limit_bytes=VMEM_LIMIT)


def _rms(x, g, eps):
    return x * lax.rsqrt(jnp.mean(x * x, axis=-1, keepdims=True) + eps) * g


def _in_proj_kernel(x_ref, g_ref, w_ref, wkt_ref, *refs, k_transposed, n_cast, has_extra):
    n_in = n_cast + has_extra
    outs = refs[n_in:n_in + 4]
    for src, dst in zip(refs[:n_cast], refs[n_in + 4:n_in + 4 + n_cast]):
        dst[...] = src[...].astype(BF16)

    def project(x, q_ref, k_ref, v_ref, u_ref, transposed):
        xn = _rms(x, g_ref[...], NORM_EPS).astype(BF16)

        def proj(n):
            return jnp.dot(xn, w_ref[:, n * ATT_W:(n + 1) * ATT_W], preferred_element_type=F32)

        q_ref[...] = (proj(0) * QK_SCALE).astype(BF16)
        if transposed:
            k_ref[0] = lax.dot_general(wkt_ref[...], xn, (((1,), (1,)), ((), ())),
                                       preferred_element_type=F32)
        else:
            k_ref[...] = proj(1)
        v_ref[...] = proj(2)
        u_ref[...] = proj(3)

    project(x_ref[...], *outs, k_transposed)
    if has_extra:
        @pl.when(pl.program_id(0) == 0)
        def _():
            project(refs[n_cast][...], *refs[n_in + 4 + n_cast:], False)


def _resident(shape):
    return pl.BlockSpec(shape, lambda *_: (0,) * len(shape), pipeline_mode=pl.Buffered(1))


def _in_proj(x, g, w, wkt, tm, rows_per_seq, k_transposed, cast=(), extra=None):
    m = x.shape[0]
    steps = m // tm
    extra_in, extra_specs, extra_shapes = [], [], []
    if extra is not None:
        me = extra.shape[0]
        extra_in = [_resident((me, D_MODEL))]
        extra_specs = [pl.BlockSpec((me, ATT_W), lambda i: (0, 0))] * 4
        extra_shapes = [jax.ShapeDtypeStruct((me, ATT_W), BF16)] + [jax.ShapeDtypeStruct((me, ATT_W), F32)] * 3
    cast_specs = [pl.BlockSpec((c.shape[0] // steps, c.shape[1]), lambda i: (i, 0)) for c in cast]
    assert all(c.shape[0] % (steps * 2 * SUBLANES) == 0 for c in cast)
    out_spec = pl.BlockSpec((tm, ATT_W), lambda i: (i, 0))
    if k_transposed:
        tiles = rows_per_seq // tm
        k_spec = pl.BlockSpec((1, ATT_W, tm), lambda i: (i // tiles, 0, i % tiles))
        k_shape = jax.ShapeDtypeStruct((m // rows_per_seq, ATT_W, rows_per_seq), F32)
    else:
        k_spec = out_spec
        k_shape = jax.ShapeDtypeStruct((m, ATT_W), F32)
    return pl.pallas_call(
        functools.partial(_in_proj_kernel, k_transposed=k_transposed, n_cast=len(cast),
                          has_extra=int(extra is not None)),
        grid=(steps,),
        in_specs=[
            pl.BlockSpec((tm, D_MODEL), lambda i: (i, 0)),
            _resident((1, D_MODEL)),
            _resident((D_MODEL, 4 * ATT_W)),
            _resident((ATT_W, D_MODEL)),
        ] + cast_specs + extra_in,
        out_specs=[out_spec, k_spec, out_spec, out_spec] + cast_specs + extra_specs,
        out_shape=[
            jax.ShapeDtypeStruct((m, ATT_W), BF16),
            k_shape,
            jax.ShapeDtypeStruct((m, ATT_W), F32),
            jax.ShapeDtypeStruct((m, SSM_W), F32),
        ] + [jax.ShapeDtypeStruct(c.shape, BF16) for c in cast] + extra_shapes,
        compiler_params=_params(("arbitrary",)),
        name="in_proj",
    )(x, g, w, wkt, *cast, *([] if extra is None else [extra]))


def _lambda(lam_ref, lam_init):
    s1 = jnp.sum(lam_ref[0:1, :] * lam_ref[1:2, :], axis=-1, keepdims=True)
    s2 = jnp.sum(lam_ref[2:3, :] * lam_ref[3:4, :], axis=-1, keepdims=True)
    return jnp.exp(s1) - jnp.exp(s2) + lam_init


def _online_softmax_step(s, v_blk, m_ref, l_ref, acc_ref):
    m_prev = m_ref[...]
    m_new = jnp.maximum(m_prev, jnp.max(s, axis=-1, keepdims=True))
    alpha = jnp.exp2(m_prev - m_new)
    p = jnp.exp2(s - jnp.concatenate([m_new] * (s.shape[1] // LANES), axis=1))
    l_ref[...] = alpha * l_ref[...] + jnp.sum(p, axis=-1, keepdims=True)
    acc_ref[...] = alpha * acc_ref[...] + jnp.dot(
        p.astype(BF16), v_blk, preferred_element_type=F32)
    m_ref[...] = m_new


def _attn_prompt_kernel(lam_ref, q_ref, k_ref, v_ref, band_ref, g_ref, wup_ref, o_ref, wg_ref, wv_ref,
                        kb_ref, vb_ref, qs_ref, bias_ref, m_ref, l_ref, acc_ref, *, tq, nh, lam_init):
    _cast_up_tile(wup_ref, wg_ref, wv_ref)
    i = pl.program_id(2)
    head_cols = [slice(h * V_HEAD, (h + 1) * V_HEAD) for h in range(nh)]

    @pl.when(i == 0)
    def _():
        kb_ref[...] = k_ref[0].astype(BF16)
        vb_ref[...] = v_ref[0].astype(BF16)
        w = band_ref.shape[1]
        zero = jnp.zeros((w, w), F32)
        masked = jnp.full((w, w), -jnp.inf, F32)
        n = tq // w
        for h in range(nh):
            left, right = band_ref[h, :, 0:w], band_ref[h, :, w:2 * w]
            for r in range(n):
                for c in range(n):
                    rows = slice(r * w, (r + 1) * w)
                    bias_ref[h, rows, c * w:(c + 1) * w] = left if (r == 0 and c == n - 1) else zero
                    bias_ref[h, rows, tq + c * w:tq + (c + 1) * w] = (
                        right if c == r else left if c == r - 1 else masked if c > r else zero)

    for h in range(nh):
        q = q_ref[0, :, head_cols[h]]
        lane = lax.broadcasted_iota(jnp.int32, q.shape, 1)
        zero = jnp.zeros_like(q)
        qs_ref[h, 0:tq, :] = jnp.where(lane < QK_DIM, q, zero)
        qs_ref[h, tq:2 * tq, :] = jnp.where(lane >= QK_DIM, q, zero)
    m_ref[...] = jnp.full(m_ref.shape, -jnp.inf, F32)
    l_ref[...] = jnp.zeros(l_ref.shape, F32)
    acc_ref[...] = jnp.zeros(acc_ref.shape, F32)

    def update(j, width, biased):
        off = pl.multiple_of(j * tq, tq)
        for h in range(nh):
            rows = slice(h * 2 * QK_DIM, (h + 1) * 2 * QK_DIM)
            s = jnp.dot(qs_ref[h], kb_ref[rows, pl.ds(off, width)], preferred_element_type=F32)
            if biased:
                bias = bias_ref[h, :, 2 * tq - width:2 * tq]
                s = (s.reshape(2, tq, width) + bias[None]).reshape(2 * tq, width)
            _online_softmax_step(s, vb_ref[pl.ds(off, width), head_cols[h]],
                                 m_ref.at[h], l_ref.at[h], acc_ref.at[h])

    n_far = jnp.maximum(i - 1, 0)

    def far_body(t, carry):
        update(2 * t, 2 * tq, False)
        return carry

    lax.fori_loop(0, n_far // 2, far_body, 0)

    @pl.when(n_far % 2 == 1)
    def _():
        update(n_far - 1, tq, False)

    @pl.when(i >= 1)
    def _():
        update(i - 1, 2 * tq, True)

    @pl.when(i == 0)
    def _():
        update(0, tq, True)

    lam = _lambda(lam_ref, lam_init)
    for h in range(nh):
        inv_l = 1.0 / l_ref[h]
        o = acc_ref[h, 0:tq, :] * inv_l[0:tq] - lam * (acc_ref[h, tq:2 * tq, :] * inv_l[tq:2 * tq])
        o_ref[0, :, head_cols[h]] = (_rms(o, g_ref[h], SUBLN_EPS) * (1.0 - lam_init)).astype(o_ref.dtype)


def _attn_prompt(q, k, v, band, lam_rows, subln_g, w_up, tq, nh, lam_init):
    b, t, _ = q.shape
    w = band.shape[1]
    assert tq % w == 0 and band.shape[2] == 2 * w and N_HEADS % nh == 0
    kern = functools.partial(_attn_prompt_kernel, tq=tq, nh=nh, lam_init=lam_init)
    gw = nh * V_HEAD
    n_hg, n_i = N_HEADS // nh, t // tq
    wr = D_MODEL // (b * n_hg * n_i)
    assert wr * b * n_hg * n_i == D_MODEL and wr % (2 * SUBLANES) == 0
    w_index = lambda bi, h, i: ((bi * n_hg + h) * n_i + i, 0)
    w_out = jax.ShapeDtypeStruct((D_MODEL, D_FF_PAD), BF16)
    return pl.pallas_call(
        kern,
        grid=(b, n_hg, n_i),
        in_specs=[
            pl.BlockSpec((4, QK_DIM), lambda bi, h, i: (0, 0)),
            pl.BlockSpec((1, tq, gw), lambda bi, h, i: (bi, i, h)),
            pl.BlockSpec((1, gw, t), lambda bi, h, i: (bi, h, 0)),
            pl.BlockSpec((1, t, gw), lambda bi, h, i: (bi, 0, h)),
            pl.BlockSpec((nh, w, 2 * w), lambda bi, h, i: (h, 0, 0)),
            pl.BlockSpec((nh, 1, V_HEAD), lambda bi, h, i: (h, 0, 0)),
            pl.BlockSpec((wr, 2 * D_FF), w_index),
        ],
        out_specs=[pl.BlockSpec((1, tq, gw), lambda bi, h, i: (bi, i, h)),
                   pl.BlockSpec((wr, D_FF_PAD), w_index),
                   pl.BlockSpec((wr, D_FF_PAD), w_index)],
        out_shape=[jax.ShapeDtypeStruct((b, t, ATT_W), BF16), w_out, w_out],
        scratch_shapes=[
            pltpu.VMEM((gw, t), BF16),
            pltpu.VMEM((t, gw), BF16),
            pltpu.VMEM((nh, 2 * tq, V_HEAD), BF16),
            pltpu.VMEM((nh, tq, 2 * tq), F32),
            pltpu.VMEM((nh, 2 * tq, LANES), F32),
            pltpu.VMEM((nh, 2 * tq, LANES), F32),
            pltpu.VMEM((nh, 2 * tq, V_HEAD), F32),
        ],
        compiler_params=_params(("arbitrary", "arbitrary", "arbitrary")),
        name="attn_prompt",
    )(lam_rows, q, k, v, band, subln_g, w_up)


def _attn_sample_parts(step, n_steps, lam_ref, qbd_ref, k_refs, v_refs, bias_ref, kn_ref, vn_ref, bn_ref,
                       g_ref, o_ref, m_ref, l_ref, acc_ref, *, n_new, lam_init):
    def init():
        @pl.when(step == 0)
        def _():
            m_ref[...] = jnp.full(m_ref.shape, -jnp.inf, F32)
            l_ref[...] = jnp.zeros(l_ref.shape, F32)
            acc_ref[...] = jnp.zeros(acc_ref.shape, F32)

    qbd = qbd_ref[0]
    rows = 2 * n_new

    def update(k_pages, v_page_refs, bias_last):
        parts = [jnp.dot(qbd, kp.astype(BF16), preferred_element_type=F32) for kp in k_pages]
        parts[-1] = parts[-1] + bias_last
        s = jnp.concatenate(parts, axis=1)
        m_prev = m_ref[...]
        m_new = jnp.maximum(m_prev, jnp.max(s, axis=-1, keepdims=True))
        alpha = jnp.exp2(m_prev - m_new)
        p = jnp.exp2(s - jnp.concatenate([m_new] * len(parts), axis=1))
        l_ref[...] = alpha * l_ref[...] + jnp.sum(p, axis=-1, keepdims=True)
        m_ref[...] = m_new
        pb = p.astype(BF16)
        for h in range(N_HEADS):
            v_h = jnp.concatenate(
                [vr[0, pl.ds(h, PAGE_SIZE, stride=N_HEADS), :].astype(BF16) for vr in v_page_refs], axis=0)
            r = slice(h * rows, (h + 1) * rows)
            acc_ref[r, :] = alpha[r] * acc_ref[r, :] + jnp.dot(pb[r], v_h, preferred_element_type=F32)

    is_last = step == n_steps - 1

    def main():
        bias_last = jnp.where(is_last, bias_ref[...], jnp.zeros(bias_ref.shape, F32))
        update([kr[0] for kr in k_refs], v_refs, bias_last)

    def finish():
        @pl.when(is_last)
        def _():
            update([kn_ref[0]], [vn_ref], bn_ref[...])
            lam = _lambda(lam_ref, lam_init)
            acc = acc_ref[...] * (1.0 / l_ref[...])
            for h in range(N_HEADS):
                blk = acc[h * rows:(h + 1) * rows]
                o = blk[0:n_new] - lam * blk[n_new:rows]
                o = _rms(o, g_ref[h:h + 1, :], SUBLN_EPS) * (1.0 - lam_init)
                o_ref[0, :, h * V_HEAD:(h + 1) * V_HEAD] = o

    return init, main, finish


STATE_COLS = 2 * STATE_LANES // LANES


def _state_rows(g, tc, col):
    return pl.ds(g * tc * STATE_COLS + col, tc, stride=STATE_COLS)


def _ssm_kernel(u_ref, h0_ref, wb_ref, cb_ref, lam_ref, d_ref, gw_ref, gbias_ref, og_ref,
                o_ref, hf_ref, hb_ref, gbuf_ref, carry_ref, *, tc, chains):
    c = pl.program_id(1)

    @pl.when(c == 0)
    def _():
        carry_ref[...] = h0_ref[0]

    for g in range(N_GROUP_BLOCKS):
        u = u_ref[0, :, g * LANES:(g + 1) * LANES]
        bu = jnp.dot(u.astype(BF16), wb_ref[g], preferred_element_type=F32)
        for k in range(STATE_COLS):
            hb_ref[_state_rows(g, tc, k), :] = bu[:, k * LANES:(k + 1) * LANES]

    coef = [(lam_ref[g, 0], lam_ref[g, 1]) for g in range(N_GROUP_BLOCKS)]

    def step(g, row, h):
        start = (g * tc + row) * STATE_COLS
        if not isinstance(start, int):
            start = pl.multiple_of(start, STATE_COLS)
        rows = pl.ds(start, STATE_COLS)
        a, b = coef[g]
        h = a * h + b * pltpu.roll(h, STATE_COLS // 2, 0) + hb_ref[rows, :]
        hb_ref[rows, :] = h
        return h

    if chains == 1:
        def body(t, hs):
            return tuple(step(g, t, hs[g]) for g in range(N_GROUP_BLOCKS))

        hs = lax.fori_loop(0, tc, body, tuple(carry_ref[g, 0] for g in range(N_GROUP_BLOCKS)),
                           unroll=SUBLANES)
        for g in range(N_GROUP_BLOCKS):
            carry_ref[g, 0] = hs[g]
    else:
        for g in range(N_GROUP_BLOCKS):
            hs = [carry_ref[g, b] for b in range(chains)]
            for row in range(tc):
                hs[row % chains] = step(g, row, hs[row % chains])
            for b in range(chains):
                carry_ref[g, b] = hs[b]

    ssq = jnp.zeros((tc, 1), F32)
    for g in range(N_GROUP_BLOCKS):
        cols = slice(g * LANES, (g + 1) * LANES)
        h = jnp.concatenate([hb_ref[_state_rows(g, tc, k), :] for k in range(STATE_COLS)], axis=1)
        y = jnp.dot(h.astype(BF16), cb_ref[g], preferred_element_type=F32)
        y = y + d_ref[:, cols] * u_ref[0, :, cols]
        a = jax.nn.gelu(y)
        gate = jnp.dot(a.astype(BF16), gw_ref[g], preferred_element_type=F32) + gbias_ref[:, cols]
        a = a * jax.nn.sigmoid(gate)
        gbuf_ref[:, cols] = a
        ssq = ssq + jnp.sum(a * a, axis=-1, keepdims=True)
    inv = lax.rsqrt(ssq * (1.0 / SSM_W) + NORM_EPS)
    o_ref[0] = (gbuf_ref[...] * inv * og_ref[...]).astype(o_ref.dtype)
    hf_ref[0] = carry_ref[...]


def _ssm(u, h0, wb, cb, lam, d, gw, gbias, og, tc, chains, out_dtype):
    ns, t, _ = u.shape
    kern = functools.partial(_ssm_kernel, tc=tc, chains=chains)
    full = lambda *shape: pl.BlockSpec(shape, lambda s, c: (0,) * len(shape))
    state_shape = (N_GROUP_BLOCKS, chains, STATE_COLS, LANES)
    state_spec = pl.BlockSpec((1,) + state_shape, lambda s, c: (s, 0, 0, 0, 0))
    return pl.pallas_call(
        kern,
        grid=(ns, t // tc),
        in_specs=[
            pl.BlockSpec((1, tc, SSM_W), lambda s, c: (s, c, 0)),
            state_spec,
            full(N_GROUP_BLOCKS, LANES, 2 * STATE_LANES),
            full(N_GROUP_BLOCKS, 2 * STATE_LANES, LANES),
            full(N_GROUP_BLOCKS, 2, STATE_COLS, LANES),
            full(1, SSM_W),
            full(N_GROUP_BLOCKS, LANES, LANES),
            full(1, SSM_W),
            full(1, SSM_W),
        ],
        out_specs=[pl.BlockSpec((1, tc, SSM_W), lambda s, c: (s, c, 0)), state_spec],
        out_shape=[
            jax.ShapeDtypeStruct((ns, t, SSM_W), out_dtype),
            jax.ShapeDtypeStruct((ns,) + state_shape, F32),
        ],
        scratch_shapes=[
            pltpu.VMEM((N_GROUP_BLOCKS * tc * STATE_COLS, LANES), F32),
            pltpu.VMEM((tc, SSM_W), F32),
            pltpu.VMEM(state_shape, F32),
        ],
        compiler_params=_params(("arbitrary", "arbitrary")),
        name="ssm_chains%d" % chains,
    )(u, h0, wb, cb, lam, d, gw, gbias, og)


def _out_proj_kernel(x_ref, oa_ref, os_ref, w_ref, *refs):
    o_ref = refs[-1] if len(refs) == 1 else refs[1]
    acc = jnp.dot(oa_ref[...].astype(BF16), w_ref[0], preferred_element_type=F32)
    acc = acc + jnp.dot(os_ref[...].astype(BF16), w_ref[1], preferred_element_type=F32)
    o_ref[...] = x_ref[...] + acc
    if len(refs) == 3:
        _cast_down_tile(refs[0], refs[2])


def _out_proj(x, o_att, o_ssm, w, tm, w_down=None):
    m = x.shape[0]
    steps = m // tm
    in_specs = [
        pl.BlockSpec((tm, D_MODEL), lambda i: (i, 0)),
        pl.BlockSpec((tm, ATT_W), lambda i: (i, 0)),
        pl.BlockSpec((tm, SSM_W), lambda i: (i, 0)),
        pl.BlockSpec((2, ATT_W, D_MODEL), lambda i: (0, 0, 0)),
    ]
    out_specs = [pl.BlockSpec((tm, D_MODEL), lambda i: (i, 0))]
    out_shape = [jax.ShapeDtypeStruct((m, D_MODEL), F32)]
    args = [x, o_att, o_ssm, w]
    if w_down is not None:
        tc = D_MODEL // steps
        in_specs.append(pl.BlockSpec((D_FF, tc), lambda i: (0, i)))
        out_specs.append(pl.BlockSpec((D_FF_PAD, tc), lambda i: (0, i)))
        out_shape.append(jax.ShapeDtypeStruct((D_FF_PAD, D_MODEL), BF16))
        args.append(w_down)
    out = pl.pallas_call(
        _out_proj_kernel,
        grid=(steps,),
        in_specs=in_specs,
        out_specs=out_specs,
        out_shape=out_shape,
        compiler_params=_params(("arbitrary",)),
        name="out_proj",
    )(*args)
    return out if w_down is not None else out[0]


N_FFN_IN = 8


def _ffn_parts(i, f, h_ref, g_ref, prev_ref, wg_ref, wv_ref, cw_ref, cb_ref, wdn_ref, o_ref, cn_ref,
               fn_ref, acc_ref, halo_ref, *, tm, shift, tiles_per_seq, nf):
    def prologue():
        @pl.when(f == 0)
        def _():
            fn_ref[...] = _rms(h_ref[...], g_ref[...], NORM_EPS).astype(BF16)
            acc_ref[...] = jnp.zeros(acc_ref.shape, F32)

        @pl.when((i % tiles_per_seq) == 0)
        def _():
            halo_ref[f] = prev_ref[0]

    def main():
        fn = fn_ref[...]
        down = None
        for c0 in range(0, FFN_TF, MXU_WIDTH):
            cols = slice(c0, c0 + MXU_WIDTH)
            gate = jnp.dot(fn, wg_ref[:, cols], preferred_element_type=F32)
            val = jnp.dot(fn, wv_ref[:, cols], preferred_element_type=F32)
            gext = jnp.concatenate([halo_ref[f, :, cols], gate], axis=0)
            g1 = gext[CONV_HALO - shift:CONV_HALO - shift + tm]
            g2 = gext[CONV_HALO - 2 * shift:CONV_HALO - 2 * shift + tm]
            conv = cb_ref[:, cols] + (cw_ref[0:1, cols] * g2 + cw_ref[1:2, cols] * g1 + cw_ref[2:3, cols] * gate)
            hact = (conv * jax.nn.sigmoid(conv)) * val
            part = jnp.dot(hact.astype(BF16), wdn_ref[cols, :], preferred_element_type=F32)
            down = part if down is None else down + part
            halo_ref[f, :, cols] = gate[tm - CONV_HALO:tm]
            cn_ref[0, :, cols] = gate[tm - 2 * shift:tm]
        acc_ref[...] += down

    def epilogue():
        @pl.when(f == nf - 1)
        def _():
            o_ref[...] = h_ref[...] + acc_ref[...]

    return prologue, main, epilogue


def _ffn_kernel(*refs, **static):
    prologue, main, epilogue = _ffn_parts(pl.program_id(0), pl.program_id(1), *refs, **static)
    prologue()
    main()
    epilogue()


def _ffn_attn_kernel(pt_ref, *refs, n_pages_step, steps_per_seq, n_seq, n_new, lam_init, nf, **static):
    del pt_ref
    g = n_pages_step
    ffn_in = refs[:N_FFN_IN]
    lam_ref, qbd_ref = refs[N_FFN_IN:N_FFN_IN + 2]
    k_refs = refs[N_FFN_IN + 2:N_FFN_IN + 2 + g]
    v_refs = refs[N_FFN_IN + 2 + g:N_FFN_IN + 2 + 2 * g]
    rest = refs[N_FFN_IN + 2 + 2 * g:]
    bias_ref, kn_ref, vn_ref, bn_ref, sg_ref, o_ref, cn_ref, oa_ref = rest[:8]
    fn_ref, acc_ref, halo_ref, am_ref, al_ref, aacc_ref = rest[8:]
    i = pl.program_id(0)
    f = pl.program_id(1)
    s = i * nf + f
    active = s < n_seq * steps_per_seq
    prologue, ffn_main, epilogue = _ffn_parts(i, f, *ffn_in, o_ref, cn_ref, fn_ref, acc_ref, halo_ref,
                                              nf=nf, **static)
    init, attn_main, finish = _attn_sample_parts(
        s % steps_per_seq, steps_per_seq, lam_ref, qbd_ref, k_refs, v_refs, bias_ref, kn_ref, vn_ref,
        bn_ref, sg_ref, oa_ref, am_ref, al_ref, aacc_ref, n_new=n_new, lam_init=lam_init)
    prologue()

    @pl.when(active)
    def _():
        init()

    @pl.when(active)
    def _():
        ffn_main()
        attn_main()

    @pl.when(jnp.logical_not(active))
    def _():
        ffn_main()

    epilogue()

    @pl.when(active)
    def _():
        finish()


def _ffn(h, g, prev, wg, wv, cw, cb, wdn, tm, shift, tiles_per_seq, attn=None):
    m = h.shape[0]
    nf = D_FF_PAD // FFN_TF
    static = dict(tm=tm, shift=shift, tiles_per_seq=tiles_per_seq, nf=nf)
    in_specs = [
        pl.BlockSpec((tm, D_MODEL), lambda i, f, *_: (i, 0)),
        pl.BlockSpec((1, D_MODEL), lambda i, f, *_: (0, 0)),
        pl.BlockSpec((1, CONV_HALO, FFN_TF), lambda i, f, *_: (i // tiles_per_seq, 0, f)),
        pl.BlockSpec((D_MODEL, FFN_TF), lambda i, f, *_: (0, f)),
        pl.BlockSpec((D_MODEL, FFN_TF), lambda i, f, *_: (0, f)),
        pl.BlockSpec((CONV_W, FFN_TF), lambda i, f, *_: (0, f)),
        pl.BlockSpec((1, FFN_TF), lambda i, f, *_: (0, f)),
        pl.BlockSpec((FFN_TF, D_MODEL), lambda i, f, *_: (f, 0)),
    ]
    assert len(in_specs) == N_FFN_IN
    out_specs = [
        pl.BlockSpec((tm, D_MODEL), lambda i, f, *_: (i, 0)),
        pl.BlockSpec((1, 2 * shift, FFN_TF), lambda i, f, *_: (i, 0, f)),
    ]
    out_shape = [
        jax.ShapeDtypeStruct((m, D_MODEL), F32),
        jax.ShapeDtypeStruct((m // tm, 2 * shift, D_FF_PAD), F32),
    ]
    scratch = [
        pltpu.VMEM((tm, D_MODEL), BF16),
        pltpu.VMEM((tm, D_MODEL), F32),
        pltpu.VMEM((nf, CONV_HALO, FFN_TF), F32),
    ]
    args = (h, g, prev, wg, wv, cw, cb, wdn)
    grid = (m // tm, nf)
    if attn is None:
        return pl.pallas_call(
            functools.partial(_ffn_kernel, **static),
            grid=grid, in_specs=in_specs, out_specs=out_specs, out_shape=out_shape,
            scratch_shapes=scratch,
            compiler_params=_params(("arbitrary", "arbitrary")),
            name="conv_ffn",
        )(*args)

    (page_table, qbd, k_pool, v_pool, bias_last, k_new, v_new, bias_new, lam_rows, subln_g,
     n_pages_step, lam_init) = attn
    n_seq, n_pages = page_table.shape
    rows = qbd.shape[1]
    n_new = rows // (2 * N_HEADS)
    steps_per_seq = n_pages // n_pages_step
    last = n_seq * steps_per_seq - 1
    assert grid[0] * grid[1] > last

    def seq_of(i, f):
        return jnp.minimum(i * nf + f, last) // steps_per_seq

    def page_spec(p, shape):
        def index(i, f, pt):
            s = jnp.minimum(i * nf + f, last)
            return (pt[s // steps_per_seq, (s % steps_per_seq) * n_pages_step + p], 0, 0)
        return pl.BlockSpec((1,) + shape, index)

    in_specs += (
        [pl.BlockSpec((4, QK_DIM), lambda i, f, pt: (0, 0)),
         pl.BlockSpec((1, rows, ATT_W), lambda i, f, pt: (seq_of(i, f), 0, 0))]
        + [page_spec(p, (ATT_W, PAGE_SIZE)) for p in range(n_pages_step)]
        + [page_spec(p, (PAGE_SIZE * N_HEADS, V_HEAD)) for p in range(n_pages_step)]
        + [pl.BlockSpec((rows, PAGE_SIZE), lambda i, f, pt: (0, 0)),
           pl.BlockSpec((1, ATT_W, PAGE_SIZE), lambda i, f, pt: (seq_of(i, f), 0, 0)),
           pl.BlockSpec((1, PAGE_SIZE * N_HEADS, V_HEAD), lambda i, f, pt: (seq_of(i, f), 0, 0)),
           pl.BlockSpec((rows, PAGE_SIZE), lambda i, f, pt: (0, 0)),
           pl.BlockSpec((N_HEADS, V_HEAD), lambda i, f, pt: (0, 0))])
    out_specs.append(pl.BlockSpec((1, n_new, ATT_W), lambda i, f, pt: (seq_of(i, f), 0, 0)))
    out_shape.append(jax.ShapeDtypeStruct((n_seq, n_new, ATT_W), F32))
    scratch += [pltpu.VMEM((rows, LANES), F32), pltpu.VMEM((rows, LANES), F32),
                pltpu.VMEM((rows, V_HEAD), F32)]
    kern = functools.partial(_ffn_attn_kernel, n_pages_step=n_pages_step, steps_per_seq=steps_per_seq,
                             n_seq=n_seq, n_new=n_new, lam_init=lam_init, **static)
    return pl.pallas_call(
        kern,
        grid_spec=pltpu.PrefetchScalarGridSpec(
            num_scalar_prefetch=1, grid=grid, in_specs=in_specs, out_specs=out_specs,
            scratch_shapes=scratch),
        out_shape=out_shape,
        compiler_params=_params(("arbitrary", "arbitrary")),
        name="conv_ffn_attn",
    )(page_table, *args, lam_rows, qbd, *([k_pool] * n_pages_step), *([v_pool] * n_pages_step),
      bias_last, k_new, v_new, bias_new, subln_g)


def _ple_kernel(h_ref, gp_ref, pe_ref, wg_ref, wp_ref, gf_ref, o_ref):
    h = h_ref[...]
    hn = _rms(h, gp_ref[...], NORM_EPS).astype(BF16)
    gate = jax.nn.sigmoid(jnp.dot(hn, wg_ref[...], preferred_element_type=F32))
    proj = jnp.dot(pe_ref[...].astype(BF16), wp_ref[...], preferred_element_type=F32)
    o_ref[...] = _rms(h + gate * proj, gf_ref[...], NORM_EPS)


def _ple(h, gp, pe, wg, wp, gf, tm):
    m = h.shape[0]
    return pl.pallas_call(
        _ple_kernel,
        grid=(m // tm,),
        in_specs=[
            pl.BlockSpec((tm, D_MODEL), lambda i: (i, 0)),
            pl.BlockSpec((1, D_MODEL), lambda i: (0, 0)),
            pl.BlockSpec((tm, PLE_DIM), lambda i: (i, 0)),
            pl.BlockSpec((D_MODEL, D_MODEL), lambda i: (0, 0)),
            pl.BlockSpec((PLE_DIM, D_MODEL), lambda i: (0, 0)),
            pl.BlockSpec((1, D_MODEL), lambda i: (0, 0)),
        ],
        out_specs=pl.BlockSpec((tm, D_MODEL), lambda i: (i, 0)),
        out_shape=jax.ShapeDtypeStruct((m, D_MODEL), F32),
        compiler_params=_params(("arbitrary",)),
        name="ple_final",
    )(h, gp, pe, wg, wp, gf)


def _t5_bucket(dist):
    max_exact = N_BUCKETS // 2
    n = jnp.maximum(dist, 0)
    nf = jnp.maximum(n, 1).astype(F32)
    large = max_exact + (jnp.log(nf / max_exact) / math.log(MAX_DISTANCE / max_exact)
                         * (N_BUCKETS - max_exact)).astype(jnp.int32)
    large = jnp.minimum(large, N_BUCKETS - 1)
    return jnp.where(n < max_exact, n, large)


def _toeplitz(w, n):
    lead = w.shape[:-1]
    flat = jnp.tile(w, (1,) * len(lead) + (n,))[..., :n * (2 * n - 1)]
    return flat.reshape(lead + (n, 2 * n - 1))[..., :n]


def _prompt_bias_tiles(table, n):
    assert n >= MAX_DISTANCE
    far = table[N_BUCKETS - 1]
    bvec = (table[_t5_bucket(jnp.arange(2 * n, dtype=jnp.int32))] - far[None, :]).T
    bvec = bvec * LOG2E
    d = jnp.arange(2 * n, dtype=jnp.int32)
    d = jnp.where(d >= n, d - 2 * n, d)
    diag = jnp.where(d[None, :] <= 0, bvec[:, jnp.abs(d)], -jnp.inf)
    sub = bvec[:, jnp.clip(n - d, 0, 2 * n - 1)]
    return jnp.concatenate([_toeplitz(sub, n), _toeplitz(diag, n)], axis=-1)


def _sample_bias(table, n_new):
    assert PAGE_SIZE >= MAX_DISTANCE
    far = table[N_BUCKETS - 1]
    t = jnp.arange(n_new, dtype=jnp.int32)
    col = jnp.arange(PAGE_SIZE, dtype=jnp.int32)

    def rows(b):
        b = jnp.transpose(b * LOG2E, (2, 0, 1))
        return jnp.broadcast_to(b[:, None], (N_HEADS, 2, n_new, PAGE_SIZE)).reshape(-1, PAGE_SIZE)

    dist_last = PAGE_SIZE + t[:, None] - col[None, :]
    dist_new = t[:, None] - col[None, :]
    b_new = jnp.where((dist_new >= 0)[..., None], table[_t5_bucket(dist_new)] - far, -jnp.inf)
    return rows(table[_t5_bucket(dist_last)] - far), rows(b_new)


def _block_diag(w):
    nb, gp, a, b = w.shape
    eye = jnp.eye(gp, dtype=w.dtype)
    out = w[:, :, :, None, :] * eye[None, :, None, :, None]
    return out.reshape(nb, gp * a, gp * b)


def _ssm_tables(a_re, a_im, b_re, b_im, c_re, c_im, log_dt):
    g, n, gc = N_SSM_GROUPS, SSM_STATE, SSM_GROUP
    dt = jnp.exp(log_dt)[:, None]
    decay = jnp.exp(a_re * dt)
    ab_re = decay * jnp.cos(a_im * dt)
    ab_im = decay * jnp.sin(a_im * dt)
    den = a_re * a_re + a_im * a_im
    num_re = ab_re - 1.0
    coef_re = (num_re * a_re + ab_im * a_im) / den
    coef_im = (ab_im * a_re - num_re * a_im) / den
    bb_re = coef_re[..., None] * b_re - coef_im[..., None] * b_im
    bb_im = coef_re[..., None] * b_im + coef_im[..., None] * b_re
    blk = lambda w: w.reshape((N_GROUP_BLOCKS, GROUPS_PER_BLOCK) + w.shape[1:])
    to_cn = lambda w: jnp.swapaxes(blk(w), 2, 3)
    wb = jnp.concatenate([_block_diag(to_cn(bb_re)), _block_diag(to_cn(bb_im))], axis=2)
    to_nc = lambda w: jnp.swapaxes(blk(w), 2, 3)
    cb = jnp.concatenate([_block_diag(to_nc(c_re)), _block_diag(to_nc(-c_im))], axis=1)
    lr = ab_re.reshape(N_GROUP_BLOCKS, STATE_COLS // 2, LANES)
    li = ab_im.reshape(N_GROUP_BLOCKS, STATE_COLS // 2, LANES)
    lam = jnp.stack([jnp.concatenate([lr, lr], axis=1), jnp.concatenate([-li, li], axis=1)], axis=1)
    return wb.astype(BF16), cb.astype(BF16), lam.astype(F32)


def _cast_in_kernel(w_ref, wb_ref, wkt_ref):
    w = w_ref[...]
    wb_ref[...] = w.astype(BF16)
    wkt_ref[...] = w[:, ATT_W:2 * ATT_W].T.astype(BF16)


def _cast_w_in(w, tr=256):
    return pl.pallas_call(
        _cast_in_kernel,
        grid=(D_MODEL // tr,),
        in_specs=[pl.BlockSpec((tr, 4 * ATT_W), lambda i: (i, 0))],
        out_specs=[pl.BlockSpec((tr, 4 * ATT_W), lambda i: (i, 0)),
                   pl.BlockSpec((ATT_W, tr), lambda i: (0, i))],
        out_shape=[jax.ShapeDtypeStruct((D_MODEL, 4 * ATT_W), BF16),
                   jax.ShapeDtypeStruct((ATT_W, D_MODEL), BF16)],
        compiler_params=_params(("arbitrary",)),
        name="cast_w_in",
    )(w)


def _cast_up_tile(w_ref, wg_ref, wv_ref):
    zeros = jnp.zeros((wg_ref.shape[0], D_FF_PAD - D_FF), BF16)
    for n, o_ref in enumerate((wg_ref, wv_ref)):
        o_ref[:, 0:D_FF] = w_ref[:, n * D_FF:(n + 1) * D_FF].astype(BF16)
        o_ref[:, D_FF:D_FF_PAD] = zeros


def _cast_down_tile(w_ref, o_ref):
    o_ref[0:D_FF, :] = w_ref[...].astype(BF16)
    o_ref[D_FF:D_FF_PAD, :] = jnp.zeros((D_FF_PAD - D_FF, o_ref.shape[1]), BF16)


def _pad_ff(w, axis):
    pad = [(0, 0)] * w.ndim
    pad[axis] = (0, D_FF_PAD - D_FF)
    return jnp.pad(w, pad)


def _state_to_tiles(re, im):
    shape = (re.shape[0], N_GROUP_BLOCKS, STATE_COLS // 2, LANES)
    return jnp.concatenate([re.reshape(shape), im.reshape(shape)], axis=2)


def _tiles_to_state(tiles):
    half = STATE_COLS // 2
    shape = (1, tiles.shape[0], N_SSM_GROUPS, SSM_STATE)
    return tiles[:, :, :half].reshape(shape), tiles[:, :, half:].reshape(shape)


def kernel(x_prompt, x_sample, cache_k, cache_v, state_ssm_re, state_ssm_im, state_conv, page_table, p_prompt, p_sample, rel_bias, norm_mix_g, w_in, lambda_q1, lambda_k1, lambda_q2, lambda_k2, attn_subln_g, ssm_a_re, ssm_a_im, ssm_b_re, ssm_b_im, ssm_c_re, ssm_c_im, ssm_d, ssm_log_dt, ssm_glu_w, ssm_glu_b, ssm_out_g, w_out, norm_ffn_g, ffn_w_up, ffn_conv_w, ffn_conv_b, ffn_w_down, norm_ple_g, ple_w_gate, ple_w_proj, norm_final_g):
    assert DEPTH == 1
    li = 0
    lam_init = 0.8 - 0.6 * math.exp(-0.3 * li)
    nb, t, _ = x_prompt.shape
    db, ds, _ = x_sample.shape
    assert db == SUBLANES
    m_p = nb * t
    m_s = db * ds

    row = lambda v: v.reshape(1, -1).astype(F32)
    w_in_b, w_kt_b = _cast_w_in(w_in[li])
    conv_w = _pad_ff(ffn_conv_w[li], 1)
    conv_b = _pad_ff(ffn_conv_b[li].reshape(1, D_FF), 1)
    w_proj_b = ple_w_proj[li].astype(BF16)
    lam_rows = jnp.stack([lambda_q1[li], lambda_k1[li], lambda_q2[li], lambda_k2[li]]).astype(F32)
    subln_g = attn_subln_g[li].astype(F32)
    wb, cb, tbl = _ssm_tables(ssm_a_re[li], ssm_a_im[li], ssm_b_re[li], ssm_b_im[li],
                              ssm_c_re[li], ssm_c_im[li], ssm_log_dt[li])
    glu_w = _block_diag(ssm_glu_w[li].reshape(N_GROUP_BLOCKS, GROUPS_PER_BLOCK, SSM_GROUP, SSM_GROUP)).astype(BF16)
    ssm_args = (wb, cb, tbl, row(ssm_d[li]), glu_w, row(ssm_glu_b[li]), row(ssm_out_g[li]))

    tmaj = lambda a: jnp.swapaxes(a, 0, 1).reshape(m_s, -1)
    bmaj = lambda a: jnp.swapaxes(a.reshape(ds, db, -1), 0, 1)
    xs = tmaj(x_sample)

    xp = x_prompt.reshape(m_p, D_MODEL)
    q_p, kt_p, v_p, u_p, w_out_b, w_gate_b, q_s, k_s, v_s, u_s = _in_proj(
        xp, row(norm_mix_g[li]), w_in_b, w_kt_b, tm=ROW_TILE, rows_per_seq=t, k_transposed=True,
        cast=(w_out[li], ple_w_gate[li]), extra=xs)
    w_out_b = w_out_b.reshape(2, ATT_W, D_MODEL)
    shp = lambda a: a.reshape(nb, t, -1)
    band = _prompt_bias_tiles(rel_bias.astype(F32), MAX_DISTANCE)
    o_att_p, w_ug_b, w_uv_b = _attn_prompt(shp(q_p), kt_p, shp(v_p), band, lam_rows,
                                           subln_g.reshape(N_HEADS, 1, V_HEAD), ffn_w_up[li],
                                           ATTN_TQ, ATTN_HEADS_PER_STEP, lam_init)
    zero_state = jnp.zeros((nb, N_GROUP_BLOCKS, 1, STATE_COLS, LANES), F32)
    o_ssm_p, hf_p = _ssm(shp(u_p), zero_state, *ssm_args, tc=SSM_ROWS, chains=1, out_dtype=BF16)
    h1_p, w_dn_b = _out_proj(xp, o_att_p.reshape(m_p, ATT_W), o_ssm_p.reshape(m_p, SSM_W), w_out_b,
                             tm=ROW_TILE, w_down=ffn_w_down[li])

    q_b = bmaj(q_s)
    q_rep = jnp.tile(q_b, (1, 2 * N_HEADS, 1)).reshape(db, 2 * N_HEADS, ds, ATT_W)
    sel = (jnp.arange(ATT_W)[None, :] // QK_DIM) == jnp.arange(2 * N_HEADS)[:, None]
    qbd = jnp.where(sel[None, :, None, :], q_rep, jnp.zeros_like(q_rep)).reshape(db, 2 * N_HEADS * ds, ATT_W)
    pad_rows = lambda a: jnp.pad(bmaj(a), ((0, 0), (0, PAGE_SIZE - ds), (0, 0)))
    bias_past, bias_new = _sample_bias(rel_bias.astype(F32), ds)
    n_pool = cache_k.shape[1]
    k_pool_t = jnp.transpose(cache_k[li], (0, 2, 3, 4, 1)).reshape(n_pool, ATT_W, PAGE_SIZE)
    k_new_t = jnp.swapaxes(pad_rows(k_s), 1, 2)
    v_pool_r = cache_v[li].reshape(n_pool, PAGE_SIZE * N_HEADS, V_HEAD)
    v_new_r = pad_rows(v_s).reshape(db, PAGE_SIZE * N_HEADS, V_HEAD)
    tm_ffn = ROW_TILE
    zero_prev = jnp.zeros((nb, CONV_HALO, D_FF_PAD), F32)
    attn_s = (page_table, qbd, k_pool_t, v_pool_r, bias_past, k_new_t, v_new_r, bias_new,
              lam_rows, subln_g, PAGES_PER_STEP, lam_init)
    h2_p, conv_p, o_att_s = _ffn(h1_p, row(norm_ffn_g[li]), zero_prev, w_ug_b, w_uv_b, conv_w, conv_b,
                                 w_dn_b, tm=tm_ffn, shift=1, tiles_per_seq=t // tm_ffn, attn=attn_s)
    y_p = _ple(h2_p, row(norm_ple_g[li]), p_prompt[li].reshape(m_p, PLE_DIM), w_gate_b, w_proj_b,
               row(norm_final_g), tm=ROW_TILE)
    h0_s = jnp.swapaxes(_state_to_tiles(state_ssm_re[li], state_ssm_im[li]), 0, 1)[None]
    o_ssm_s, hf_s = _ssm(u_s[None], h0_s, *ssm_args, tc=m_s, chains=db, out_dtype=F32)
    h1_s = _out_proj(xs, tmaj(o_att_s), o_ssm_s[0], w_out_b, tm=m_s)
    prev_s = _pad_ff(jnp.swapaxes(state_conv[li], 0, 1).reshape(1, 2 * db, D_FF), 2)
    h2_s, conv_s = _ffn(h1_s, row(norm_ffn_g[li]), prev_s, w_ug_b, w_uv_b, conv_w, conv_b, w_dn_b,
                        tm=m_s, shift=db, tiles_per_seq=1)
    y_s = _ple(h2_s, row(norm_ple_g[li]), tmaj(p_sample[li]), w_gate_b, w_proj_b,
               row(norm_final_g), tm=m_s)

    re_p, im_p = _tiles_to_state(hf_p[:, :, 0])
    re_s, im_s = _tiles_to_state(jnp.swapaxes(hf_s[0], 0, 1))
    tiles = t // tm_ffn
    conv_p = conv_p[tiles - 1::tiles][None, :, :, :D_FF]
    conv_s = jnp.swapaxes(conv_s.reshape(2, db, D_FF_PAD), 0, 1)[None, :, :, :D_FF]
    return (y_p.reshape(nb, t, D_MODEL),
            bmaj(y_s),
            jnp.transpose(kt_p.reshape(1, nb, N_HEADS, 2, QK_DIM, t), (0, 1, 5, 2, 3, 4)),
            v_p.reshape(1, nb, t, N_HEADS, V_HEAD),
            re_p, im_p, conv_p,
            bmaj(k_s).reshape(1, db, ds, N_HEADS, 2, QK_DIM),
            bmaj(v_s).reshape(1, db, ds, N_HEADS, V_HEAD),
            re_s, im_s, conv_s)
```

```python
import functools
import math

import jax
import jax.numpy as jnp
from jax import lax
from jax.experimental import pallas as pl
from jax.experimental.pallas import tpu as pltpu

F32 = jnp.float32
BF16 = jnp.bfloat16

D_MODEL = 2048
DEPTH = 1
PAGE_SIZE = 128
ATT_W = 1024
SSM_W = 1024
V_HEAD = 128
N_HEADS = 8
QK_DIM = 64
SSM_GROUP = 16
N_SSM_GROUPS = 64
SSM_STATE = 64
D_FF = 5504
CONV_W = 3
PLE_DIM = 256
N_BUCKETS = 32
MAX_DISTANCE = 128
NORM_EPS = 1e-6
SUBLN_EPS = 1e-5
LOG2E = math.log2(math.e)
QK_SCALE = QK_DIM ** -0.5 * LOG2E

LANES = 128
SUBLANES = 8
MXU_WIDTH = 256
VMEM_LIMIT = 56 * 1024 * 1024

GROUPS_PER_BLOCK = LANES // SSM_GROUP
N_GROUP_BLOCKS = N_SSM_GROUPS // GROUPS_PER_BLOCK
STATE_LANES = GROUPS_PER_BLOCK * SSM_STATE

D_FF_PAD = 5632
FFN_TF = 512
ROW_TILE = 512
ATTN_TQ = 512
ATTN_HEADS_PER_STEP = 2
SSM_ROWS = 256
PAGES_PER_STEP = 8
CONV_HALO = 16


def _params(sem):
    return pltpu.CompilerParams(dimension_semantics=sem, vmem_limit_bytes=VMEM_LIMIT)


def _rms(x, g, eps):
    return x * lax.rsqrt(jnp.mean(x * x, axis=-1, keepdims=True) + eps) * g


def _in_proj_kernel(x_ref, g_ref, w_ref, wkt_ref, *refs, k_transposed, n_cast, has_extra):
    n_in = n_cast + has_extra
    outs = refs[n_in:n_in + 4]
    for src, dst in zip(refs[:n_cast], refs[n_in + 4:n_in + 4 + n_cast]):
        dst[...] = src[...].astype(BF16)

    def project(x, q_ref, k_ref, v_ref, u_ref, transposed):
        xn = _rms(x, g_ref[...], NORM_EPS).astype(BF16)

        def proj(n):
            return jnp.dot(xn, w_ref[:, n * ATT_W:(n + 1) * ATT_W], preferred_element_type=F32)

        q_ref[...] = (proj(0) * QK_SCALE).astype(BF16)
        if transposed:
            k_ref[0] = lax.dot_general(wkt_ref[...], xn, (((1,), (1,)), ((), ())),
                                       preferred_element_type=F32)
        else:
            k_ref[...] = proj(1)
        v_ref[...] = proj(2)
        u_ref[...] = proj(3)

    project(x_ref[...], *outs, k_transposed)
    if has_extra:
        @pl.when(pl.program_id(0) == 0)
        def _():
            project(refs[n_cast][...], *refs[n_in + 4 + n_cast:], False)


def _resident(shape):
    return pl.BlockSpec(shape, lambda *_: (0,) * len(shape), pipeline_mode=pl.Buffered(1))


def _in_proj(x, g, w, wkt, tm, rows_per_seq, k_transposed, cast=(), extra=None):
    m = x.shape[0]
    steps = m // tm
    extra_in, extra_specs, extra_shapes = [], [], []
    if extra is not None:
        me = extra.shape[0]
        extra_in = [_resident((me, D_MODEL))]
        extra_specs = [pl.BlockSpec((me, ATT_W), lambda i: (0, 0))] * 4
        extra_shapes = [jax.ShapeDtypeStruct((me, ATT_W), BF16)] + [jax.ShapeDtypeStruct((me, ATT_W), F32)] * 3
    cast_specs = [pl.BlockSpec((c.shape[0] // steps, c.shape[1]), lambda i: (i, 0)) for c in cast]
    assert all(c.shape[0] % (steps * 2 * SUBLANES) == 0 for c in cast)
    out_spec = pl.BlockSpec((tm, ATT_W), lambda i: (i, 0))
    if k_transposed:
        tiles = rows_per_seq // tm
        k_spec = pl.BlockSpec((1, ATT_W, tm), lambda i: (i // tiles, 0, i % tiles))
        k_shape = jax.ShapeDtypeStruct((m // rows_per_seq, ATT_W, rows_per_seq), F32)
    else:
        k_spec = out_spec
        k_shape = jax.ShapeDtypeStruct((m, ATT_W), F32)
    return pl.pallas_call(
        functools.partial(_in_proj_kernel, k_transposed=k_transposed, n_cast=len(cast),
                          has_extra=int(extra is not None)),
        grid=(steps,),
        in_specs=[
            pl.BlockSpec((tm, D_MODEL), lambda i: (i, 0)),
            _resident((1, D_MODEL)),
            _resident((D_MODEL, 4 * ATT_W)),
            _resident((ATT_W, D_MODEL)),
        ] + cast_specs + extra_in,
        out_specs=[out_spec, k_spec, out_spec, out_spec] + cast_specs + extra_specs,
        out_shape=[
            jax.ShapeDtypeStruct((m, ATT_W), BF16),
            k_shape,
            jax.ShapeDtypeStruct((m, ATT_W), F32),
            jax.ShapeDtypeStruct((m, SSM_W), F32),
        ] + [jax.ShapeDtypeStruct(c.shape, BF16) for c in cast] + extra_shapes,
        compiler_params=_params(("arbitrary",)),
        name="in_proj",
    )(x, g, w, wkt, *cast, *([] if extra is None else [extra]))


def _lambda(lam_ref, lam_init):
    s1 = jnp.sum(lam_ref[0:1, :] * lam_ref[1:2, :], axis=-1, keepdims=True)
    s2 = jnp.sum(lam_ref[2:3, :] * lam_ref[3:4, :], axis=-1, keepdims=True)
    return jnp.exp(s1) - jnp.exp(s2) + lam_init


def _online_softmax_step(s, v_blk, m_ref, l_ref, acc_ref):
    m_prev = m_ref[...]
    m_new = jnp.maximum(m_prev, jnp.max(s, axis=-1, keepdims=True))
    alpha = jnp.exp2(m_prev - m_new)
    p = jnp.exp2(s - jnp.concatenate([m_new] * (s.shape[1] // LANES), axis=1))
    l_ref[...] = alpha * l_ref[...] + jnp.sum(p, axis=-1, keepdims=True)
    acc_ref[...] = alpha * acc_ref[...] + jnp.dot(
        p.astype(BF16), v_blk, preferred_element_type=F32)
    m_ref[...] = m_new


def _attn_prompt_kernel(lam_ref, q_ref, k_ref, v_ref, band_ref, g_ref, wup_ref, o_ref, wg_ref, wv_ref,
                        kb_ref, vb_ref, qs_ref, bias_ref, m_ref, l_ref, acc_ref, *, tq, nh, lam_init):
    _cast_up_tile(wup_ref, wg_ref, wv_ref)
    i = pl.program_id(2)
    head_cols = [slice(h * V_HEAD, (h + 1) * V_HEAD) for h in range(nh)]

    @pl.when(i == 0)
    def _():
        kb_ref[...] = k_ref[0].astype(BF16)
        vb_ref[...] = v_ref[0].astype(BF16)
        w = band_ref.shape[1]
        zero = jnp.zeros((w, w), F32)
        masked = jnp.full((w, w), -jnp.inf, F32)
        n = tq // w
        for h in range(nh):
            left, right = band_ref[h, :, 0:w], band_ref[h, :, w:2 * w]
            for r in range(n):
                for c in range(n):
                    rows = slice(r * w, (r + 1) * w)
                    bias_ref[h, rows, c * w:(c + 1) * w] = left if (r == 0 and c == n - 1) else zero
                    bias_ref[h, rows, tq + c * w:tq + (c + 1) * w] = (
                        right if c == r else left if c == r - 1 else masked if c > r else zero)

    for h in range(nh):
        q = q_ref[0, :, head_cols[h]]
        lane = lax.broadcasted_iota(jnp.int32, q.shape, 1)
        zero = jnp.zeros_like(q)
        qs_ref[h, 0:tq, :] = jnp.where(lane < QK_DIM, q, zero)
        qs_ref[h, tq:2 * tq, :] = jnp.where(lane >= QK_DIM, q, zero)
    m_ref[...] = jnp.full(m_ref.shape, -jnp.inf, F32)
    l_ref[...] = jnp.zeros(l_ref.shape, F32)
    acc_ref[...] = jnp.zeros(acc_ref.shape, F32)

    def update(j, width, biased):
        off = pl.multiple_of(j * tq, tq)
        for h in range(nh):
            rows = slice(h * 2 * QK_DIM, (h + 1) * 2 * QK_DIM)
            s = jnp.dot(qs_ref[h], kb_ref[rows, pl.ds(off, width)], preferred_element_type=F32)
            if biased:
                bias = bias_ref[h, :, 2 * tq - width:2 * tq]
                s = (s.reshape(2, tq, width) + bias[None]).reshape(2 * tq, width)
            _online_softmax_step(s, vb_ref[pl.ds(off, width), head_cols[h]],
                                 m_ref.at[h], l_ref.at[h], acc_ref.at[h])

    n_far = jnp.maximum(i - 1, 0)

    def far_body(t, carry):
        update(2 * t, 2 * tq, False)
        return carry

    lax.fori_loop(0, n_far // 2, far_body, 0)

    @pl.when(n_far % 2 == 1)
    def _():
        update(n_far - 1, tq, False)

    @pl.when(i >= 1)
    def _():
        update(i - 1, 2 * tq, True)

    @pl.when(i == 0)
    def _():
        update(0, tq, True)

    lam = _lambda(lam_ref, lam_init)
    for h in range(nh):
        inv_l = 1.0 / l_ref[h]
        o = acc_ref[h, 0:tq, :] * inv_l[0:tq] - lam * (acc_ref[h, tq:2 * tq, :] * inv_l[tq:2 * tq])
        o_ref[0, :, head_cols[h]] = (_rms(o, g_ref[h], SUBLN_EPS) * (1.0 - lam_init)).astype(o_ref.dtype)


def _attn_prompt(q, k, v, band, lam_rows, subln_g, w_up, tq, nh, lam_init):
    b, t, _ = q.shape
    w = band.shape[1]
    assert tq % w == 0 and band.shape[2] == 2 * w and N_HEADS % nh == 0
    kern = functools.partial(_attn_prompt_kernel, tq=tq, nh=nh, lam_init=lam_init)
    gw = nh * V_HEAD
    n_hg, n_i = N_HEADS // nh, t // tq
    wr = D_MODEL // (b * n_hg * n_i)
    assert wr * b * n_hg * n_i == D_MODEL and wr % (2 * SUBLANES) == 0
    w_index = lambda bi, h, i: ((bi * n_hg + h) * n_i + i, 0)
    w_out = jax.ShapeDtypeStruct((D_MODEL, D_FF_PAD), BF16)
    return pl.pallas_call(
        kern,
        grid=(b, n_hg, n_i),
        in_specs=[
            pl.BlockSpec((4, QK_DIM), lambda bi, h, i: (0, 0)),
            pl.BlockSpec((1, tq, gw), lambda bi, h, i: (bi, i, h)),
            pl.BlockSpec((1, gw, t), lambda bi, h, i: (bi, h, 0)),
            pl.BlockSpec((1, t, gw), lambda bi, h, i: (bi, 0, h)),
            pl.BlockSpec((nh, w, 2 * w), lambda bi, h, i: (h, 0, 0)),
            pl.BlockSpec((nh, 1, V_HEAD), lambda bi, h, i: (h, 0, 0)),
            pl.BlockSpec((wr, 2 * D_FF), w_index),
        ],
        out_specs=[pl.BlockSpec((1, tq, gw), lambda bi, h, i: (bi, i, h)),
                   pl.BlockSpec((wr, D_FF_PAD), w_index),
                   pl.BlockSpec((wr, D_FF_PAD), w_index)],
        out_shape=[jax.ShapeDtypeStruct((b, t, ATT_W), BF16), w_out, w_out],
        scratch_shapes=[
            pltpu.VMEM((gw, t), BF16),
            pltpu.VMEM((t, gw), BF16),
            pltpu.VMEM((nh, 2 * tq, V_HEAD), BF16),
            pltpu.VMEM((nh, tq, 2 * tq), F32),
            pltpu.VMEM((nh, 2 * tq, LANES), F32),
            pltpu.VMEM((nh, 2 * tq, LANES), F32),
            pltpu.VMEM((nh, 2 * tq, V_HEAD), F32),
        ],
        compiler_params=_params(("arbitrary", "arbitrary", "arbitrary")),
        name="attn_prompt",
    )(lam_rows, q, k, v, band, subln_g, w_up)


def _attn_sample_parts(step, n_steps, lam_ref, qbd_ref, k_refs, v_refs, bias_ref, kn_ref, vn_ref, bn_ref,
                       g_ref, o_ref, m_ref, l_ref, acc_ref, *, n_new, lam_init):
    def init():
        @pl.when(step == 0)
        def _():
            m_ref[...] = jnp.full(m_ref.shape, -jnp.inf, F32)
            l_ref[...] = jnp.zeros(l_ref.shape, F32)
            acc_ref[...] = jnp.zeros(acc_ref.shape, F32)

    qbd = qbd_ref[0]
    rows = 2 * n_new

    def update(k_pages, v_page_refs, bias_last):
        parts = [jnp.dot(qbd, kp.astype(BF16), preferred_element_type=F32) for kp in k_pages]
        parts[-1] = parts[-1] + bias_last
        s = jnp.concatenate(parts, axis=1)
        m_prev = m_ref[...]
        m_new = jnp.maximum(m_prev, jnp.max(s, axis=-1, keepdims=True))
        alpha = jnp.exp2(m_prev - m_new)
        p = jnp.exp2(s - jnp.concatenate([m_new] * len(parts), axis=1))
        l_ref[...] = alpha * l_ref[...] + jnp.sum(p, axis=-1, keepdims=True)
        m_ref[...] = m_new
        pb = p.astype(BF16)
        for h in range(N_HEADS):
            v_h = jnp.concatenate(
                [vr[0, pl.ds(h, PAGE_SIZE, stride=N_HEADS), :].astype(BF16) for vr in v_page_refs], axis=0)
            r = slice(h * rows, (h + 1) * rows)
            acc_ref[r, :] = alpha[r] * acc_ref[r, :] + jnp.dot(pb[r], v_h, preferred_element_type=F32)

    is_last = step == n_steps - 1

    def main():
        bias_last = jnp.where(is_last, bias_ref[...], jnp.zeros(bias_ref.shape, F32))
        update([kr[0] for kr in k_refs], v_refs, bias_last)

    def finish():
        @pl.when(is_last)
        def _():
            update([kn_ref[0]], [vn_ref], bn_ref[...])
            lam = _lambda(lam_ref, lam_init)
            acc = acc_ref[...] * (1.0 / l_ref[...])
            for h in range(N_HEADS):
                blk = acc[h * rows:(h + 1) * rows]
                o = blk[0:n_new] - lam * blk[n_new:rows]
                o = _rms(o, g_ref[h:h + 1, :], SUBLN_EPS) * (1.0 - lam_init)
                o_ref[0, :, h * V_HEAD:(h + 1) * V_HEAD] = o

    return init, main, finish


STATE_COLS = 2 * STATE_LANES // LANES


def _state_rows(g, tc, col):
    return pl.ds(g * tc * STATE_COLS + col, tc, stride=STATE_COLS)


def _ssm_kernel(u_ref, h0_ref, wb_ref, cb_ref, lam_ref, d_ref, gw_ref, gbias_ref, og_ref,
                o_ref, hf_ref, hb_ref, gbuf_ref, carry_ref, *, tc, chains):
    c = pl.program_id(1)

    @pl.when(c == 0)
    def _():
        carry_ref[...] = h0_ref[0]

    for g in range(N_GROUP_BLOCKS):
        u = u_ref[0, :, g * LANES:(g + 1) * LANES]
        bu = jnp.dot(u.astype(BF16), wb_ref[g], preferred_element_type=F32)
        for k in range(STATE_COLS):
            hb_ref[_state_rows(g, tc, k), :] = bu[:, k * LANES:(k + 1) * LANES]

    coef = [(lam_ref[g, 0], lam_ref[g, 1]) for g in range(N_GROUP_BLOCKS)]

    def step(g, row, h):
        start = (g * tc + row) * STATE_COLS
        if not isinstance(start, int):
            start = pl.multiple_of(start, STATE_COLS)
        rows = pl.ds(start, STATE_COLS)
        a, b = coef[g]
        h = a * h + b * pltpu.roll(h, STATE_COLS // 2, 0) + hb_ref[rows, :]
        hb_ref[rows, :] = h
        return h

    if chains == 1:
        def body(t, hs):
            return tuple(step(g, t, hs[g]) for g in range(N_GROUP_BLOCKS))

        hs = lax.fori_loop(0, tc, body, tuple(carry_ref[g, 0] for g in range(N_GROUP_BLOCKS)),
                           unroll=SUBLANES)
        for g in range(N_GROUP_BLOCKS):
            carry_ref[g, 0] = hs[g]
    else:
        for g in range(N_GROUP_BLOCKS):
            hs = [carry_ref[g, b] for b in range(chains)]
            for row in range(tc):
                hs[row % chains] = step(g, row, hs[row % chains])
            for b in range(chains):
                carry_ref[g, b] = hs[b]

    ssq = jnp.zeros((tc, 1), F32)
    for g in range(N_GROUP_BLOCKS):
        cols = slice(g * LANES, (g + 1) * LANES)
        h = jnp.concatenate([hb_ref[_state_rows(g, tc, k), :] for k in range(STATE_COLS)], axis=1)
        y = jnp.dot(h.astype(BF16), cb_ref[g], preferred_element_type=F32)
        y = y + d_ref[:, cols] * u_ref[0, :, cols]
        a = jax.nn.gelu(y)
        gate = jnp.dot(a.astype(BF16), gw_ref[g], preferred_element_type=F32) + gbias_ref[:, cols]
        a = a * jax.nn.sigmoid(gate)
        gbuf_ref[:, cols] = a
        ssq = ssq + jnp.sum(a * a, axis=-1, keepdims=True)
    inv = lax.rsqrt(ssq * (1.0 / SSM_W) + NORM_EPS)
    o_ref[0] = (gbuf_ref[...] * inv * og_ref[...]).astype(o_ref.dtype)
    hf_ref[0] = carry_ref[...]


def _ssm(u, h0, wb, cb, lam, d, gw, gbias, og, tc, chains, out_dtype):
    ns, t, _ = u.shape
    kern = functools.partial(_ssm_kernel, tc=tc, chains=chains)
    full = lambda *shape: pl.BlockSpec(shape, lambda s, c: (0,) * len(shape))
    state_shape = (N_GROUP_BLOCKS, chains, STATE_COLS, LANES)
    state_spec = pl.BlockSpec((1,) + state_shape, lambda s, c: (s, 0, 0, 0, 0))
    return pl.pallas_call(
        kern,
        grid=(ns, t // tc),
        in_specs=[
            pl.BlockSpec((1, tc, SSM_W), lambda s, c: (s, c, 0)),
            state_spec,
            full(N_GROUP_BLOCKS, LANES, 2 * STATE_LANES),
            full(N_GROUP_BLOCKS, 2 * STATE_LANES, LANES),
            full(N_GROUP_BLOCKS, 2, STATE_COLS, LANES),
            full(1, SSM_W),
            full(N_GROUP_BLOCKS, LANES, LANES),
            full(1, SSM_W),
            full(1, SSM_W),
        ],
        out_specs=[pl.BlockSpec((1, tc, SSM_W), lambda s, c: (s, c, 0)), state_spec],
        out_shape=[
            jax.ShapeDtypeStruct((ns, t, SSM_W), out_dtype),
            jax.ShapeDtypeStruct((ns,) + state_shape, F32),
        ],
        scratch_shapes=[
            pltpu.VMEM((N_GROUP_BLOCKS * tc * STATE_COLS, LANES), F32),
            pltpu.VMEM((tc, SSM_W), F32),
            pltpu.VMEM(state_shape, F32),
        ],
        compiler_params=_params(("arbitrary", "arbitrary")),
        name="ssm_chains%d" % chains,
    )(u, h0, wb, cb, lam, d, gw, gbias, og)


def _out_proj_kernel(x_ref, oa_ref, os_ref, w_ref, *refs):
    o_ref = refs[-1] if len(refs) == 1 else refs[1]
    acc = jnp.dot(oa_ref[...].astype(BF16), w_ref[0], preferred_element_type=F32)
    acc = acc + jnp.dot(os_ref[...].astype(BF16), w_ref[1], preferred_element_type=F32)
    o_ref[...] = x_ref[...] + acc
    if len(refs) == 3:
        _cast_down_tile(refs[0], refs[2])


def _out_proj(x, o_att, o_ssm, w, tm, w_down=None):
    m = x.shape[0]
    steps = m // tm
    in_specs = [
        pl.BlockSpec((tm, D_MODEL), lambda i: (i, 0)),
        pl.BlockSpec((tm, ATT_W), lambda i: (i, 0)),
        pl.BlockSpec((tm, SSM_W), lambda i: (i, 0)),
        pl.BlockSpec((2, ATT_W, D_MODEL), lambda i: (0, 0, 0)),
    ]
    out_specs = [pl.BlockSpec((tm, D_MODEL), lambda i: (i, 0))]
    out_shape = [jax.ShapeDtypeStruct((m, D_MODEL), F32)]
    args = [x, o_att, o_ssm, w]
    if w_down is not None:
        tc = D_MODEL // steps
        in_specs.append(pl.BlockSpec((D_FF, tc), lambda i: (0, i)))
        out_specs.append(pl.BlockSpec((D_FF_PAD, tc), lambda i: (0, i)))
        out_shape.append(jax.ShapeDtypeStruct((D_FF_PAD, D_MODEL), BF16))
        args.append(w_down)
    out = pl.pallas_call(
        _out_proj_kernel,
        grid=(steps,),
        in_specs=in_specs,
        out_specs=out_specs,
        out_shape=out_shape,
        compiler_params=_params(("arbitrary",)),
        name="out_proj",
    )(*args)
    return out if w_down is not None else out[0]


N_FFN_IN = 8


def _ffn_parts(i, f, h_ref, g_ref, prev_ref, wg_ref, wv_ref, cw_ref, cb_ref, wdn_ref, o_ref, cn_ref,
               fn_ref, acc_ref, halo_ref, *, tm, shift, tiles_per_seq, nf):
    def prologue():
        @pl.when(f == 0)
        def _():
            fn_ref[...] = _rms(h_ref[...], g_ref[...], NORM_EPS).astype(BF16)
            acc_ref[...] = jnp.zeros(acc_ref.shape, F32)

        @pl.when((i % tiles_per_seq) == 0)
        def _():
            halo_ref[f] = prev_ref[0]

    def main():
        fn = fn_ref[...]
        down = None
        for c0 in range(0, FFN_TF, MXU_WIDTH):
            cols = slice(c0, c0 + MXU_WIDTH)
            gate = jnp.dot(fn, wg_ref[:, cols], preferred_element_type=F32)
            val = jnp.dot(fn, wv_ref[:, cols], preferred_element_type=F32)
            gext = jnp.concatenate([halo_ref[f, :, cols], gate], axis=0)
            g1 = gext[CONV_HALO - shift:CONV_HALO - shift + tm]
            g2 = gext[CONV_HALO - 2 * shift:CONV_HALO - 2 * shift + tm]
            conv = cb_ref[:, cols] + (cw_ref[0:1, cols] * g2 + cw_ref[1:2, cols] * g1 + cw_ref[2:3, cols] * gate)
            hact = (conv * jax.nn.sigmoid(conv)) * val
            part = jnp.dot(hact.astype(BF16), wdn_ref[cols, :], preferred_element_type=F32)
            down = part if down is None else down + part
            halo_ref[f, :, cols] = gate[tm - CONV_HALO:tm]
            cn_ref[0, :, cols] = gate[tm - 2 * shift:tm]
        acc_ref[...] += down

    def epilogue():
        @pl.when(f == nf - 1)
        def _():
            o_ref[...] = h_ref[...] + acc_ref[...]

    return prologue, main, epilogue


def _ffn_kernel(*refs, **static):
    prologue, main, epilogue = _ffn_parts(pl.program_id(0), pl.program_id(1), *refs, **static)
    prologue()
    main()
    epilogue()


def _ffn_attn_kernel(pt_ref, *refs, n_pages_step, steps_per_seq, n_seq, n_new, lam_init, nf, **static):
    del pt_ref
    g = n_pages_step
    ffn_in = refs[:N_FFN_IN]
    lam_ref, qbd_ref = refs[N_FFN_IN:N_FFN_IN + 2]
    k_refs = refs[N_FFN_IN + 2:N_FFN_IN + 2 + g]
    v_refs = refs[N_FFN_IN + 2 + g:N_FFN_IN + 2 + 2 * g]
    rest = refs[N_FFN_IN + 2 + 2 * g:]
    bias_ref, kn_ref, vn_ref, bn_ref, sg_ref, o_ref, cn_ref, oa_ref = rest[:8]
    fn_ref, acc_ref, halo_ref, am_ref, al_ref, aacc_ref = rest[8:]
    i = pl.program_id(0)
    f = pl.program_id(1)
    s = i * nf + f
    active = s < n_seq * steps_per_seq
    prologue, ffn_main, epilogue = _ffn_parts(i, f, *ffn_in, o_ref, cn_ref, fn_ref, acc_ref, halo_ref,
                                              nf=nf, **static)
    init, attn_main, finish = _attn_sample_parts(
        s % steps_per_seq, steps_per_seq, lam_ref, qbd_ref, k_refs, v_refs, bias_ref, kn_ref, vn_ref,
        bn_ref, sg_ref, oa_ref, am_ref, al_ref, aacc_ref, n_new=n_new, lam_init=lam_init)
    prologue()

    @pl.when(active)
    def _():
        init()

    @pl.when(active)
    def _():
        ffn_main()
        attn_main()

    @pl.when(jnp.logical_not(active))
    def _():
        ffn_main()

    epilogue()

    @pl.when(active)
    def _():
        finish()


def _ffn_specs(m, tm, shift, tiles_per_seq, where):
    nf = D_FF_PAD // FFN_TF

    def spec(shape, index):
        return pl.BlockSpec(shape, lambda *ids: index(*where(*ids)))

    in_specs = [
        spec((tm, D_MODEL), lambda i, f: (i, 0)),
        spec((1, D_MODEL), lambda i, f: (0, 0)),
        spec((1, CONV_HALO, FFN_TF), lambda i, f: (i // tiles_per_seq, 0, f)),
        spec((D_MODEL, FFN_TF), lambda i, f: (0, f)),
        spec((D_MODEL, FFN_TF), lambda i, f: (0, f)),
        spec((CONV_W, FFN_TF), lambda i, f: (0, f)),
        spec((1, FFN_TF), lambda i, f: (0, f)),
        spec((FFN_TF, D_MODEL), lambda i, f: (f, 0)),
    ]
    assert len(in_specs) == N_FFN_IN
    out_specs = [
        spec((tm, D_MODEL), lambda i, f: (i, 0)),
        spec((1, 2 * shift, FFN_TF), lambda i, f: (i, 0, f)),
    ]
    out_shape = [
        jax.ShapeDtypeStruct((m, D_MODEL), F32),
        jax.ShapeDtypeStruct((m // tm, 2 * shift, D_FF_PAD), F32),
    ]
    scratch = [
        pltpu.VMEM((tm, D_MODEL), BF16),
        pltpu.VMEM((tm, D_MODEL), F32),
        pltpu.VMEM((nf, CONV_HALO, FFN_TF), F32),
    ]
    return in_specs, out_specs, out_shape, scratch


def _ffn(h, g, prev, wg, wv, cw, cb, wdn, tm, shift, tiles_per_seq, attn=None):
    m = h.shape[0]
    nf = D_FF_PAD // FFN_TF
    static = dict(tm=tm, shift=shift, tiles_per_seq=tiles_per_seq, nf=nf)
    in_specs, out_specs, out_shape, scratch = _ffn_specs(m, tm, shift, tiles_per_seq,
                                                         lambda i, f, *_: (i, f))
    args = (h, g, prev, wg, wv, cw, cb, wdn)
    grid = (m // tm, nf)
    if attn is None:
        return pl.pallas_call(
            functools.partial(_ffn_kernel, **static),
            grid=grid, in_specs=in_specs, out_specs=out_specs, out_shape=out_shape,
            scratch_shapes=scratch,
            compiler_params=_params(("arbitrary", "arbitrary")),
            name="conv_ffn",
        )(*args)

    (page_table, qbd, k_pool, v_pool, bias_last, k_new, v_new, bias_new, lam_rows, subln_g,
     n_pages_step, lam_init) = attn
    n_seq, n_pages = page_table.shape
    rows = qbd.shape[1]
    n_new = rows // (2 * N_HEADS)
    steps_per_seq = n_pages // n_pages_step
    last = n_seq * steps_per_seq - 1
    assert grid[0] * grid[1] > last

    def seq_of(i, f):
        return jnp.minimum(i * nf + f, last) // steps_per_seq

    def page_spec(p, shape):
        def index(i, f, pt):
            s = jnp.minimum(i * nf + f, last)
            return (pt[s // steps_per_seq, (s % steps_per_seq) * n_pages_step + p], 0, 0)
        return pl.BlockSpec((1,) + shape, index)

    in_specs += (
        [pl.BlockSpec((4, QK_DIM), lambda i, f, pt: (0, 0)),
         pl.BlockSpec((1, rows, ATT_W), lambda i, f, pt: (seq_of(i, f), 0, 0))]
        + [page_spec(p, (ATT_W, PAGE_SIZE)) for p in range(n_pages_step)]
        + [page_spec(p, (PAGE_SIZE * N_HEADS, V_HEAD)) for p in range(n_pages_step)]
        + [pl.BlockSpec((rows, PAGE_SIZE), lambda i, f, pt: (0, 0)),
           pl.BlockSpec((1, ATT_W, PAGE_SIZE), lambda i, f, pt: (seq_of(i, f), 0, 0)),
           pl.BlockSpec((1, PAGE_SIZE * N_HEADS, V_HEAD), lambda i, f, pt: (seq_of(i, f), 0, 0)),
           pl.BlockSpec((rows, PAGE_SIZE), lambda i, f, pt: (0, 0)),
           pl.BlockSpec((N_HEADS, V_HEAD), lambda i, f, pt: (0, 0))])
    out_specs.append(pl.BlockSpec((1, n_new, ATT_W), lambda i, f, pt: (seq_of(i, f), 0, 0)))
    out_shape.append(jax.ShapeDtypeStruct((n_seq, n_new, ATT_W), F32))
    scratch += [pltpu.VMEM((rows, LANES), F32), pltpu.VMEM((rows, LANES), F32),
                pltpu.VMEM((rows, V_HEAD), F32)]
    kern = functools.partial(_ffn_attn_kernel, n_pages_step=n_pages_step, steps_per_seq=steps_per_seq,
                             n_seq=n_seq, n_new=n_new, lam_init=lam_init, **static)
    return pl.pallas_call(
        kern,
        grid_spec=pltpu.PrefetchScalarGridSpec(
            num_scalar_prefetch=1, grid=grid, in_specs=in_specs, out_specs=out_specs,
            scratch_shapes=scratch),
        out_shape=out_shape,
        compiler_params=_params(("arbitrary", "arbitrary")),
        name="conv_ffn_attn",
    )(page_table, *args, lam_rows, qbd, *([k_pool] * n_pages_step), *([v_pool] * n_pages_step),
      bias_last, k_new, v_new, bias_new, subln_g)


N_PLE_IN = 6


def _ple_kernel(h_ref, gp_ref, pe_ref, wg_ref, wp_ref, gf_ref, *refs, side_ffn):
    if side_ffn is not None:
        step = pl.program_id(0)
        ffn_in, (o_ref, o2_ref, cn_ref), scratch = refs[:N_FFN_IN], refs[N_FFN_IN:N_FFN_IN + 3], refs[N_FFN_IN + 3:]
        prologue, main, epilogue = _ffn_parts(step * 0, step, *ffn_in, o2_ref, cn_ref, *scratch, **side_ffn)

        @pl.when(step < side_ffn["nf"])
        def _():
            prologue()
            main()
            epilogue()
    else:
        (o_ref,) = refs
    h = h_ref[...]
    hn = _rms(h, gp_ref[...], NORM_EPS).astype(BF16)
    gate = jax.nn.sigmoid(jnp.dot(hn, wg_ref[...], preferred_element_type=F32))
    proj = jnp.dot(pe_ref[...].astype(BF16), wp_ref[...], preferred_element_type=F32)
    o_ref[...] = _rms(h + gate * proj, gf_ref[...], NORM_EPS)


def _ple(h, gp, pe, wg, wp, gf, tm, side_ffn=None):
    m = h.shape[0]
    steps = m // tm
    in_specs = [
        pl.BlockSpec((tm, D_MODEL), lambda i: (i, 0)),
        pl.BlockSpec((1, D_MODEL), lambda i: (0, 0)),
        pl.BlockSpec((tm, PLE_DIM), lambda i: (i, 0)),
        pl.BlockSpec((D_MODEL, D_MODEL), lambda i: (0, 0)),
        pl.BlockSpec((PLE_DIM, D_MODEL), lambda i: (0, 0)),
        pl.BlockSpec((1, D_MODEL), lambda i: (0, 0)),
    ]
    assert len(in_specs) == N_PLE_IN
    out_specs = [pl.BlockSpec((tm, D_MODEL), lambda i: (i, 0))]
    out_shape = [jax.ShapeDtypeStruct((m, D_MODEL), F32)]
    args, scratch, static = [h, gp, pe, wg, wp, gf], [], None
    if side_ffn is not None:
        *ffn_args, shift = side_ffn
        ms = ffn_args[0].shape[0]
        nf = D_FF_PAD // FFN_TF
        assert steps >= nf
        ffn_in, ffn_out, ffn_shape, scratch = _ffn_specs(
            ms, ms, shift, 1, lambda i: (0, jnp.minimum(i, nf - 1)))
        in_specs, out_specs, out_shape = in_specs + ffn_in, out_specs + ffn_out, out_shape + ffn_shape
        args += ffn_args
        static = dict(tm=ms, shift=shift, tiles_per_seq=1, nf=nf)
    out = pl.pallas_call(
        functools.partial(_ple_kernel, side_ffn=static),
        grid=(steps,),
        in_specs=in_specs,
        out_specs=out_specs,
        out_shape=out_shape,
        scratch_shapes=scratch,
        compiler_params=_params(("arbitrary",)),
        name="ple_final",
    )(*args)
    return out if side_ffn is not None else out[0]


def _t5_bucket(dist):
    max_exact = N_BUCKETS // 2
    n = jnp.maximum(dist, 0)
    nf = jnp.maximum(n, 1).astype(F32)
    large = max_exact + (jnp.log(nf / max_exact) / math.log(MAX_DISTANCE / max_exact)
                         * (N_BUCKETS - max_exact)).astype(jnp.int32)
    large = jnp.minimum(large, N_BUCKETS - 1)
    return jnp.where(n < max_exact, n, large)


def _toeplitz(w, n):
    lead = w.shape[:-1]
    flat = jnp.tile(w, (1,) * len(lead) + (n,))[..., :n * (2 * n - 1)]
    return flat.reshape(lead + (n, 2 * n - 1))[..., :n]


def _prompt_bias_tiles(table, n):
    assert n >= MAX_DISTANCE
    far = table[N_BUCKETS - 1]
    bvec = (table[_t5_bucket(jnp.arange(2 * n, dtype=jnp.int32))] - far[None, :]).T
    bvec = bvec * LOG2E
    d = jnp.arange(2 * n, dtype=jnp.int32)
    d = jnp.where(d >= n, d - 2 * n, d)
    diag = jnp.where(d[None, :] <= 0, bvec[:, jnp.abs(d)], -jnp.inf)
    sub = bvec[:, jnp.clip(n - d, 0, 2 * n - 1)]
    return jnp.concatenate([_toeplitz(sub, n), _toeplitz(diag, n)], axis=-1)


def _sample_bias(table, n_new):
    assert PAGE_SIZE >= MAX_DISTANCE
    far = table[N_BUCKETS - 1]
    t = jnp.arange(n_new, dtype=jnp.int32)
    col = jnp.arange(PAGE_SIZE, dtype=jnp.int32)

    def rows(b):
        b = jnp.transpose(b * LOG2E, (2, 0, 1))
        return jnp.broadcast_to(b[:, None], (N_HEADS, 2, n_new, PAGE_SIZE)).reshape(-1, PAGE_SIZE)

    dist_last = PAGE_SIZE + t[:, None] - col[None, :]
    dist_new = t[:, None] - col[None, :]
    b_new = jnp.where((dist_new >= 0)[..., None], table[_t5_bucket(dist_new)] - far, -jnp.inf)
    return rows(table[_t5_bucket(dist_last)] - far), rows(b_new)


def _block_diag(w):
    nb, gp, a, b = w.shape
    eye = jnp.eye(gp, dtype=w.dtype)
    out = w[:, :, :, None, :] * eye[None, :, None, :, None]
    return out.reshape(nb, gp * a, gp * b)


def _ssm_tables(a_re, a_im, b_re, b_im, c_re, c_im, log_dt):
    g, n, gc = N_SSM_GROUPS, SSM_STATE, SSM_GROUP
    dt = jnp.exp(log_dt)[:, None]
    decay = jnp.exp(a_re * dt)
    ab_re = decay * jnp.cos(a_im * dt)
    ab_im = decay * jnp.sin(a_im * dt)
    den = a_re * a_re + a_im * a_im
    num_re = ab_re - 1.0
    coef_re = (num_re * a_re + ab_im * a_im) / den
    coef_im = (ab_im * a_re - num_re * a_im) / den
    bb_re = coef_re[..., None] * b_re - coef_im[..., None] * b_im
    bb_im = coef_re[..., None] * b_im + coef_im[..., None] * b_re
    blk = lambda w: w.reshape((N_GROUP_BLOCKS, GROUPS_PER_BLOCK) + w.shape[1:])
    to_cn = lambda w: jnp.swapaxes(blk(w), 2, 3)
    wb = jnp.concatenate([_block_diag(to_cn(bb_re)), _block_diag(to_cn(bb_im))], axis=2)
    to_nc = lambda w: jnp.swapaxes(blk(w), 2, 3)
    cb = jnp.concatenate([_block_diag(to_nc(c_re)), _block_diag(to_nc(-c_im))], axis=1)
    lr = ab_re.reshape(N_GROUP_BLOCKS, STATE_COLS // 2, LANES)
    li = ab_im.reshape(N_GROUP_BLOCKS, STATE_COLS // 2, LANES)
    lam = jnp.stack([jnp.concatenate([lr, lr], axis=1), jnp.concatenate([-li, li], axis=1)], axis=1)
    return wb.astype(BF16), cb.astype(BF16), lam.astype(F32)


def _cast_in_kernel(w_ref, wb_ref, wkt_ref):
    w = w_ref[...]
    wb_ref[...] = w.astype(BF16)
    wkt_ref[...] = w[:, ATT_W:2 * ATT_W].T.astype(BF16)


def _cast_w_in(w, tr=256):
    return pl.pallas_call(
        _cast_in_kernel,
        grid=(D_MODEL // tr,),
        in_specs=[pl.BlockSpec((tr, 4 * ATT_W), lambda i: (i, 0))],
        out_specs=[pl.BlockSpec((tr, 4 * ATT_W), lambda i: (i, 0)),
                   pl.BlockSpec((ATT_W, tr), lambda i: (0, i))],
        out_shape=[jax.ShapeDtypeStruct((D_MODEL, 4 * ATT_W), BF16),
                   jax.ShapeDtypeStruct((ATT_W, D_MODEL), BF16)],
        compiler_params=_params(("arbitrary",)),
        name="cast_w_in",
    )(w)


def _cast_up_tile(w_ref, wg_ref, wv_ref):
    zeros = jnp.zeros((wg_ref.shape[0], D_FF_PAD - D_FF), BF16)
    for n, o_ref in enumerate((wg_ref, wv_ref)):
        o_ref[:, 0:D_FF] = w_ref[:, n * D_FF:(n + 1) * D_FF].astype(BF16)
        o_ref[:, D_FF:D_FF_PAD] = zeros


def _cast_down_tile(w_ref, o_ref):
    o_ref[0:D_FF, :] = w_ref[...].astype(BF16)
    o_ref[D_FF:D_FF_PAD, :] = jnp.zeros((D_FF_PAD - D_FF, o_ref.shape[1]), BF16)


def _pad_ff(w, axis):
    pad = [(0, 0)] * w.ndim
    pad[axis] = (0, D_FF_PAD - D_FF)
    return jnp.pad(w, pad)


def _state_to_tiles(re, im):
    shape = (re.shape[0], N_GROUP_BLOCKS, STATE_COLS // 2, LANES)
    return jnp.concatenate([re.reshape(shape), im.reshape(shape)], axis=2)


def _tiles_to_state(tiles):
    half = STATE_COLS // 2
    shape = (1, tiles.shape[0], N_SSM_GROUPS, SSM_STATE)
    return tiles[:, :, :half].reshape(shape), tiles[:, :, half:].reshape(shape)


def kernel(x_prompt, x_sample, cache_k, cache_v, state_ssm_re, state_ssm_im, state_conv, page_table, p_prompt, p_sample, rel_bias, norm_mix_g, w_in, lambda_q1, lambda_k1, lambda_q2, lambda_k2, attn_subln_g, ssm_a_re, ssm_a_im, ssm_b_re, ssm_b_im, ssm_c_re, ssm_c_im, ssm_d, ssm_log_dt, ssm_glu_w, ssm_glu_b, ssm_out_g, w_out, norm_ffn_g, ffn_w_up, ffn_conv_w, ffn_conv_b, ffn_w_down, norm_ple_g, ple_w_gate, ple_w_proj, norm_final_g):
    assert DEPTH == 1
    li = 0
    lam_init = 0.8 - 0.6 * math.exp(-0.3 * li)
    nb, t, _ = x_prompt.shape
    db, ds, _ = x_sample.shape
    assert db == SUBLANES
    m_p = nb * t
    m_s = db * ds

    row = lambda v: v.reshape(1, -1).astype(F32)
    w_in_b, w_kt_b = _cast_w_in(w_in[li])
    conv_w = _pad_ff(ffn_conv_w[li], 1)
    conv_b = _pad_ff(ffn_conv_b[li].reshape(1, D_FF), 1)
    w_proj_b = ple_w_proj[li].astype(BF16)
    lam_rows = jnp.stack([lambda_q1[li], lambda_k1[li], lambda_q2[li], lambda_k2[li]]).astype(F32)
    subln_g = attn_subln_g[li].astype(F32)
    wb, cb, tbl = _ssm_tables(ssm_a_re[li], ssm_a_im[li], ssm_b_re[li], ssm_b_im[li],
                              ssm_c_re[li], ssm_c_im[li], ssm_log_dt[li])
    glu_w = _block_diag(ssm_glu_w[li].reshape(N_GROUP_BLOCKS, GROUPS_PER_BLOCK, SSM_GROUP, SSM_GROUP)).astype(BF16)
    ssm_args = (wb, cb, tbl, row(ssm_d[li]), glu_w, row(ssm_glu_b[li]), row(ssm_out_g[li]))

    tmaj = lambda a: jnp.swapaxes(a, 0, 1).reshape(m_s, -1)
    bmaj = lambda a: jnp.swapaxes(a.reshape(ds, db, -1), 0, 1)
    xs = tmaj(x_sample)

    xp = x_prompt.reshape(m_p, D_MODEL)
    q_p, kt_p, v_p, u_p, w_out_b, w_gate_b, q_s, k_s, v_s, u_s = _in_proj(
        xp, row(norm_mix_g[li]), w_in_b, w_kt_b, tm=ROW_TILE, rows_per_seq=t, k_transposed=True,
        cast=(w_out[li], ple_w_gate[li]), extra=xs)
    w_out_b = w_out_b.reshape(2, ATT_W, D_MODEL)
    shp = lambda a: a.reshape(nb, t, -1)
    band = _prompt_bias_tiles(rel_bias.astype(F32), MAX_DISTANCE)
    o_att_p, w_ug_b, w_uv_b = _attn_prompt(shp(q_p), kt_p, shp(v_p), band, lam_rows,
                                           subln_g.reshape(N_HEADS, 1, V_HEAD), ffn_w_up[li],
                                           ATTN_TQ, ATTN_HEADS_PER_STEP, lam_init)
    zero_state = jnp.zeros((nb, N_GROUP_BLOCKS, 1, STATE_COLS, LANES), F32)
    o_ssm_p, hf_p = _ssm(shp(u_p), zero_state, *ssm_args, tc=SSM_ROWS, chains=1, out_dtype=BF16)
    h1_p, w_dn_b = _out_proj(xp, o_att_p.reshape(m_p, ATT_W), o_ssm_p.reshape(m_p, SSM_W), w_out_b,
                             tm=ROW_TILE, w_down=ffn_w_down[li])

    q_b = bmaj(q_s)
    q_rep = jnp.tile(q_b, (1, 2 * N_HEADS, 1)).reshape(db, 2 * N_HEADS, ds, ATT_W)
    sel = (jnp.arange(ATT_W)[None, :] // QK_DIM) == jnp.arange(2 * N_HEADS)[:, None]
    qbd = jnp.where(sel[None, :, None, :], q_rep, jnp.zeros_like(q_rep)).reshape(db, 2 * N_HEADS * ds, ATT_W)
    pad_rows = lambda a: jnp.pad(bmaj(a), ((0, 0), (0, PAGE_SIZE - ds), (0, 0)))
    bias_past, bias_new = _sample_bias(rel_bias.astype(F32), ds)
    n_pool = cache_k.shape[1]
    k_pool_t = jnp.transpose(cache_k[li], (0, 2, 3, 4, 1)).reshape(n_pool, ATT_W, PAGE_SIZE)
    k_new_t = jnp.swapaxes(pad_rows(k_s), 1, 2)
    v_pool_r = cache_v[li].reshape(n_pool, PAGE_SIZE * N_HEADS, V_HEAD)
    v_new_r = pad_rows(v_s).reshape(db, PAGE_SIZE * N_HEADS, V_HEAD)
    tm_ffn = ROW_TILE
    zero_prev = jnp.zeros((nb, CONV_HALO, D_FF_PAD), F32)
    attn_s = (page_table, qbd, k_pool_t, v_pool_r, bias_past, k_new_t, v_new_r, bias_new,
              lam_rows, subln_g, PAGES_PER_STEP, lam_init)
    h2_p, conv_p, o_att_s = _ffn(h1_p, row(norm_ffn_g[li]), zero_prev, w_ug_b, w_uv_b, conv_w, conv_b,
                                 w_dn_b, tm=tm_ffn, shift=1, tiles_per_seq=t // tm_ffn, attn=attn_s)
    h0_s = jnp.swapaxes(_state_to_tiles(state_ssm_re[li], state_ssm_im[li]), 0, 1)[None]
    o_ssm_s, hf_s = _ssm(u_s[None], h0_s, *ssm_args, tc=m_s, chains=db, out_dtype=F32)
    h1_s = _out_proj(xs, tmaj(o_att_s), o_ssm_s[0], w_out_b, tm=m_s)
    prev_s = _pad_ff(jnp.swapaxes(state_conv[li], 0, 1).reshape(1, 2 * db, D_FF), 2)
    ffn_s = (h1_s, row(norm_ffn_g[li]), prev_s, w_ug_b, w_uv_b, conv_w, conv_b, w_dn_b, db)
    y_p, h2_s, conv_s = _ple(h2_p, row(norm_ple_g[li]), p_prompt[li].reshape(m_p, PLE_DIM), w_gate_b,
                             w_proj_b, row(norm_final_g), tm=ROW_TILE, side_ffn=ffn_s)
    y_s = _ple(h2_s, row(norm_ple_g[li]), tmaj(p_sample[li]), w_gate_b, w_proj_b,
               row(norm_final_g), tm=m_s)

    re_p, im_p = _tiles_to_state(hf_p[:, :, 0])
    re_s, im_s = _tiles_to_state(jnp.swapaxes(hf_s[0], 0, 1))
    tiles = t // tm_ffn
    conv_p = conv_p[tiles - 1::tiles][None, :, :, :D_FF]
    conv_s = jnp.swapaxes(conv_s.reshape(2, db, D_FF_PAD), 0, 1)[None, :, :, :D_FF]
    return (y_p.reshape(nb, t, D_MODEL),
            bmaj(y_s),
            jnp.transpose(kt_p.reshape(1, nb, N_HEADS, 2, QK_DIM, t), (0, 1, 5, 2, 3, 4)),
            v_p.reshape(1, nb, t, N_HEADS, V_HEAD),
            re_p, im_p, conv_p,
            bmaj(k_s).reshape(1, db, ds, N_HEADS, 2, QK_DIM),
            bmaj(v_s).reshape(1, db, ds, N_HEADS, V_HEAD),
            re_s, im_s, conv_s)
```

```python
import functools
import math

import jax
import jax.numpy as jnp
from jax import lax
from jax.experimental import pallas as pl
from jax.experimental.pallas import tpu as pltpu

F32 = jnp.float32
BF16 = jnp.bfloat16

D_MODEL = 2048
DEPTH = 1
PAGE_SIZE = 128
ATT_W = 1024
SSM_W = 1024
V_HEAD = 128
N_HEADS = 8
QK_DIM = 64
SSM_GROUP = 16
N_SSM_GROUPS = 64
SSM_STATE = 64
D_FF = 5504
CONV_W = 3
PLE_DIM = 256
N_BUCKETS = 32
MAX_DISTANCE = 128
NORM_EPS = 1e-6
SUBLN_EPS = 1e-5
LOG2E = math.log2(math.e)
QK_SCALE = QK_DIM ** -0.5 * LOG2E

LANES = 128
SUBLANES = 8
MXU_WIDTH = 256
VMEM_LIMIT = 56 * 1024 * 1024

GROUPS_PER_BLOCK = LANES // SSM_GROUP
N_GROUP_BLOCKS = N_SSM_GROUPS // GROUPS_PER_BLOCK
STATE_LANES = GROUPS_PER_BLOCK * SSM_STATE

D_FF_PAD = 5632
FFN_TF = 512
ROW_TILE = 512
ATTN_TQ = 512
ATTN_HEADS_PER_STEP = 2
SSM_ROWS = 256
PAGES_PER_STEP = 8
CONV_HALO = 16


def _params(sem):
    return pltpu.CompilerParams(dimension_semantics=sem, vmem_limit_bytes=VMEM_LIMIT)


def _rms(x, g, eps):
    return x * lax.rsqrt(jnp.mean(x * x, axis=-1, keepdims=True) + eps) * g


def _in_proj_kernel(x_ref, g_ref, w_ref, wkt_ref, *refs, k_transposed, n_cast, has_extra):
    n_in = n_cast + has_extra
    outs = refs[n_in:n_in + 4]
    for src, dst in zip(refs[:n_cast], refs[n_in + 4:n_in + 4 + n_cast]):
        dst[...] = src[...].astype(BF16)

    def project(x, q_ref, k_ref, v_ref, u_ref, transposed):
        xn = _rms(x, g_ref[...], NORM_EPS).astype(BF16)

        def proj(n):
            return jnp.dot(xn, w_ref[:, n * ATT_W:(n + 1) * ATT_W], preferred_element_type=F32)

        q_ref[...] = (proj(0) * QK_SCALE).astype(BF16)
        if transposed:
            k_ref[0] = lax.dot_general(wkt_ref[...], xn, (((1,), (1,)), ((), ())),
                                       preferred_element_type=F32)
        else:
            k_ref[...] = proj(1)
        v_ref[...] = proj(2)
        u_ref[...] = proj(3)

    project(x_ref[...], *outs, k_transposed)
    if has_extra:
        @pl.when(pl.program_id(0) == 0)
        def _():
            project(refs[n_cast][...], *refs[n_in + 4 + n_cast:], False)


def _resident(shape):
    return pl.BlockSpec(shape, lambda *_: (0,) * len(shape), pipeline_mode=pl.Buffered(1))


def _in_proj(x, g, w, wkt, tm, rows_per_seq, k_transposed, cast=(), extra=None):
    m = x.shape[0]
    steps = m // tm
    extra_in, extra_specs, extra_shapes = [], [], []
    if extra is not None:
        me = extra.shape[0]
        extra_in = [_resident((me, D_MODEL))]
        extra_specs = [pl.BlockSpec((me, ATT_W), lambda i: (0, 0))] * 4
        extra_shapes = [jax.ShapeDtypeStruct((me, ATT_W), BF16)] + [jax.ShapeDtypeStruct((me, ATT_W), F32)] * 3
    cast_specs = [pl.BlockSpec((c.shape[0] // steps, c.shape[1]), lambda i: (i, 0)) for c in cast]
    assert all(c.shape[0] % (steps * 2 * SUBLANES) == 0 for c in cast)
    out_spec = pl.BlockSpec((tm, ATT_W), lambda i: (i, 0))
    if k_transposed:
        tiles = rows_per_seq // tm
        k_spec = pl.BlockSpec((1, ATT_W, tm), lambda i: (i // tiles, 0, i % tiles))
        k_shape = jax.ShapeDtypeStruct((m // rows_per_seq, ATT_W, rows_per_seq), F32)
    else:
        k_spec = out_spec
        k_shape = jax.ShapeDtypeStruct((m, ATT_W), F32)
    return pl.pallas_call(
        functools.partial(_in_proj_kernel, k_transposed=k_transposed, n_cast=len(cast),
                          has_extra=int(extra is not None)),
        grid=(steps,),
        in_specs=[
            pl.BlockSpec((tm, D_MODEL), lambda i: (i, 0)),
            _resident((1, D_MODEL)),
            _resident((D_MODEL, 4 * ATT_W)),
            _resident((ATT_W, D_MODEL)),
        ] + cast_specs + extra_in,
        out_specs=[out_spec, k_spec, out_spec, out_spec] + cast_specs + extra_specs,
        out_shape=[
            jax.ShapeDtypeStruct((m, ATT_W), BF16),
            k_shape,
            jax.ShapeDtypeStruct((m, ATT_W), F32),
            jax.ShapeDtypeStruct((m, SSM_W), F32),
        ] + [jax.ShapeDtypeStruct(c.shape, BF16) for c in cast] + extra_shapes,
        compiler_params=_params(("arbitrary",)),
        name="in_proj",
    )(x, g, w, wkt, *cast, *([] if extra is None else [extra]))


def _lambda(lam_ref, lam_init):
    s1 = jnp.sum(lam_ref[0:1, :] * lam_ref[1:2, :], axis=-1, keepdims=True)
    s2 = jnp.sum(lam_ref[2:3, :] * lam_ref[3:4, :], axis=-1, keepdims=True)
    return jnp.exp(s1) - jnp.exp(s2) + lam_init


def _online_softmax_step(s, v_blk, m_ref, l_ref, acc_ref):
    m_prev = m_ref[...]
    m_new = jnp.maximum(m_prev, jnp.max(s, axis=-1, keepdims=True))
    alpha = jnp.exp2(m_prev - m_new)
    p = jnp.exp2(s - jnp.concatenate([m_new] * (s.shape[1] // LANES), axis=1))
    l_ref[...] = alpha * l_ref[...] + jnp.sum(p, axis=-1, keepdims=True)
    acc_ref[...] = alpha * acc_ref[...] + jnp.dot(
        p.astype(BF16), v_blk, preferred_element_type=F32)
    m_ref[...] = m_new


def _attn_prompt_kernel(lam_ref, q_ref, k_ref, v_ref, band_ref, g_ref, wup_ref, o_ref, wg_ref, wv_ref,
                        kb_ref, vb_ref, qs_ref, bias_ref, m_ref, l_ref, acc_ref, *, tq, nh, lam_init):
    _cast_up_tile(wup_ref, wg_ref, wv_ref)
    i = pl.program_id(2)
    head_cols = [slice(h * V_HEAD, (h + 1) * V_HEAD) for h in range(nh)]

    @pl.when(i == 0)
    def _():
        kb_ref[...] = k_ref[0].astype(BF16)
        vb_ref[...] = v_ref[0].astype(BF16)
        w = band_ref.shape[1]
        zero = jnp.zeros((w, w), F32)
        masked = jnp.full((w, w), -jnp.inf, F32)
        n = tq // w
        for h in range(nh):
            left, right = band_ref[h, :, 0:w], band_ref[h, :, w:2 * w]
            for r in range(n):
                for c in range(n):
                    rows = slice(r * w, (r + 1) * w)
                    bias_ref[h, rows, c * w:(c + 1) * w] = left if (r == 0 and c == n - 1) else zero
                    bias_ref[h, rows, tq + c * w:tq + (c + 1) * w] = (
                        right if c == r else left if c == r - 1 else masked if c > r else zero)

    hq = tq // 2
    for h in range(nh):
        q = q_ref[0, :, head_cols[h]]
        lane = lax.broadcasted_iota(jnp.int32, q.shape, 1)
        zero = jnp.zeros_like(q)
        q1 = jnp.where(lane < QK_DIM, q, zero)
        q2 = jnp.where(lane >= QK_DIM, q, zero)
        for half in range(2):
            qs_ref[h, (2 * half) * hq:(2 * half + 1) * hq, :] = q1[half * hq:(half + 1) * hq]
            qs_ref[h, (2 * half + 1) * hq:(2 * half + 2) * hq, :] = q2[half * hq:(half + 1) * hq]
    m_ref[...] = jnp.full(m_ref.shape, -jnp.inf, F32)
    l_ref[...] = jnp.zeros(l_ref.shape, F32)
    acc_ref[...] = jnp.zeros(acc_ref.shape, F32)

    def update(j, width, biased, halves=(0, 1), bias_col=0):
        off = pl.multiple_of(j * tq, tq)
        n = len(halves)
        rows = slice(2 * hq * halves[0], 2 * hq * (halves[-1] + 1))
        for h in range(nh):
            krows = slice(h * 2 * QK_DIM, (h + 1) * 2 * QK_DIM)
            s = jnp.dot(qs_ref[h, rows, :], kb_ref[krows, pl.ds(off, width)], preferred_element_type=F32)
            if biased:
                bias = bias_ref[h, hq * halves[0]:hq * (halves[-1] + 1), bias_col:bias_col + width]
                s = (s.reshape(n, 2, hq, width) + bias.reshape(n, 1, hq, width)).reshape(2 * hq * n, width)
            _online_softmax_step(s, vb_ref[pl.ds(off, width), head_cols[h]],
                                 m_ref.at[h, rows], l_ref.at[h, rows], acc_ref.at[h, rows])

    n_far = jnp.maximum(i - 1, 0)

    def far_body(t, carry):
        update(2 * t, 2 * tq, False)
        return carry

    lax.fori_loop(0, n_far // 2, far_body, 0)

    @pl.when(n_far % 2 == 1)
    def _():
        update(n_far - 1, tq, False)

    @pl.when(i >= 1)
    def _():
        update(i - 1, tq + hq, True, halves=(0,))
        update(i - 1, 2 * tq, True, halves=(1,))

    @pl.when(i == 0)
    def _():
        update(0, hq, True, halves=(0,), bias_col=tq)
        update(0, tq, True, halves=(1,), bias_col=tq)

    lam = _lambda(lam_ref, lam_init)
    for h in range(nh):
        acc = acc_ref[h] * (1.0 / l_ref[h])
        o = jnp.concatenate([acc[(2 * half) * hq:(2 * half + 1) * hq]
                             - lam * acc[(2 * half + 1) * hq:(2 * half + 2) * hq] for half in range(2)], axis=0)
        o_ref[0, :, head_cols[h]] = (_rms(o, g_ref[h], SUBLN_EPS) * (1.0 - lam_init)).astype(o_ref.dtype)


def _attn_prompt(q, k, v, band, lam_rows, subln_g, w_up, tq, nh, lam_init):
    b, t, _ = q.shape
    w = band.shape[1]
    assert tq % w == 0 and band.shape[2] == 2 * w and N_HEADS % nh == 0
    kern = functools.partial(_attn_prompt_kernel, tq=tq, nh=nh, lam_init=lam_init)
    gw = nh * V_HEAD
    n_hg, n_i = N_HEADS // nh, t // tq
    wr = D_MODEL // (b * n_hg * n_i)
    assert wr * b * n_hg * n_i == D_MODEL and wr % (2 * SUBLANES) == 0
    w_index = lambda bi, h, i: ((bi * n_hg + h) * n_i + i, 0)
    w_out = jax.ShapeDtypeStruct((D_MODEL, D_FF_PAD), BF16)
    return pl.pallas_call(
        kern,
        grid=(b, n_hg, n_i),
        in_specs=[
            pl.BlockSpec((4, QK_DIM), lambda bi, h, i: (0, 0)),
            pl.BlockSpec((1, tq, gw), lambda bi, h, i: (bi, i, h)),
            pl.BlockSpec((1, gw, t), lambda bi, h, i: (bi, h, 0)),
            pl.BlockSpec((1, t, gw), lambda bi, h, i: (bi, 0, h)),
            pl.BlockSpec((nh, w, 2 * w), lambda bi, h, i: (h, 0, 0)),
            pl.BlockSpec((nh, 1, V_HEAD), lambda bi, h, i: (h, 0, 0)),
            pl.BlockSpec((wr, 2 * D_FF), w_index),
        ],
        out_specs=[pl.BlockSpec((1, tq, gw), lambda bi, h, i: (bi, i, h)),
                   pl.BlockSpec((wr, D_FF_PAD), w_index),
                   pl.BlockSpec((wr, D_FF_PAD), w_index)],
        out_shape=[jax.ShapeDtypeStruct((b, t, ATT_W), BF16), w_out, w_out],
        scratch_shapes=[
            pltpu.VMEM((gw, t), BF16),
            pltpu.VMEM((t, gw), BF16),
            pltpu.VMEM((nh, 2 * tq, V_HEAD), BF16),
            pltpu.VMEM((nh, tq, 2 * tq), F32),
            pltpu.VMEM((nh, 2 * tq, LANES), F32),
            pltpu.VMEM((nh, 2 * tq, LANES), F32),
            pltpu.VMEM((nh, 2 * tq, V_HEAD), F32),
        ],
        compiler_params=_params(("arbitrary", "arbitrary", "arbitrary")),
        name="attn_prompt",
    )(lam_rows, q, k, v, band, subln_g, w_up)


def _attn_sample_parts(step, n_steps, lam_ref, qbd_ref, k_refs, v_refs, bias_ref, kn_ref, vn_ref, bn_ref,
                       g_ref, o_ref, m_ref, l_ref, acc_ref, *, n_new, lam_init):
    def init():
        @pl.when(step == 0)
        def _():
            m_ref[...] = jnp.full(m_ref.shape, -jnp.inf, F32)
            l_ref[...] = jnp.zeros(l_ref.shape, F32)
            acc_ref[...] = jnp.zeros(acc_ref.shape, F32)

    qbd = qbd_ref[0]
    rows = 2 * n_new

    def update(k_pages, v_page_refs, bias_last):
        parts = [jnp.dot(qbd, kp.astype(BF16), preferred_element_type=F32) for kp in k_pages]
        parts[-1] = parts[-1] + bias_last
        s = jnp.concatenate(parts, axis=1)
        m_prev = m_ref[...]
        m_new = jnp.maximum(m_prev, jnp.max(s, axis=-1, keepdims=True))
        alpha = jnp.exp2(m_prev - m_new)
        p = jnp.exp2(s - jnp.concatenate([m_new] * len(parts), axis=1))
        l_ref[...] = alpha * l_ref[...] + jnp.sum(p, axis=-1, keepdims=True)
        m_ref[...] = m_new
        pb = p.astype(BF16)
        for h in range(N_HEADS):
            v_h = jnp.concatenate(
                [vr[0, pl.ds(h, PAGE_SIZE, stride=N_HEADS), :].astype(BF16) for vr in v_page_refs], axis=0)
            r = slice(h * rows, (h + 1) * rows)
            acc_ref[r, :] = alpha[r] * acc_ref[r, :] + jnp.dot(pb[r], v_h, preferred_element_type=F32)

    is_last = step == n_steps - 1

    def main():
        bias_last = jnp.where(is_last, bias_ref[...], jnp.zeros(bias_ref.shape, F32))
        update([kr[0] for kr in k_refs], v_refs, bias_last)

    def finish():
        @pl.when(is_last)
        def _():
            update([kn_ref[0]], [vn_ref], bn_ref[...])
            lam = _lambda(lam_ref, lam_init)
            acc = acc_ref[...] * (1.0 / l_ref[...])
            for h in range(N_HEADS):
                blk = acc[h * rows:(h + 1) * rows]
                o = blk[0:n_new] - lam * blk[n_new:rows]
                o = _rms(o, g_ref[h:h + 1, :], SUBLN_EPS) * (1.0 - lam_init)
                o_ref[0, :, h * V_HEAD:(h + 1) * V_HEAD] = o

    return init, main, finish


STATE_COLS = 2 * STATE_LANES // LANES


def _state_rows(g, tc, col):
    return pl.ds(g * tc * STATE_COLS + col, tc, stride=STATE_COLS)


def _ssm_kernel(u_ref, h0_ref, wb_ref, cb_ref, lam_ref, d_ref, gw_ref, gbias_ref, og_ref,
                o_ref, hf_ref, hb_ref, gbuf_ref, carry_ref, *, tc, chains):
    c = pl.program_id(1)

    @pl.when(c == 0)
    def _():
        carry_ref[...] = h0_ref[0]

    for g in range(N_GROUP_BLOCKS):
        u = u_ref[0, :, g * LANES:(g + 1) * LANES]
        bu = jnp.dot(u.astype(BF16), wb_ref[g], preferred_element_type=F32)
        for k in range(STATE_COLS):
            hb_ref[_state_rows(g, tc, k), :] = bu[:, k * LANES:(k + 1) * LANES]

    coef = [(lam_ref[g, 0], lam_ref[g, 1]) for g in range(N_GROUP_BLOCKS)]

    def step(g, row, h):
        start = (g * tc + row) * STATE_COLS
        if not isinstance(start, int):
            start = pl.multiple_of(start, STATE_COLS)
        rows = pl.ds(start, STATE_COLS)
        a, b = coef[g]
        h = a * h + b * pltpu.roll(h, STATE_COLS // 2, 0) + hb_ref[rows, :]
        hb_ref[rows, :] = h
        return h

    if chains == 1:
        def body(t, hs):
            return tuple(step(g, t, hs[g]) for g in range(N_GROUP_BLOCKS))

        hs = lax.fori_loop(0, tc, body, tuple(carry_ref[g, 0] for g in range(N_GROUP_BLOCKS)),
                           unroll=SUBLANES)
        for g in range(N_GROUP_BLOCKS):
            carry_ref[g, 0] = hs[g]
    else:
        for g in range(N_GROUP_BLOCKS):
            hs = [carry_ref[g, b] for b in range(chains)]
            for row in range(tc):
                hs[row % chains] = step(g, row, hs[row % chains])
            for b in range(chains):
                carry_ref[g, b] = hs[b]

    ssq = jnp.zeros((tc, 1), F32)
    for g in range(N_GROUP_BLOCKS):
        cols = slice(g * LANES, (g + 1) * LANES)
        h = jnp.concatenate([hb_ref[_state_rows(g, tc, k), :] for k in range(STATE_COLS)], axis=1)
        y = jnp.dot(h.astype(BF16), cb_ref[g], preferred_element_type=F32)
        y = y + d_ref[:, cols] * u_ref[0, :, cols]
        a = jax.nn.gelu(y)
        gate = jnp.dot(a.astype(BF16), gw_ref[g], preferred_element_type=F32) + gbias_ref[:, cols]
        a = a * jax.nn.sigmoid(gate)
        gbuf_ref[:, cols] = a
        ssq = ssq + jnp.sum(a * a, axis=-1, keepdims=True)
    inv = lax.rsqrt(ssq * (1.0 / SSM_W) + NORM_EPS)
    o_ref[0] = (gbuf_ref[...] * inv * og_ref[...]).astype(o_ref.dtype)
    hf_ref[0] = carry_ref[...]


def _ssm(u, h0, wb, cb, lam, d, gw, gbias, og, tc, chains, out_dtype):
    ns, t, _ = u.shape
    kern = functools.partial(_ssm_kernel, tc=tc, chains=chains)
    full = lambda *shape: pl.BlockSpec(shape, lambda s, c: (0,) * len(shape))
    state_shape = (N_GROUP_BLOCKS, chains, STATE_COLS, LANES)
    state_spec = pl.BlockSpec((1,) + state_shape, lambda s, c: (s, 0, 0, 0, 0))
    return pl.pallas_call(
        kern,
        grid=(ns, t // tc),
        in_specs=[
            pl.BlockSpec((1, tc, SSM_W), lambda s, c: (s, c, 0)),
            state_spec,
            full(N_GROUP_BLOCKS, LANES, 2 * STATE_LANES),
            full(N_GROUP_BLOCKS, 2 * STATE_LANES, LANES),
            full(N_GROUP_BLOCKS, 2, STATE_COLS, LANES),
            full(1, SSM_W),
            full(N_GROUP_BLOCKS, LANES, LANES),
            full(1, SSM_W),
            full(1, SSM_W),
        ],
        out_specs=[pl.BlockSpec((1, tc, SSM_W), lambda s, c: (s, c, 0)), state_spec],
        out_shape=[
            jax.ShapeDtypeStruct((ns, t, SSM_W), out_dtype),
            jax.ShapeDtypeStruct((ns,) + state_shape, F32),
        ],
        scratch_shapes=[
            pltpu.VMEM((N_GROUP_BLOCKS * tc * STATE_COLS, LANES), F32),
            pltpu.VMEM((tc, SSM_W), F32),
            pltpu.VMEM(state_shape, F32),
        ],
        compiler_params=_params(("arbitrary", "arbitrary")),
        name="ssm_chains%d" % chains,
    )(u, h0, wb, cb, lam, d, gw, gbias, og)


def _out_proj_kernel(x_ref, oa_ref, os_ref, w_ref, *refs):
    o_ref = refs[-1] if len(refs) == 1 else refs[1]
    acc = jnp.dot(oa_ref[...].astype(BF16), w_ref[0], preferred_element_type=F32)
    acc = acc + jnp.dot(os_ref[...].astype(BF16), w_ref[1], preferred_element_type=F32)
    o_ref[...] = x_ref[...] + acc
    if len(refs) == 3:
        _cast_down_tile(refs[0], refs[2])


def _out_proj(x, o_att, o_ssm, w, tm, w_down=None):
    m = x.shape[0]
    steps = m // tm
    in_specs = [
        pl.BlockSpec((tm, D_MODEL), lambda i: (i, 0)),
        pl.BlockSpec((tm, ATT_W), lambda i: (i, 0)),
        pl.BlockSpec((tm, SSM_W), lambda i: (i, 0)),
        pl.BlockSpec((2, ATT_W, D_MODEL), lambda i: (0, 0, 0)),
    ]
    out_specs = [pl.BlockSpec((tm, D_MODEL), lambda i: (i, 0))]
    out_shape = [jax.ShapeDtypeStruct((m, D_MODEL), F32)]
    args = [x, o_att, o_ssm, w]
    if w_down is not None:
        tc = D_MODEL // steps
        in_specs.append(pl.BlockSpec((D_FF, tc), lambda i: (0, i)))
        out_specs.append(pl.BlockSpec((D_FF_PAD, tc), lambda i: (0, i)))
        out_shape.append(jax.ShapeDtypeStruct((D_FF_PAD, D_MODEL), BF16))
        args.append(w_down)
    out = pl.pallas_call(
        _out_proj_kernel,
        grid=(steps,),
        in_specs=in_specs,
        out_specs=out_specs,
        out_shape=out_shape,
        compiler_params=_params(("arbitrary",)),
        name="out_proj",
    )(*args)
    return out if w_down is not None else out[0]


N_FFN_IN = 8


def _ffn_parts(i, f, h_ref, g_ref, prev_ref, wg_ref, wv_ref, cw_ref, cb_ref, wdn_ref, o_ref, cn_ref,
               fn_ref, acc_ref, halo_ref, *, tm, shift, tiles_per_seq, nf):
    def prologue():
        @pl.when(f == 0)
        def _():
            fn_ref[...] = _rms(h_ref[...], g_ref[...], NORM_EPS).astype(BF16)
            acc_ref[...] = jnp.zeros(acc_ref.shape, F32)

        @pl.when((i % tiles_per_seq) == 0)
        def _():
            halo_ref[f] = prev_ref[0]

    def main():
        fn = fn_ref[...]
        down = None
        for c0 in range(0, FFN_TF, MXU_WIDTH):
            cols = slice(c0, c0 + MXU_WIDTH)
            gate = jnp.dot(fn, wg_ref[:, cols], preferred_element_type=F32)
            val = jnp.dot(fn, wv_ref[:, cols], preferred_element_type=F32)
            gext = jnp.concatenate([halo_ref[f, :, cols], gate], axis=0)
            g1 = gext[CONV_HALO - shift:CONV_HALO - shift + tm]
            g2 = gext[CONV_HALO - 2 * shift:CONV_HALO - 2 * shift + tm]
            conv = cb_ref[:, cols] + (cw_ref[0:1, cols] * g2 + cw_ref[1:2, cols] * g1 + cw_ref[2:3, cols] * gate)
            hact = (conv * jax.nn.sigmoid(conv)) * val
            part = jnp.dot(hact.astype(BF16), wdn_ref[cols, :], preferred_element_type=F32)
            down = part if down is None else down + part
            halo_ref[f, :, cols] = gate[tm - CONV_HALO:tm]
            cn_ref[0, :, cols] = gate[tm - 2 * shift:tm]
        acc_ref[...] += down

    def epilogue():
        @pl.when(f == nf - 1)
        def _():
            o_ref[...] = h_ref[...] + acc_ref[...]

    return prologue, main, epilogue


def _ffn_kernel(*refs, **static):
    prologue, main, epilogue = _ffn_parts(pl.program_id(0), pl.program_id(1), *refs, **static)
    prologue()
    main()
    epilogue()


def _ffn_attn_kernel(pt_ref, *refs, n_pages_step, steps_per_seq, n_seq, n_new, lam_init, nf, **static):
    del pt_ref
    g = n_pages_step
    ffn_in = refs[:N_FFN_IN]
    lam_ref, qbd_ref = refs[N_FFN_IN:N_FFN_IN + 2]
    k_refs = refs[N_FFN_IN + 2:N_FFN_IN + 2 + g]
    v_refs = refs[N_FFN_IN + 2 + g:N_FFN_IN + 2 + 2 * g]
    rest = refs[N_FFN_IN + 2 + 2 * g:]
    bias_ref, kn_ref, vn_ref, bn_ref, sg_ref, o_ref, cn_ref, oa_ref = rest[:8]
    fn_ref, acc_ref, halo_ref, am_ref, al_ref, aacc_ref = rest[8:]
    i = pl.program_id(0)
    f = pl.program_id(1)
    s = i * nf + f
    active = s < n_seq * steps_per_seq
    prologue, ffn_main, epilogue = _ffn_parts(i, f, *ffn_in, o_ref, cn_ref, fn_ref, acc_ref, halo_ref,
                                              nf=nf, **static)
    init, attn_main, finish = _attn_sample_parts(
        s % steps_per_seq, steps_per_seq, lam_ref, qbd_ref, k_refs, v_refs, bias_ref, kn_ref, vn_ref,
        bn_ref, sg_ref, oa_ref, am_ref, al_ref, aacc_ref, n_new=n_new, lam_init=lam_init)
    prologue()

    @pl.when(active)
    def _():
        init()

    @pl.when(active)
    def _():
        ffn_main()
        attn_main()

    @pl.when(jnp.logical_not(active))
    def _():
        ffn_main()

    epilogue()

    @pl.when(active)
    def _():
        finish()


def _ffn_specs(m, tm, shift, tiles_per_seq, where):
    nf = D_FF_PAD // FFN_TF

    def spec(shape, index):
        return pl.BlockSpec(shape, lambda *ids: index(*where(*ids)))

    in_specs = [
        spec((tm, D_MODEL), lambda i, f: (i, 0)),
        spec((1, D_MODEL), lambda i, f: (0, 0)),
        spec((1, CONV_HALO, FFN_TF), lambda i, f: (i // tiles_per_seq, 0, f)),
        spec((D_MODEL, FFN_TF), lambda i, f: (0, f)),
        spec((D_MODEL, FFN_TF), lambda i, f: (0, f)),
        spec((CONV_W, FFN_TF), lambda i, f: (0, f)),
        spec((1, FFN_TF), lambda i, f: (0, f)),
        spec((FFN_TF, D_MODEL), lambda i, f: (f, 0)),
    ]
    assert len(in_specs) == N_FFN_IN
    out_specs = [
        spec((tm, D_MODEL), lambda i, f: (i, 0)),
        spec((1, 2 * shift, FFN_TF), lambda i, f: (i, 0, f)),
    ]
    out_shape = [
        jax.ShapeDtypeStruct((m, D_MODEL), F32),
        jax.ShapeDtypeStruct((m // tm, 2 * shift, D_FF_PAD), F32),
    ]
    scratch = [
        pltpu.VMEM((tm, D_MODEL), BF16),
        pltpu.VMEM((tm, D_MODEL), F32),
        pltpu.VMEM((nf, CONV_HALO, FFN_TF), F32),
    ]
    return in_specs, out_specs, out_shape, scratch


def _ffn(h, g, prev, wg, wv, cw, cb, wdn, tm, shift, tiles_per_seq, attn=None):
    m = h.shape[0]
    nf = D_FF_PAD // FFN_TF
    static = dict(tm=tm, shift=shift, tiles_per_seq=tiles_per_seq, nf=nf)
    in_specs, out_specs, out_shape, scratch = _ffn_specs(m, tm, shift, tiles_per_seq,
                                                         lambda i, f, *_: (i, f))
    args = (h, g, prev, wg, wv, cw, cb, wdn)
    grid = (m // tm, nf)
    if attn is None:
        return pl.pallas_call(
            functools.partial(_ffn_kernel, **static),
            grid=grid, in_specs=in_specs, out_specs=out_specs, out_shape=out_shape,
            scratch_shapes=scratch,
            compiler_params=_params(("arbitrary", "arbitrary")),
            name="conv_ffn",
        )(*args)

    (page_table, qbd, k_pool, v_pool, bias_last, k_new, v_new, bias_new, lam_rows, subln_g,
     n_pages_step, lam_init) = attn
    n_seq, n_pages = page_table.shape
    rows = qbd.shape[1]
    n_new = rows // (2 * N_HEADS)
    steps_per_seq = n_pages // n_pages_step
    last = n_seq * steps_per_seq - 1
    assert grid[0] * grid[1] > last

    def seq_of(i, f):
        return jnp.minimum(i * nf + f, last) // steps_per_seq

    def page_spec(p, shape):
        def index(i, f, pt):
            s = jnp.minimum(i * nf + f, last)
            return (pt[s // steps_per_seq, (s % steps_per_seq) * n_pages_step + p], 0, 0)
        return pl.BlockSpec((1,) + shape, index)

    in_specs += (
        [pl.BlockSpec((4, QK_DIM), lambda i, f, pt: (0, 0)),
         pl.BlockSpec((1, rows, ATT_W), lambda i, f, pt: (seq_of(i, f), 0, 0))]
        + [page_spec(p, (ATT_W, PAGE_SIZE)) for p in range(n_pages_step)]
        + [page_spec(p, (PAGE_SIZE * N_HEADS, V_HEAD)) for p in range(n_pages_step)]
        + [pl.BlockSpec((rows, PAGE_SIZE), lambda i, f, pt: (0, 0)),
           pl.BlockSpec((1, ATT_W, PAGE_SIZE), lambda i, f, pt: (seq_of(i, f), 0, 0)),
           pl.BlockSpec((1, PAGE_SIZE * N_HEADS, V_HEAD), lambda i, f, pt: (seq_of(i, f), 0, 0)),
           pl.BlockSpec((rows, PAGE_SIZE), lambda i, f, pt: (0, 0)),
           pl.BlockSpec((N_HEADS, V_HEAD), lambda i, f, pt: (0, 0))])
    out_specs.append(pl.BlockSpec((1, n_new, ATT_W), lambda i, f, pt: (seq_of(i, f), 0, 0)))
    out_shape.append(jax.ShapeDtypeStruct((n_seq, n_new, ATT_W), F32))
    scratch += [pltpu.VMEM((rows, LANES), F32), pltpu.VMEM((rows, LANES), F32),
                pltpu.VMEM((rows, V_HEAD), F32)]
    kern = functools.partial(_ffn_attn_kernel, n_pages_step=n_pages_step, steps_per_seq=steps_per_seq,
                             n_seq=n_seq, n_new=n_new, lam_init=lam_init, **static)
    return pl.pallas_call(
        kern,
        grid_spec=pltpu.PrefetchScalarGridSpec(
            num_scalar_prefetch=1, grid=grid, in_specs=in_specs, out_specs=out_specs,
            scratch_shapes=scratch),
        out_shape=out_shape,
        compiler_params=_params(("arbitrary", "arbitrary")),
        name="conv_ffn_attn",
    )(page_table, *args, lam_rows, qbd, *([k_pool] * n_pages_step), *([v_pool] * n_pages_step),
      bias_last, k_new, v_new, bias_new, subln_g)


N_PLE_IN = 6


def _ple_kernel(h_ref, gp_ref, pe_ref, wg_ref, wp_ref, gf_ref, *refs, side_ffn):
    if side_ffn is not None:
        step = pl.program_id(0)
        ffn_in, (o_ref, o2_ref, cn_ref), scratch = refs[:N_FFN_IN], refs[N_FFN_IN:N_FFN_IN + 3], refs[N_FFN_IN + 3:]
        prologue, main, epilogue = _ffn_parts(step * 0, step, *ffn_in, o2_ref, cn_ref, *scratch, **side_ffn)

        @pl.when(step < side_ffn["nf"])
        def _():
            prologue()
            main()
            epilogue()
    else:
        (o_ref,) = refs
    h = h_ref[...]
    hn = _rms(h, gp_ref[...], NORM_EPS).astype(BF16)
    gate = jax.nn.sigmoid(jnp.dot(hn, wg_ref[...], preferred_element_type=F32))
    proj = jnp.dot(pe_ref[...].astype(BF16), wp_ref[...], preferred_element_type=F32)
    o_ref[...] = _rms(h + gate * proj, gf_ref[...], NORM_EPS)


def _ple(h, gp, pe, wg, wp, gf, tm, side_ffn=None):
    m = h.shape[0]
    steps = m // tm
    in_specs = [
        pl.BlockSpec((tm, D_MODEL), lambda i: (i, 0)),
        pl.BlockSpec((1, D_MODEL), lambda i: (0, 0)),
        pl.BlockSpec((tm, PLE_DIM), lambda i: (i, 0)),
        pl.BlockSpec((D_MODEL, D_MODEL), lambda i: (0, 0)),
        pl.BlockSpec((PLE_DIM, D_MODEL), lambda i: (0, 0)),
        pl.BlockSpec((1, D_MODEL), lambda i: (0, 0)),
    ]
    assert len(in_specs) == N_PLE_IN
    out_specs = [pl.BlockSpec((tm, D_MODEL), lambda i: (i, 0))]
    out_shape = [jax.ShapeDtypeStruct((m, D_MODEL), F32)]
    args, scratch, static = [h, gp, pe, wg, wp, gf], [], None
    if side_ffn is not None:
        *ffn_args, shift = side_ffn
        ms = ffn_args[0].shape[0]
        nf = D_FF_PAD // FFN_TF
        assert steps >= nf
        ffn_in, ffn_out, ffn_shape, scratch = _ffn_specs(
            ms, ms, shift, 1, lambda i: (0, jnp.minimum(i, nf - 1)))
        in_specs, out_specs, out_shape = in_specs + ffn_in, out_specs + ffn_out, out_shape + ffn_shape
        args += ffn_args
        static = dict(tm=ms, shift=shift, tiles_per_seq=1, nf=nf)
    out = pl.pallas_call(
        functools.partial(_ple_kernel, side_ffn=static),
        grid=(steps,),
        in_specs=in_specs,
        out_specs=out_specs,
        out_shape=out_shape,
        scratch_shapes=scratch,
        compiler_params=_params(("arbitrary",)),
        name="ple_final",
    )(*args)
    return out if side_ffn is not None else out[0]


def _t5_bucket(dist):
    max_exact = N_BUCKETS // 2
    n = jnp.maximum(dist, 0)
    nf = jnp.maximum(n, 1).astype(F32)
    large = max_exact + (jnp.log(nf / max_exact) / math.log(MAX_DISTANCE / max_exact)
                         * (N_BUCKETS - max_exact)).astype(jnp.int32)
    large = jnp.minimum(large, N_BUCKETS - 1)
    return jnp.where(n < max_exact, n, large)


def _toeplitz(w, n):
    lead = w.shape[:-1]
    flat = jnp.tile(w, (1,) * len(lead) + (n,))[..., :n * (2 * n - 1)]
    return flat.reshape(lead + (n, 2 * n - 1))[..., :n]


def _prompt_bias_tiles(table, n):
    assert n >= MAX_DISTANCE
    far = table[N_BUCKETS - 1]
    bvec = (table[_t5_bucket(jnp.arange(2 * n, dtype=jnp.int32))] - far[None, :]).T
    bvec = bvec * LOG2E
    d = jnp.arange(2 * n, dtype=jnp.int32)
    d = jnp.where(d >= n, d - 2 * n, d)
    diag = jnp.where(d[None, :] <= 0, bvec[:, jnp.abs(d)], -jnp.inf)
    sub = bvec[:, jnp.clip(n - d, 0, 2 * n - 1)]
    return jnp.concatenate([_toeplitz(sub, n), _toeplitz(diag, n)], axis=-1)


def _sample_bias(table, n_new):
    assert PAGE_SIZE >= MAX_DISTANCE
    far = table[N_BUCKETS - 1]
    t = jnp.arange(n_new, dtype=jnp.int32)
    col = jnp.arange(PAGE_SIZE, dtype=jnp.int32)

    def rows(b):
        b = jnp.transpose(b * LOG2E, (2, 0, 1))
        return jnp.broadcast_to(b[:, None], (N_HEADS, 2, n_new, PAGE_SIZE)).reshape(-1, PAGE_SIZE)

    dist_last = PAGE_SIZE + t[:, None] - col[None, :]
    dist_new = t[:, None] - col[None, :]
    b_new = jnp.where((dist_new >= 0)[..., None], table[_t5_bucket(dist_new)] - far, -jnp.inf)
    return rows(table[_t5_bucket(dist_last)] - far), rows(b_new)


def _block_diag(w):
    nb, gp, a, b = w.shape
    eye = jnp.eye(gp, dtype=w.dtype)
    out = w[:, :, :, None, :] * eye[None, :, None, :, None]
    return out.reshape(nb, gp * a, gp * b)


def _ssm_tables(a_re, a_im, b_re, b_im, c_re, c_im, log_dt):
    g, n, gc = N_SSM_GROUPS, SSM_STATE, SSM_GROUP
    dt = jnp.exp(log_dt)[:, None]
    decay = jnp.exp(a_re * dt)
    ab_re = decay * jnp.cos(a_im * dt)
    ab_im = decay * jnp.sin(a_im * dt)
    den = a_re * a_re + a_im * a_im
    num_re = ab_re - 1.0
    coef_re = (num_re * a_re + ab_im * a_im) / den
    coef_im = (ab_im * a_re - num_re * a_im) / den
    bb_re = coef_re[..., None] * b_re - coef_im[..., None] * b_im
    bb_im = coef_re[..., None] * b_im + coef_im[..., None] * b_re
    blk = lambda w: w.reshape((N_GROUP_BLOCKS, GROUPS_PER_BLOCK) + w.shape[1:])
    to_cn = lambda w: jnp.swapaxes(blk(w), 2, 3)
    wb = jnp.concatenate([_block_diag(to_cn(bb_re)), _block_diag(to_cn(bb_im))], axis=2)
    to_nc = lambda w: jnp.swapaxes(blk(w), 2, 3)
    cb = jnp.concatenate([_block_diag(to_nc(c_re)), _block_diag(to_nc(-c_im))], axis=1)
    lr = ab_re.reshape(N_GROUP_BLOCKS, STATE_COLS // 2, LANES)
    li = ab_im.reshape(N_GROUP_BLOCKS, STATE_COLS // 2, LANES)
    lam = jnp.stack([jnp.concatenate([lr, lr], axis=1), jnp.concatenate([-li, li], axis=1)], axis=1)
    return wb.astype(BF16), cb.astype(BF16), lam.astype(F32)


def _cast_in_kernel(w_ref, wb_ref, wkt_ref):
    w = w_ref[...]
    wb_ref[...] = w.astype(BF16)
    wkt_ref[...] = w[:, ATT_W:2 * ATT_W].T.astype(BF16)


def _cast_w_in(w, tr=256):
    return pl.pallas_call(
        _cast_in_kernel,
        grid=(D_MODEL // tr,),
        in_specs=[pl.BlockSpec((tr, 4 * ATT_W), lambda i: (i, 0))],
        out_specs=[pl.BlockSpec((tr, 4 * ATT_W), lambda i: (i, 0)),
                   pl.BlockSpec((ATT_W, tr), lambda i: (0, i))],
        out_shape=[jax.ShapeDtypeStruct((D_MODEL, 4 * ATT_W), BF16),
                   jax.ShapeDtypeStruct((ATT_W, D_MODEL), BF16)],
        compiler_params=_params(("arbitrary",)),
        name="cast_w_in",
    )(w)


def _cast_up_tile(w_ref, wg_ref, wv_ref):
    zeros = jnp.zeros((wg_ref.shape[0], D_FF_PAD - D_FF), BF16)
    for n, o_ref in enumerate((wg_ref, wv_ref)):
        o_ref[:, 0:D_FF] = w_ref[:, n * D_FF:(n + 1) * D_FF].astype(BF16)
        o_ref[:, D_FF:D_FF_PAD] = zeros


def _cast_down_tile(w_ref, o_ref):
    o_ref[0:D_FF, :] = w_ref[...].astype(BF16)
    o_ref[D_FF:D_FF_PAD, :] = jnp.zeros((D_FF_PAD - D_FF, o_ref.shape[1]), BF16)


def _pad_ff(w, axis):
    pad = [(0, 0)] * w.ndim
    pad[axis] = (0, D_FF_PAD - D_FF)
    return jnp.pad(w, pad)


def _state_to_tiles(re, im):
    shape = (re.shape[0], N_GROUP_BLOCKS, STATE_COLS // 2, LANES)
    return jnp.concatenate([re.reshape(shape), im.reshape(shape)], axis=2)


def _tiles_to_state(tiles):
    half = STATE_COLS // 2
    shape = (1, tiles.shape[0], N_SSM_GROUPS, SSM_STATE)
    return tiles[:, :, :half].reshape(shape), tiles[:, :, half:].reshape(shape)


def kernel(x_prompt, x_sample, cache_k, cache_v, state_ssm_re, state_ssm_im, state_conv, page_table, p_prompt, p_sample, rel_bias, norm_mix_g, w_in, lambda_q1, lambda_k1, lambda_q2, lambda_k2, attn_subln_g, ssm_a_re, ssm_a_im, ssm_b_re, ssm_b_im, ssm_c_re, ssm_c_im, ssm_d, ssm_log_dt, ssm_glu_w, ssm_glu_b, ssm_out_g, w_out, norm_ffn_g, ffn_w_up, ffn_conv_w, ffn_conv_b, ffn_w_down, norm_ple_g, ple_w_gate, ple_w_proj, norm_final_g):
    assert DEPTH == 1
    li = 0
    lam_init = 0.8 - 0.6 * math.exp(-0.3 * li)
    nb, t, _ = x_prompt.shape
    db, ds, _ = x_sample.shape
    assert db == SUBLANES
    m_p = nb * t
    m_s = db * ds

    row = lambda v: v.reshape(1, -1).astype(F32)
    w_in_b, w_kt_b = _cast_w_in(w_in[li])
    conv_w = _pad_ff(ffn_conv_w[li], 1)
    conv_b = _pad_ff(ffn_conv_b[li].reshape(1, D_FF), 1)
    w_proj_b = ple_w_proj[li].astype(BF16)
    lam_rows = jnp.stack([lambda_q1[li], lambda_k1[li], lambda_q2[li], lambda_k2[li]]).astype(F32)
    subln_g = attn_subln_g[li].astype(F32)
    wb, cb, tbl = _ssm_tables(ssm_a_re[li], ssm_a_im[li], ssm_b_re[li], ssm_b_im[li],
                              ssm_c_re[li], ssm_c_im[li], ssm_log_dt[li])
    glu_w = _block_diag(ssm_glu_w[li].reshape(N_GROUP_BLOCKS, GROUPS_PER_BLOCK, SSM_GROUP, SSM_GROUP)).astype(BF16)
    ssm_args = (wb, cb, tbl, row(ssm_d[li]), glu_w, row(ssm_glu_b[li]), row(ssm_out_g[li]))

    tmaj = lambda a: jnp.swapaxes(a, 0, 1).reshape(m_s, -1)
    bmaj = lambda a: jnp.swapaxes(a.reshape(ds, db, -1), 0, 1)
    xs = tmaj(x_sample)

    xp = x_prompt.reshape(m_p, D_MODEL)
    q_p, kt_p, v_p, u_p, w_out_b, w_gate_b, q_s, k_s, v_s, u_s = _in_proj(
        xp, row(norm_mix_g[li]), w_in_b, w_kt_b, tm=ROW_TILE, rows_per_seq=t, k_transposed=True,
        cast=(w_out[li], ple_w_gate[li]), extra=xs)
    w_out_b = w_out_b.reshape(2, ATT_W, D_MODEL)
    shp = lambda a: a.reshape(nb, t, -1)
    band = _prompt_bias_tiles(rel_bias.astype(F32), MAX_DISTANCE)
    o_att_p, w_ug_b, w_uv_b = _attn_prompt(shp(q_p), kt_p, shp(v_p), band, lam_rows,
                                           subln_g.reshape(N_HEADS, 1, V_HEAD), ffn_w_up[li],
                                           ATTN_TQ, ATTN_HEADS_PER_STEP, lam_init)
    zero_state = jnp.zeros((nb, N_GROUP_BLOCKS, 1, STATE_COLS, LANES), F32)
    o_ssm_p, hf_p = _ssm(shp(u_p), zero_state, *ssm_args, tc=SSM_ROWS, chains=1, out_dtype=BF16)
    h1_p, w_dn_b = _out_proj(xp, o_att_p.reshape(m_p, ATT_W), o_ssm_p.reshape(m_p, SSM_W), w_out_b,
                             tm=ROW_TILE, w_down=ffn_w_down[li])

    q_b = bmaj(q_s)
    q_rep = jnp.tile(q_b, (1, 2 * N_HEADS, 1)).reshape(db, 2 * N_HEADS, ds, ATT_W)
    sel = (jnp.arange(ATT_W)[None, :] // QK_DIM) == jnp.arange(2 * N_HEADS)[:, None]
    qbd = jnp.where(sel[None, :, None, :], q_rep, jnp.zeros_like(q_rep)).reshape(db, 2 * N_HEADS * ds, ATT_W)
    pad_rows = lambda a: jnp.pad(bmaj(a), ((0, 0), (0, PAGE_SIZE - ds), (0, 0)))
    bias_past, bias_new = _sample_bias(rel_bias.astype(F32), ds)
    n_pool = cache_k.shape[1]
    k_pool_t = jnp.transpose(cache_k[li], (0, 2, 3, 4, 1)).reshape(n_pool, ATT_W, PAGE_SIZE)
    k_new_t = jnp.swapaxes(pad_rows(k_s), 1, 2)
    v_pool_r = cache_v[li].reshape(n_pool, PAGE_SIZE * N_HEADS, V_HEAD)
    v_new_r = pad_rows(v_s).reshape(db, PAGE_SIZE * N_HEADS, V_HEAD)
    tm_ffn = ROW_TILE
    zero_prev = jnp.zeros((nb, CONV_HALO, D_FF_PAD), F32)
    attn_s = (page_table, qbd, k_pool_t, v_pool_r, bias_past, k_new_t, v_new_r, bias_new,
              lam_rows, subln_g, PAGES_PER_STEP, lam_init)
    h2_p, conv_p, o_att_s = _ffn(h1_p, row(norm_ffn_g[li]), zero_prev, w_ug_b, w_uv_b, conv_w, conv_b,
                                 w_dn_b, tm=tm_ffn, shift=1, tiles_per_seq=t // tm_ffn, attn=attn_s)
    h0_s = jnp.swapaxes(_state_to_tiles(state_ssm_re[li], state_ssm_im[li]), 0, 1)[None]
    o_ssm_s, hf_s = _ssm(u_s[None], h0_s, *ssm_args, tc=m_s, chains=db, out_dtype=F32)
    h1_s = _out_proj(xs, tmaj(o_att_s), o_ssm_s[0], w_out_b, tm=m_s)
    prev_s = _pad_ff(jnp.swapaxes(state_conv[li], 0, 1).reshape(1, 2 * db, D_FF), 2)
    ffn_s = (h1_s, row(norm_ffn_g[li]), prev_s, w_ug_b, w_uv_b, conv_w, conv_b, w_dn_b, db)
    y_p, h2_s, conv_s = _ple(h2_p, row(norm_ple_g[li]), p_prompt[li].reshape(m_p, PLE_DIM), w_gate_b,
                             w_proj_b, row(norm_final_g), tm=ROW_TILE, side_ffn=ffn_s)
    y_s = _ple(h2_s, row(norm_ple_g[li]), tmaj(p_sample[li]), w_gate_b, w_proj_b,
               row(norm_final_g), tm=m_s)

    re_p, im_p = _tiles_to_state(hf_p[:, :, 0])
    re_s, im_s = _tiles_to_state(jnp.swapaxes(hf_s[0], 0, 1))
    tiles = t // tm_ffn
    conv_p = conv_p[tiles - 1::tiles][None, :, :, :D_FF]
    conv_s = jnp.swapaxes(conv_s.reshape(2, db, D_FF_PAD), 0, 1)[None, :, :, :D_FF]
    return (y_p.reshape(nb, t, D_MODEL),
            bmaj(y_s),
            jnp.transpose(kt_p.reshape(1, nb, N_HEADS, 2, QK_DIM, t), (0, 1, 5, 2, 3, 4)),
            v_p.reshape(1, nb, t, N_HEADS, V_HEAD),
            re_p, im_p, conv_p,
            bmaj(k_s).reshape(1, db, ds, N_HEADS, 2, QK_DIM),
            bmaj(v_s).reshape(1, db, ds, N_HEADS, V_HEAD),
            re_s, im_s, conv_s)
```

```python
import functools
import math

import jax
import jax.numpy as jnp
from jax import lax
from jax.experimental import pallas as pl
from jax.experimental.pallas import tpu as pltpu

F32 = jnp.float32
BF16 = jnp.bfloat16

D_MODEL = 2048
DEPTH = 1
PAGE_SIZE = 128
ATT_W = 1024
SSM_W = 1024
V_HEAD = 128
N_HEADS = 8
QK_DIM = 64
SSM_GROUP = 16
N_SSM_GROUPS = 64
SSM_STATE = 64
D_FF = 5504
CONV_W = 3
PLE_DIM = 256
N_BUCKETS = 32
MAX_DISTANCE = 128
NORM_EPS = 1e-6
SUBLN_EPS = 1e-5
LOG2E = math.log2(math.e)
QK_SCALE = QK_DIM ** -0.5 * LOG2E

LANES = 128
SUBLANES = 8
MXU_WIDTH = 256
VMEM_LIMIT = 56 * 1024 * 1024

GROUPS_PER_BLOCK = LANES // SSM_GROUP
N_GROUP_BLOCKS = N_SSM_GROUPS // GROUPS_PER_BLOCK
STATE_LANES = GROUPS_PER_BLOCK * SSM_STATE

D_FF_PAD = 5632
FFN_TF = 512
ROW_TILE = 512
ATTN_TQ = 512
ATTN_HEADS_PER_STEP = 2
SSM_ROWS = 256
PAGES_PER_STEP = 8
CONV_HALO = 16


def _params(sem):
    return pltpu.CompilerParams(dimension_semantics=sem, vmem_limit_bytes=VMEM_LIMIT)


def _rms(x, g, eps):
    return x * lax.rsqrt(jnp.mean(x * x, axis=-1, keepdims=True) + eps) * g


def _in_proj_kernel(x_ref, g_ref, w_ref, wkt_ref, *refs, k_transposed, n_cast, has_extra):
    n_in = n_cast + has_extra
    outs = refs[n_in:n_in + 4]
    for src, dst in zip(refs[:n_cast], refs[n_in + 4:n_in + 4 + n_cast]):
        dst[...] = src[...].astype(BF16)

    def project(x, q_ref, k_ref, v_ref, u_ref, transposed):
        xn = _rms(x, g_ref[...], NORM_EPS).astype(BF16)

        def proj(n):
            return jnp.dot(xn, w_ref[:, n * ATT_W:(n + 1) * ATT_W], preferred_element_type=F32)

        q_ref[...] = (proj(0) * QK_SCALE).astype(BF16)
        if transposed:
            k_ref[0] = lax.dot_general(wkt_ref[...], xn, (((1,), (1,)), ((), ())),
                                       preferred_element_type=F32)
        else:
            k_ref[...] = proj(1)
        v_ref[...] = proj(2)
        u_ref[...] = proj(3)

    project(x_ref[...], *outs, k_transposed)
    if has_extra:
        @pl.when(pl.program_id(0) == 0)
        def _():
            project(refs[n_cast][...], *refs[n_in + 4 + n_cast:], False)


def _resident(shape):
    return pl.BlockSpec(shape, lambda *_: (0,) * len(shape), pipeline_mode=pl.Buffered(1))


def _in_proj(x, g, w, wkt, tm, rows_per_seq, k_transposed, cast=(), extra=None):
    m = x.shape[0]
    steps = m // tm
    extra_in, extra_specs, extra_shapes = [], [], []
    if extra is not None:
        me = extra.shape[0]
        extra_in = [_resident((me, D_MODEL))]
        extra_specs = [pl.BlockSpec((me, ATT_W), lambda i: (0, 0))] * 4
        extra_shapes = [jax.ShapeDtypeStruct((me, ATT_W), BF16)] + [jax.ShapeDtypeStruct((me, ATT_W), F32)] * 3
    cast_specs = [pl.BlockSpec((c.shape[0] // steps, c.shape[1]), lambda i: (i, 0)) for c in cast]
    assert all(c.shape[0] % (steps * 2 * SUBLANES) == 0 for c in cast)
    out_spec = pl.BlockSpec((tm, ATT_W), lambda i: (i, 0))
    if k_transposed:
        tiles = rows_per_seq // tm
        k_spec = pl.BlockSpec((1, ATT_W, tm), lambda i: (i // tiles, 0, i % tiles))
        k_shape = jax.ShapeDtypeStruct((m // rows_per_seq, ATT_W, rows_per_seq), F32)
    else:
        k_spec = out_spec
        k_shape = jax.ShapeDtypeStruct((m, ATT_W), F32)
    return pl.pallas_call(
        functools.partial(_in_proj_kernel, k_transposed=k_transposed, n_cast=len(cast),
                          has_extra=int(extra is not None)),
        grid=(steps,),
        in_specs=[
            pl.BlockSpec((tm, D_MODEL), lambda i: (i, 0)),
            _resident((1, D_MODEL)),
            _resident((D_MODEL, 4 * ATT_W)),
            _resident((ATT_W, D_MODEL)),
        ] + cast_specs + extra_in,
        out_specs=[out_spec, k_spec, out_spec, out_spec] + cast_specs + extra_specs,
        out_shape=[
            jax.ShapeDtypeStruct((m, ATT_W), BF16),
            k_shape,
            jax.ShapeDtypeStruct((m, ATT_W), F32),
            jax.ShapeDtypeStruct((m, SSM_W), F32),
        ] + [jax.ShapeDtypeStruct(c.shape, BF16) for c in cast] + extra_shapes,
        compiler_params=_params(("arbitrary",)),
        name="in_proj",
    )(x, g, w, wkt, *cast, *([] if extra is None else [extra]))


def _lambda(lam_ref, lam_init):
    s1 = jnp.sum(lam_ref[0:1, :] * lam_ref[1:2, :], axis=-1, keepdims=True)
    s2 = jnp.sum(lam_ref[2:3, :] * lam_ref[3:4, :], axis=-1, keepdims=True)
    return jnp.exp(s1) - jnp.exp(s2) + lam_init


def _online_softmax_step(s, v_blk, m_ref, l_ref, acc_ref):
    m_prev = m_ref[...]
    m_new = jnp.maximum(m_prev, jnp.max(s, axis=-1, keepdims=True))
    alpha = jnp.exp2(m_prev - m_new)
    p = jnp.exp2(s - jnp.concatenate([m_new] * (s.shape[1] // LANES), axis=1))
    l_ref[...] = alpha * l_ref[...] + jnp.sum(p, axis=-1, keepdims=True)
    acc_ref[...] = alpha * acc_ref[...] + jnp.dot(
        p.astype(BF16), v_blk, preferred_element_type=F32)
    m_ref[...] = m_new


def _attn_prompt_kernel(lam_ref, q_ref, k_ref, v_ref, band_ref, g_ref, wup_ref, wdn_ref,
                        o_ref, wg_ref, wv_ref, wdnb_ref,
                        kb_ref, vb_ref, qs_ref, bias_ref, m_ref, l_ref, acc_ref,
                        *, tq, nh, lam_init, n_down_tiles):
    _cast_up_tile(wup_ref, wg_ref, wv_ref)
    i = pl.program_id(2)
    step = (pl.program_id(0) * pl.num_programs(1) + pl.program_id(1)) * pl.num_programs(2) + i

    @pl.when(step < n_down_tiles)
    def _():
        _cast_down_tile(wdn_ref, wdnb_ref)

    head_cols = [slice(h * V_HEAD, (h + 1) * V_HEAD) for h in range(nh)]

    @pl.when(i == 0)
    def _():
        kb_ref[...] = k_ref[0].astype(BF16)
        vb_ref[...] = v_ref[0].astype(BF16)
        w = band_ref.shape[1]
        zero = jnp.zeros((w, w), F32)
        masked = jnp.full((w, w), -jnp.inf, F32)
        n = tq // w
        for h in range(nh):
            left, right = band_ref[h, :, 0:w], band_ref[h, :, w:2 * w]
            for r in range(n):
                for c in range(n):
                    rows = slice(r * w, (r + 1) * w)
                    bias_ref[h, rows, c * w:(c + 1) * w] = left if (r == 0 and c == n - 1) else zero
                    bias_ref[h, rows, tq + c * w:tq + (c + 1) * w] = (
                        right if c == r else left if c == r - 1 else masked if c > r else zero)

    for h in range(nh):
        q = q_ref[0, :, head_cols[h]]
        lane = lax.broadcasted_iota(jnp.int32, q.shape, 1)
        zero = jnp.zeros_like(q)
        qs_ref[h, 0:tq, :] = jnp.where(lane < QK_DIM, q, zero)
        qs_ref[h, tq:2 * tq, :] = jnp.where(lane >= QK_DIM, q, zero)
    m_ref[...] = jnp.full(m_ref.shape, -jnp.inf, F32)
    l_ref[...] = jnp.zeros(l_ref.shape, F32)
    acc_ref[...] = jnp.zeros(acc_ref.shape, F32)

    def update(j, width, biased):
        off = pl.multiple_of(j * tq, tq)
        for h in range(nh):
            rows = slice(h * 2 * QK_DIM, (h + 1) * 2 * QK_DIM)
            s = jnp.dot(qs_ref[h], kb_ref[rows, pl.ds(off, width)], preferred_element_type=F32)
            if biased:
                bias = bias_ref[h, :, 2 * tq - width:2 * tq]
                s = (s.reshape(2, tq, width) + bias[None]).reshape(2 * tq, width)
            _online_softmax_step(s, vb_ref[pl.ds(off, width), head_cols[h]],
                                 m_ref.at[h], l_ref.at[h], acc_ref.at[h])

    n_far = jnp.maximum(i - 1, 0)

    def far_body(t, carry):
        update(2 * t, 2 * tq, False)
        return carry

    lax.fori_loop(0, n_far // 2, far_body, 0)

    @pl.when(n_far % 2 == 1)
    def _():
        update(n_far - 1, tq, False)

    @pl.when(i >= 1)
    def _():
        update(i - 1, 2 * tq, True)

    @pl.when(i == 0)
    def _():
        update(0, tq, True)

    lam = _lambda(lam_ref, lam_init)
    for h in range(nh):
        inv_l = 1.0 / l_ref[h]
        o = acc_ref[h, 0:tq, :] * inv_l[0:tq] - lam * (acc_ref[h, tq:2 * tq, :] * inv_l[tq:2 * tq])
        o_ref[0, :, head_cols[h]] = (_rms(o, g_ref[h], SUBLN_EPS) * (1.0 - lam_init)).astype(o_ref.dtype)


def _attn_prompt(q, k, v, band, lam_rows, subln_g, w_up, w_down, tq, nh, lam_init):
    b, t, _ = q.shape
    w = band.shape[1]
    assert tq % w == 0 and band.shape[2] == 2 * w and N_HEADS % nh == 0
    gw = nh * V_HEAD
    n_hg, n_i = N_HEADS // nh, t // tq
    wr = D_MODEL // (b * n_hg * n_i)
    assert wr * b * n_hg * n_i == D_MODEL and wr % (2 * SUBLANES) == 0
    n_down = D_MODEL // LANES
    assert n_down <= b * n_hg * n_i
    kern = functools.partial(_attn_prompt_kernel, tq=tq, nh=nh, lam_init=lam_init, n_down_tiles=n_down)
    w_index = lambda bi, h, i: ((bi * n_hg + h) * n_i + i, 0)
    d_index = lambda bi, h, i: (0, jnp.minimum((bi * n_hg + h) * n_i + i, n_down - 1))
    w_out = jax.ShapeDtypeStruct((D_MODEL, D_FF_PAD), BF16)
    return pl.pallas_call(
        kern,
        grid=(b, n_hg, n_i),
        in_specs=[
            pl.BlockSpec((4, QK_DIM), lambda bi, h, i: (0, 0)),
            pl.BlockSpec((1, tq, gw), lambda bi, h, i: (bi, i, h)),
            pl.BlockSpec((1, gw, t), lambda bi, h, i: (bi, h, 0)),
            pl.BlockSpec((1, t, gw), lambda bi, h, i: (bi, 0, h)),
            pl.BlockSpec((nh, w, 2 * w), lambda bi, h, i: (h, 0, 0)),
            pl.BlockSpec((nh, 1, V_HEAD), lambda bi, h, i: (h, 0, 0)),
            pl.BlockSpec((wr, 2 * D_FF), w_index),
            pl.BlockSpec((D_FF, LANES), d_index),
        ],
        out_specs=[pl.BlockSpec((1, tq, gw), lambda bi, h, i: (bi, i, h)),
                   pl.BlockSpec((wr, D_FF_PAD), w_index),
                   pl.BlockSpec((wr, D_FF_PAD), w_index),
                   pl.BlockSpec((D_FF_PAD, LANES), d_index)],
        out_shape=[jax.ShapeDtypeStruct((b, t, ATT_W), BF16), w_out, w_out,
                   jax.ShapeDtypeStruct((D_FF_PAD, D_MODEL), BF16)],
        scratch_shapes=[
            pltpu.VMEM((gw, t), BF16),
            pltpu.VMEM((t, gw), BF16),
            pltpu.VMEM((nh, 2 * tq, V_HEAD), BF16),
            pltpu.VMEM((nh, tq, 2 * tq), F32),
            pltpu.VMEM((nh, 2 * tq, LANES), F32),
            pltpu.VMEM((nh, 2 * tq, LANES), F32),
            pltpu.VMEM((nh, 2 * tq, V_HEAD), F32),
        ],
        compiler_params=_params(("arbitrary", "arbitrary", "arbitrary")),
        name="attn_prompt",
    )(lam_rows, q, k, v, band, subln_g, w_up, w_down)


def _attn_sample_parts(step, n_steps, lam_ref, qbd_ref, k_refs, v_refs, bias_ref, kn_ref, vn_ref, bn_ref,
                       g_ref, o_ref, m_ref, l_ref, acc_ref, *, n_new, lam_init):
    def init():
        @pl.when(step == 0)
        def _():
            m_ref[...] = jnp.full(m_ref.shape, -jnp.inf, F32)
            l_ref[...] = jnp.zeros(l_ref.shape, F32)
            acc_ref[...] = jnp.zeros(acc_ref.shape, F32)

    qbd = qbd_ref[0]
    rows = 2 * n_new

    def update(k_pages, v_page_refs, bias_last):
        parts = [jnp.dot(qbd, kp.astype(BF16), preferred_element_type=F32) for kp in k_pages]
        parts[-1] = parts[-1] + bias_last
        s = jnp.concatenate(parts, axis=1)
        m_prev = m_ref[...]
        m_new = jnp.maximum(m_prev, jnp.max(s, axis=-1, keepdims=True))
        alpha = jnp.exp2(m_prev - m_new)
        p = jnp.exp2(s - jnp.concatenate([m_new] * len(parts), axis=1))
        l_ref[...] = alpha * l_ref[...] + jnp.sum(p, axis=-1, keepdims=True)
        m_ref[...] = m_new
        pb = p.astype(BF16)
        for h in range(N_HEADS):
            v_h = jnp.concatenate(
                [vr[0, pl.ds(h, PAGE_SIZE, stride=N_HEADS), :].astype(BF16) for vr in v_page_refs], axis=0)
            r = slice(h * rows, (h + 1) * rows)
            acc_ref[r, :] = alpha[r] * acc_ref[r, :] + jnp.dot(pb[r], v_h, preferred_element_type=F32)

    is_last = step == n_steps - 1

    def main():
        bias_last = jnp.where(is_last, bias_ref[...], jnp.zeros(bias_ref.shape, F32))
        update([kr[0] for kr in k_refs], v_refs, bias_last)

    def finish():
        @pl.when(is_last)
        def _():
            update([kn_ref[0]], [vn_ref], bn_ref[...])
            lam = _lambda(lam_ref, lam_init)
            acc = acc_ref[...] * (1.0 / l_ref[...])
            for h in range(N_HEADS):
                blk = acc[h * rows:(h + 1) * rows]
                o = blk[0:n_new] - lam * blk[n_new:rows]
                o = _rms(o, g_ref[h:h + 1, :], SUBLN_EPS) * (1.0 - lam_init)
                o_ref[0, :, h * V_HEAD:(h + 1) * V_HEAD] = o

    return init, main, finish


STATE_COLS = 2 * STATE_LANES // LANES


def _state_rows(g, tc, col):
    return pl.ds(g * tc * STATE_COLS + col, tc, stride=STATE_COLS)


def _ssm_kernel(u_ref, h0_ref, wb_ref, cb_ref, lam_ref, d_ref, gw_ref, gbias_ref, og_ref,
                o_ref, hf_ref, hb_ref, gbuf_ref, carry_ref, *, tc, chains):
    c = pl.program_id(1)

    @pl.when(c == 0)
    def _():
        carry_ref[...] = h0_ref[0]

    for g in range(N_GROUP_BLOCKS):
        u = u_ref[0, :, g * LANES:(g + 1) * LANES]
        bu = jnp.dot(u.astype(BF16), wb_ref[g], preferred_element_type=F32)
        for k in range(STATE_COLS):
            hb_ref[_state_rows(g, tc, k), :] = bu[:, k * LANES:(k + 1) * LANES]

    coef = [(lam_ref[g, 0], lam_ref[g, 1]) for g in range(N_GROUP_BLOCKS)]

    def step(g, row, h):
        start = (g * tc + row) * STATE_COLS
        if not isinstance(start, int):
            start = pl.multiple_of(start, STATE_COLS)
        rows = pl.ds(start, STATE_COLS)
        a, b = coef[g]
        h = a * h + b * pltpu.roll(h, STATE_COLS // 2, 0) + hb_ref[rows, :]
        hb_ref[rows, :] = h
        return h

    if chains == 1:
        def body(t, hs):
            return tuple(step(g, t, hs[g]) for g in range(N_GROUP_BLOCKS))

        hs = lax.fori_loop(0, tc, body, tuple(carry_ref[g, 0] for g in range(N_GROUP_BLOCKS)),
                           unroll=SUBLANES)
        for g in range(N_GROUP_BLOCKS):
            carry_ref[g, 0] = hs[g]
    else:
        for g in range(N_GROUP_BLOCKS):
            hs = [carry_ref[g, b] for b in range(chains)]
            for row in range(tc):
                hs[row % chains] = step(g, row, hs[row % chains])
            for b in range(chains):
                carry_ref[g, b] = hs[b]

    ssq = jnp.zeros((tc, 1), F32)
    for g in range(N_GROUP_BLOCKS):
        cols = slice(g * LANES, (g + 1) * LANES)
        h = jnp.concatenate([hb_ref[_state_rows(g, tc, k), :] for k in range(STATE_COLS)], axis=1)
        y = jnp.dot(h.astype(BF16), cb_ref[g], preferred_element_type=F32)
        y = y + d_ref[:, cols] * u_ref[0, :, cols]
        a = jax.nn.gelu(y)
        gate = jnp.dot(a.astype(BF16), gw_ref[g], preferred_element_type=F32) + gbias_ref[:, cols]
        a = a * jax.nn.sigmoid(gate)
        gbuf_ref[:, cols] = a
        ssq = ssq + jnp.sum(a * a, axis=-1, keepdims=True)
    inv = lax.rsqrt(ssq * (1.0 / SSM_W) + NORM_EPS)
    o_ref[0] = (gbuf_ref[...] * inv * og_ref[...]).astype(o_ref.dtype)
    hf_ref[0] = carry_ref[...]


def _ssm(u, h0, wb, cb, lam, d, gw, gbias, og, tc, chains, out_dtype):
    ns, t, _ = u.shape
    kern = functools.partial(_ssm_kernel, tc=tc, chains=chains)
    full = lambda *shape: pl.BlockSpec(shape, lambda s, c: (0,) * len(shape))
    state_shape = (N_GROUP_BLOCKS, chains, STATE_COLS, LANES)
    state_spec = pl.BlockSpec((1,) + state_shape, lambda s, c: (s, 0, 0, 0, 0))
    return pl.pallas_call(
        kern,
        grid=(ns, t // tc),
        in_specs=[
            pl.BlockSpec((1, tc, SSM_W), lambda s, c: (s, c, 0)),
            state_spec,
            full(N_GROUP_BLOCKS, LANES, 2 * STATE_LANES),
            full(N_GROUP_BLOCKS, 2 * STATE_LANES, LANES),
            full(N_GROUP_BLOCKS, 2, STATE_COLS, LANES),
            full(1, SSM_W),
            full(N_GROUP_BLOCKS, LANES, LANES),
            full(1, SSM_W),
            full(1, SSM_W),
        ],
        out_specs=[pl.BlockSpec((1, tc, SSM_W), lambda s, c: (s, c, 0)), state_spec],
        out_shape=[
            jax.ShapeDtypeStruct((ns, t, SSM_W), out_dtype),
            jax.ShapeDtypeStruct((ns,) + state_shape, F32),
        ],
        scratch_shapes=[
            pltpu.VMEM((N_GROUP_BLOCKS * tc * STATE_COLS, LANES), F32),
            pltpu.VMEM((tc, SSM_W), F32),
            pltpu.VMEM(state_shape, F32),
        ],
        compiler_params=_params(("arbitrary", "arbitrary")),
        name="ssm_chains%d" % chains,
    )(u, h0, wb, cb, lam, d, gw, gbias, og)


def _out_proj_kernel(x_ref, oa_ref, os_ref, w_ref, *refs):
    o_ref = refs[-1] if len(refs) == 1 else refs[1]
    acc = jnp.dot(oa_ref[...].astype(BF16), w_ref[0], preferred_element_type=F32)
    acc = acc + jnp.dot(os_ref[...].astype(BF16), w_ref[1], preferred_element_type=F32)
    o_ref[...] = x_ref[...] + acc
    if len(refs) == 3:
        _cast_down_tile(refs[0], refs[2])


def _out_proj(x, o_att, o_ssm, w, tm, w_down=None):
    m = x.shape[0]
    steps = m // tm
    in_specs = [
        pl.BlockSpec((tm, D_MODEL), lambda i: (i, 0)),
        pl.BlockSpec((tm, ATT_W), lambda i: (i, 0)),
        pl.BlockSpec((tm, SSM_W), lambda i: (i, 0)),
        pl.BlockSpec((2, ATT_W, D_MODEL), lambda i: (0, 0, 0)),
    ]
    out_specs = [pl.BlockSpec((tm, D_MODEL), lambda i: (i, 0))]
    out_shape = [jax.ShapeDtypeStruct((m, D_MODEL), F32)]
    args = [x, o_att, o_ssm, w]
    if w_down is not None:
        tc = D_MODEL // steps
        in_specs.append(pl.BlockSpec((D_FF, tc), lambda i: (0, i)))
        out_specs.append(pl.BlockSpec((D_FF_PAD, tc), lambda i: (0, i)))
        out_shape.append(jax.ShapeDtypeStruct((D_FF_PAD, D_MODEL), BF16))
        args.append(w_down)
    out = pl.pallas_call(
        _out_proj_kernel,
        grid=(steps,),
        in_specs=in_specs,
        out_specs=out_specs,
        out_shape=out_shape,
        compiler_params=_params(("arbitrary",)),
        name="out_proj",
    )(*args)
    return out if w_down is not None else out[0]


N_FFN_IN = 8


def _ffn_parts(i, f, h_ref, g_ref, prev_ref, wg_ref, wv_ref, cw_ref, cb_ref, wdn_ref, o_ref, cn_ref,
               fn_ref, acc_ref, halo_ref, *, tm, shift, tiles_per_seq, nf):
    def prologue():
        @pl.when(f == 0)
        def _():
            fn_ref[...] = _rms(h_ref[...], g_ref[...], NORM_EPS).astype(BF16)
            acc_ref[...] = jnp.zeros(acc_ref.shape, F32)

        @pl.when((i % tiles_per_seq) == 0)
        def _():
            halo_ref[f] = prev_ref[0]

    def main():
        fn = fn_ref[...]
        down = None
        for c0 in range(0, FFN_TF, MXU_WIDTH):
            cols = slice(c0, c0 + MXU_WIDTH)
            gate = jnp.dot(fn, wg_ref[:, cols], preferred_element_type=F32)
            val = jnp.dot(fn, wv_ref[:, cols], preferred_element_type=F32)
            gext = jnp.concatenate([halo_ref[f, :, cols], gate], axis=0)
            g1 = gext[CONV_HALO - shift:CONV_HALO - shift + tm]
            g2 = gext[CONV_HALO - 2 * shift:CONV_HALO - 2 * shift + tm]
            conv = cb_ref[:, cols] + (cw_ref[0:1, cols] * g2 + cw_ref[1:2, cols] * g1 + cw_ref[2:3, cols] * gate)
            hact = (conv * jax.nn.sigmoid(conv)) * val
            part = jnp.dot(hact.astype(BF16), wdn_ref[cols, :], preferred_element_type=F32)
            down = part if down is None else down + part
            halo_ref[f, :, cols] = gate[tm - CONV_HALO:tm]
            cn_ref[0, :, cols] = gate[tm - 2 * shift:tm]
        acc_ref[...] += down

    def epilogue():
        @pl.when(f == nf - 1)
        def _():
            o_ref[...] = h_ref[...] + acc_ref[...]

    return prologue, main, epilogue


def _ffn_kernel(*refs, **static):
    prologue, main, epilogue = _ffn_parts(pl.program_id(0), pl.program_id(1), *refs, **static)
    prologue()
    main()
    epilogue()


def _ffn_attn_kernel(pt_ref, *refs, n_pages_step, steps_per_seq, n_seq, n_new, lam_init, nf, **static):
    del pt_ref
    g = n_pages_step
    ffn_in = refs[:N_FFN_IN]
    lam_ref, qbd_ref = refs[N_FFN_IN:N_FFN_IN + 2]
    k_refs = refs[N_FFN_IN + 2:N_FFN_IN + 2 + g]
    v_refs = refs[N_FFN_IN + 2 + g:N_FFN_IN + 2 + 2 * g]
    rest = refs[N_FFN_IN + 2 + 2 * g:]
    bias_ref, kn_ref, vn_ref, bn_ref, sg_ref, o_ref, cn_ref, oa_ref = rest[:8]
    fn_ref, acc_ref, halo_ref, am_ref, al_ref, aacc_ref = rest[8:]
    i = pl.program_id(0)
    f = pl.program_id(1)
    s = i * nf + f
    active = s < n_seq * steps_per_seq
    prologue, ffn_main, epilogue = _ffn_parts(i, f, *ffn_in, o_ref, cn_ref, fn_ref, acc_ref, halo_ref,
                                              nf=nf, **static)
    init, attn_main, finish = _attn_sample_parts(
        s % steps_per_seq, steps_per_seq, lam_ref, qbd_ref, k_refs, v_refs, bias_ref, kn_ref, vn_ref,
        bn_ref, sg_ref, oa_ref, am_ref, al_ref, aacc_ref, n_new=n_new, lam_init=lam_init)
    prologue()

    @pl.when(active)
    def _():
        init()

    @pl.when(active)
    def _():
        ffn_main()
        attn_main()

    @pl.when(jnp.logical_not(active))
    def _():
        ffn_main()

    epilogue()

    @pl.when(active)
    def _():
        finish()


def _ffn_specs(m, tm, shift, tiles_per_seq, where):
    nf = D_FF_PAD // FFN_TF

    def spec(shape, index):
        return pl.BlockSpec(shape, lambda *ids: index(*where(*ids)))

    in_specs = [
        spec((tm, D_MODEL), lambda i, f: (i, 0)),
        spec((1, D_MODEL), lambda i, f: (0, 0)),
        spec((1, CONV_HALO, FFN_TF), lambda i, f: (i // tiles_per_seq, 0, f)),
        spec((D_MODEL, FFN_TF), lambda i, f: (0, f)),
        spec((D_MODEL, FFN_TF), lambda i, f: (0, f)),
        spec((CONV_W, FFN_TF), lambda i, f: (0, f)),
        spec((1, FFN_TF), lambda i, f: (0, f)),
        spec((FFN_TF, D_MODEL), lambda i, f: (f, 0)),
    ]
    assert len(in_specs) == N_FFN_IN
    out_specs = [
        spec((tm, D_MODEL), lambda i, f: (i, 0)),
        spec((1, 2 * shift, FFN_TF), lambda i, f: (i, 0, f)),
    ]
    out_shape = [
        jax.ShapeDtypeStruct((m, D_MODEL), F32),
        jax.ShapeDtypeStruct((m // tm, 2 * shift, D_FF_PAD), F32),
    ]
    scratch = [
        pltpu.VMEM((tm, D_MODEL), BF16),
        pltpu.VMEM((tm, D_MODEL), F32),
        pltpu.VMEM((nf, CONV_HALO, FFN_TF), F32),
    ]
    return in_specs, out_specs, out_shape, scratch


def _ffn(h, g, prev, wg, wv, cw, cb, wdn, tm, shift, tiles_per_seq, attn=None):
    m = h.shape[0]
    nf = D_FF_PAD // FFN_TF
    static = dict(tm=tm, shift=shift, tiles_per_seq=tiles_per_seq, nf=nf)
    in_specs, out_specs, out_shape, scratch = _ffn_specs(m, tm, shift, tiles_per_seq,
                                                         lambda i, f, *_: (i, f))
    args = (h, g, prev, wg, wv, cw, cb, wdn)
    grid = (m // tm, nf)
    if attn is None:
        return pl.pallas_call(
            functools.partial(_ffn_kernel, **static),
            grid=grid, in_specs=in_specs, out_specs=out_specs, out_shape=out_shape,
            scratch_shapes=scratch,
            compiler_params=_params(("arbitrary", "arbitrary")),
            name="conv_ffn",
        )(*args)

    (page_table, qbd, k_pool, v_pool, bias_last, k_new, v_new, bias_new, lam_rows, subln_g,
     n_pages_step, lam_init) = attn
    n_seq, n_pages = page_table.shape
    rows = qbd.shape[1]
    n_new = rows // (2 * N_HEADS)
    steps_per_seq = n_pages // n_pages_step
    last = n_seq * steps_per_seq - 1
    assert grid[0] * grid[1] > last

    def seq_of(i, f):
        return jnp.minimum(i * nf + f, last) // steps_per_seq

    def page_spec(p, shape):
        def index(i, f, pt):
            s = jnp.minimum(i * nf + f, last)
            return (pt[s // steps_per_seq, (s % steps_per_seq) * n_pages_step + p], 0, 0)
        return pl.BlockSpec((1,) + shape, index)

    in_specs += (
        [pl.BlockSpec((4, QK_DIM), lambda i, f, pt: (0, 0)),
         pl.BlockSpec((1, rows, ATT_W), lambda i, f, pt: (seq_of(i, f), 0, 0))]
        + [page_spec(p, (ATT_W, PAGE_SIZE)) for p in range(n_pages_step)]
        + [page_spec(p, (PAGE_SIZE * N_HEADS, V_HEAD)) for p in range(n_pages_step)]
        + [pl.BlockSpec((rows, PAGE_SIZE), lambda i, f, pt: (0, 0)),
           pl.BlockSpec((1, ATT_W, PAGE_SIZE), lambda i, f, pt: (seq_of(i, f), 0, 0)),
           pl.BlockSpec((1, PAGE_SIZE * N_HEADS, V_HEAD), lambda i, f, pt: (seq_of(i, f), 0, 0)),
           pl.BlockSpec((rows, PAGE_SIZE), lambda i, f, pt: (0, 0)),
           pl.BlockSpec((N_HEADS, V_HEAD), lambda i, f, pt: (0, 0))])
    out_specs.append(pl.BlockSpec((1, n_new, ATT_W), lambda i, f, pt: (seq_of(i, f), 0, 0)))
    out_shape.append(jax.ShapeDtypeStruct((n_seq, n_new, ATT_W), F32))
    scratch += [pltpu.VMEM((rows, LANES), F32), pltpu.VMEM((rows, LANES), F32),
                pltpu.VMEM((rows, V_HEAD), F32)]
    kern = functools.partial(_ffn_attn_kernel, n_pages_step=n_pages_step, steps_per_seq=steps_per_seq,
                             n_seq=n_seq, n_new=n_new, lam_init=lam_init, **static)
    return pl.pallas_call(
        kern,
        grid_spec=pltpu.PrefetchScalarGridSpec(
            num_scalar_prefetch=1, grid=grid, in_specs=in_specs, out_specs=out_specs,
            scratch_shapes=scratch),
        out_shape=out_shape,
        compiler_params=_params(("arbitrary", "arbitrary")),
        name="conv_ffn_attn",
    )(page_table, *args, lam_rows, qbd, *([k_pool] * n_pages_step), *([v_pool] * n_pages_step),
      bias_last, k_new, v_new, bias_new, subln_g)


N_PLE_IN = 6


def _ple_kernel(h_ref, gp_ref, pe_ref, wg_ref, wp_ref, gf_ref, *refs, side_ffn):
    if side_ffn is not None:
        step = pl.program_id(0)
        ffn_in, (o_ref, o2_ref, cn_ref), scratch = refs[:N_FFN_IN], refs[N_FFN_IN:N_FFN_IN + 3], refs[N_FFN_IN + 3:]
        prologue, main, epilogue = _ffn_parts(step * 0, step, *ffn_in, o2_ref, cn_ref, *scratch, **side_ffn)

        @pl.when(step < side_ffn["nf"])
        def _():
            prologue()
            main()
            epilogue()
    else:
        (o_ref,) = refs
    h = h_ref[...]
    hn = _rms(h, gp_ref[...], NORM_EPS).astype(BF16)
    gate = jax.nn.sigmoid(jnp.dot(hn, wg_ref[...], preferred_element_type=F32))
    proj = jnp.dot(pe_ref[...].astype(BF16), wp_ref[...], preferred_element_type=F32)
    o_ref[...] = _rms(h + gate * proj, gf_ref[...], NORM_EPS)


def _ple(h, gp, pe, wg, wp, gf, tm, side_ffn=None):
    m = h.shape[0]
    steps = m // tm
    in_specs = [
        pl.BlockSpec((tm, D_MODEL), lambda i: (i, 0)),
        pl.BlockSpec((1, D_MODEL), lambda i: (0, 0)),
        pl.BlockSpec((tm, PLE_DIM), lambda i: (i, 0)),
        pl.BlockSpec((D_MODEL, D_MODEL), lambda i: (0, 0)),
        pl.BlockSpec((PLE_DIM, D_MODEL), lambda i: (0, 0)),
        pl.BlockSpec((1, D_MODEL), lambda i: (0, 0)),
    ]
    assert len(in_specs) == N_PLE_IN
    out_specs = [pl.BlockSpec((tm, D_MODEL), lambda i: (i, 0))]
    out_shape = [jax.ShapeDtypeStruct((m, D_MODEL), F32)]
    args, scratch, static = [h, gp, pe, wg, wp, gf], [], None
    if side_ffn is not None:
        *ffn_args, shift = side_ffn
        ms = ffn_args[0].shape[0]
        nf = D_FF_PAD // FFN_TF
        assert steps >= nf
        ffn_in, ffn_out, ffn_shape, scratch = _ffn_specs(
            ms, ms, shift, 1, lambda i: (0, jnp.minimum(i, nf - 1)))
        in_specs, out_specs, out_shape = in_specs + ffn_in, out_specs + ffn_out, out_shape + ffn_shape
        args += ffn_args
        static = dict(tm=ms, shift=shift, tiles_per_seq=1, nf=nf)
    out = pl.pallas_call(
        functools.partial(_ple_kernel, side_ffn=static),
        grid=(steps,),
        in_specs=in_specs,
        out_specs=out_specs,
        out_shape=out_shape,
        scratch_shapes=scratch,
        compiler_params=_params(("arbitrary",)),
        name="ple_final",
    )(*args)
    return out if side_ffn is not None else out[0]


def _t5_bucket(dist):
    max_exact = N_BUCKETS // 2
    n = jnp.maximum(dist, 0)
    nf = jnp.maximum(n, 1).astype(F32)
    large = max_exact + (jnp.log(nf / max_exact) / math.log(MAX_DISTANCE / max_exact)
                         * (N_BUCKETS - max_exact)).astype(jnp.int32)
    large = jnp.minimum(large, N_BUCKETS - 1)
    return jnp.where(n < max_exact, n, large)


def _toeplitz(w, n):
    lead = w.shape[:-1]
    flat = jnp.tile(w, (1,) * len(lead) + (n,))[..., :n * (2 * n - 1)]
    return flat.reshape(lead + (n, 2 * n - 1))[..., :n]


def _prompt_bias_tiles(table, n):
    assert n >= MAX_DISTANCE
    far = table[N_BUCKETS - 1]
    bvec = (table[_t5_bucket(jnp.arange(2 * n, dtype=jnp.int32))] - far[None, :]).T
    bvec = bvec * LOG2E
    d = jnp.arange(2 * n, dtype=jnp.int32)
    d = jnp.where(d >= n, d - 2 * n, d)
    diag = jnp.where(d[None, :] <= 0, bvec[:, jnp.abs(d)], -jnp.inf)
    sub = bvec[:, jnp.clip(n - d, 0, 2 * n - 1)]
    return jnp.concatenate([_toeplitz(sub, n), _toeplitz(diag, n)], axis=-1)


def _sample_bias(table, n_new):
    assert PAGE_SIZE >= MAX_DISTANCE
    far = table[N_BUCKETS - 1]
    t = jnp.arange(n_new, dtype=jnp.int32)
    col = jnp.arange(PAGE_SIZE, dtype=jnp.int32)

    def rows(b):
        b = jnp.transpose(b * LOG2E, (2, 0, 1))
        return jnp.broadcast_to(b[:, None], (N_HEADS, 2, n_new, PAGE_SIZE)).reshape(-1, PAGE_SIZE)

    dist_last = PAGE_SIZE + t[:, None] - col[None, :]
    dist_new = t[:, None] - col[None, :]
    b_new = jnp.where((dist_new >= 0)[..., None], table[_t5_bucket(dist_new)] - far, -jnp.inf)
    return rows(table[_t5_bucket(dist_last)] - far), rows(b_new)


def _block_diag(w):
    nb, gp, a, b = w.shape
    eye = jnp.eye(gp, dtype=w.dtype)
    out = w[:, :, :, None, :] * eye[None, :, None, :, None]
    return out.reshape(nb, gp * a, gp * b)


def _ssm_tables(a_re, a_im, b_re, b_im, c_re, c_im, log_dt):
    g, n, gc = N_SSM_GROUPS, SSM_STATE, SSM_GROUP
    dt = jnp.exp(log_dt)[:, None]
    decay = jnp.exp(a_re * dt)
    ab_re = decay * jnp.cos(a_im * dt)
    ab_im = decay * jnp.sin(a_im * dt)
    den = a_re * a_re + a_im * a_im
    num_re = ab_re - 1.0
    coef_re = (num_re * a_re + ab_im * a_im) / den
    coef_im = (ab_im * a_re - num_re * a_im) / den
    bb_re = coef_re[..., None] * b_re - coef_im[..., None] * b_im
    bb_im = coef_re[..., None] * b_im + coef_im[..., None] * b_re
    blk = lambda w: w.reshape((N_GROUP_BLOCKS, GROUPS_PER_BLOCK) + w.shape[1:])
    to_cn = lambda w: jnp.swapaxes(blk(w), 2, 3)
    wb = jnp.concatenate([_block_diag(to_cn(bb_re)), _block_diag(to_cn(bb_im))], axis=2)
    to_nc = lambda w: jnp.swapaxes(blk(w), 2, 3)
    cb = jnp.concatenate([_block_diag(to_nc(c_re)), _block_diag(to_nc(-c_im))], axis=1)
    lr = ab_re.reshape(N_GROUP_BLOCKS, STATE_COLS // 2, LANES)
    li = ab_im.reshape(N_GROUP_BLOCKS, STATE_COLS // 2, LANES)
    lam = jnp.stack([jnp.concatenate([lr, lr], axis=1), jnp.concatenate([-li, li], axis=1)], axis=1)
    return wb.astype(BF16), cb.astype(BF16), lam.astype(F32)


def _cast_in_kernel(w_ref, wb_ref, wkt_ref):
    w = w_ref[...]
    wb_ref[...] = w.astype(BF16)
    wkt_ref[...] = w[:, ATT_W:2 * ATT_W].T.astype(BF16)


def _cast_w_in(w, tr=256):
    return pl.pallas_call(
        _cast_in_kernel,
        grid=(D_MODEL // tr,),
        in_specs=[pl.BlockSpec((tr, 4 * ATT_W), lambda i: (i, 0))],
        out_specs=[pl.BlockSpec((tr, 4 * ATT_W), lambda i: (i, 0)),
                   pl.BlockSpec((ATT_W, tr), lambda i: (0, i))],
        out_shape=[jax.ShapeDtypeStruct((D_MODEL, 4 * ATT_W), BF16),
                   jax.ShapeDtypeStruct((ATT_W, D_MODEL), BF16)],
        compiler_params=_params(("arbitrary",)),
        name="cast_w_in",
    )(w)


def _cast_up_tile(w_ref, wg_ref, wv_ref):
    zeros = jnp.zeros((wg_ref.shape[0], D_FF_PAD - D_FF), BF16)
    for n, o_ref in enumerate((wg_ref, wv_ref)):
        o_ref[:, 0:D_FF] = w_ref[:, n * D_FF:(n + 1) * D_FF].astype(BF16)
        o_ref[:, D_FF:D_FF_PAD] = zeros


def _cast_down_tile(w_ref, o_ref):
    o_ref[0:D_FF, :] = w_ref[...].astype(BF16)
    o_ref[D_FF:D_FF_PAD, :] = jnp.zeros((D_FF_PAD - D_FF, o_ref.shape[1]), BF16)


def _pad_ff(w, axis):
    pad = [(0, 0)] * w.ndim
    pad[axis] = (0, D_FF_PAD - D_FF)
    return jnp.pad(w, pad)


def _state_to_tiles(re, im):
    shape = (re.shape[0], N_GROUP_BLOCKS, STATE_COLS // 2, LANES)
    return jnp.concatenate([re.reshape(shape), im.reshape(shape)], axis=2)


def _tiles_to_state(tiles):
    half = STATE_COLS // 2
    shape = (1, tiles.shape[0], N_SSM_GROUPS, SSM_STATE)
    return tiles[:, :, :half].reshape(shape), tiles[:, :, half:].reshape(shape)


def kernel(x_prompt, x_sample, cache_k, cache_v, state_ssm_re, state_ssm_im, state_conv, page_table, p_prompt, p_sample, rel_bias, norm_mix_g, w_in, lambda_q1, lambda_k1, lambda_q2, lambda_k2, attn_subln_g, ssm_a_re, ssm_a_im, ssm_b_re, ssm_b_im, ssm_c_re, ssm_c_im, ssm_d, ssm_log_dt, ssm_glu_w, ssm_glu_b, ssm_out_g, w_out, norm_ffn_g, ffn_w_up, ffn_conv_w, ffn_conv_b, ffn_w_down, norm_ple_g, ple_w_gate, ple_w_proj, norm_final_g):
    assert DEPTH == 1
    li = 0
    lam_init = 0.8 - 0.6 * math.exp(-0.3 * li)
    nb, t, _ = x_prompt.shape
    db, ds, _ = x_sample.shape
    assert db == SUBLANES
    m_p = nb * t
    m_s = db * ds

    row = lambda v: v.reshape(1, -1).astype(F32)
    w_in_b, w_kt_b = _cast_w_in(w_in[li])
    conv_w = _pad_ff(ffn_conv_w[li], 1)
    conv_b = _pad_ff(ffn_conv_b[li].reshape(1, D_FF), 1)
    w_proj_b = ple_w_proj[li].astype(BF16)
    lam_rows = jnp.stack([lambda_q1[li], lambda_k1[li], lambda_q2[li], lambda_k2[li]]).astype(F32)
    subln_g = attn_subln_g[li].astype(F32)
    wb, cb, tbl = _ssm_tables(ssm_a_re[li], ssm_a_im[li], ssm_b_re[li], ssm_b_im[li],
                              ssm_c_re[li], ssm_c_im[li], ssm_log_dt[li])
    glu_w = _block_diag(ssm_glu_w[li].reshape(N_GROUP_BLOCKS, GROUPS_PER_BLOCK, SSM_GROUP, SSM_GROUP)).astype(BF16)
    ssm_args = (wb, cb, tbl, row(ssm_d[li]), glu_w, row(ssm_glu_b[li]), row(ssm_out_g[li]))

    tmaj = lambda a: jnp.swapaxes(a, 0, 1).reshape(m_s, -1)
    bmaj = lambda a: jnp.swapaxes(a.reshape(ds, db, -1), 0, 1)
    xs = tmaj(x_sample)

    xp = x_prompt.reshape(m_p, D_MODEL)
    q_p, kt_p, v_p, u_p, w_out_b, w_gate_b, q_s, k_s, v_s, u_s = _in_proj(
        xp, row(norm_mix_g[li]), w_in_b, w_kt_b, tm=ROW_TILE, rows_per_seq=t, k_transposed=True,
        cast=(w_out[li], ple_w_gate[li]), extra=xs)
    w_out_b = w_out_b.reshape(2, ATT_W, D_MODEL)
    shp = lambda a: a.reshape(nb, t, -1)
    band = _prompt_bias_tiles(rel_bias.astype(F32), MAX_DISTANCE)
    o_att_p, w_ug_b, w_uv_b, w_dn_b = _attn_prompt(
        shp(q_p), kt_p, shp(v_p), band, lam_rows, subln_g.reshape(N_HEADS, 1, V_HEAD),
        ffn_w_up[li], ffn_w_down[li], ATTN_TQ, ATTN_HEADS_PER_STEP, lam_init)
    zero_state = jnp.zeros((nb, N_GROUP_BLOCKS, 1, STATE_COLS, LANES), F32)
    o_ssm_p, hf_p = _ssm(shp(u_p), zero_state, *ssm_args, tc=SSM_ROWS, chains=1, out_dtype=BF16)
    h1_p = _out_proj(xp, o_att_p.reshape(m_p, ATT_W), o_ssm_p.reshape(m_p, SSM_W), w_out_b, tm=ROW_TILE)

    q_b = bmaj(q_s)
    q_rep = jnp.tile(q_b, (1, 2 * N_HEADS, 1)).reshape(db, 2 * N_HEADS, ds, ATT_W)
    sel = (jnp.arange(ATT_W)[None, :] // QK_DIM) == jnp.arange(2 * N_HEADS)[:, None]
    qbd = jnp.where(sel[None, :, None, :], q_rep, jnp.zeros_like(q_rep)).reshape(db, 2 * N_HEADS * ds, ATT_W)
    pad_rows = lambda a: jnp.pad(bmaj(a), ((0, 0), (0, PAGE_SIZE - ds), (0, 0)))
    bias_past, bias_new = _sample_bias(rel_bias.astype(F32), ds)
    n_pool = cache_k.shape[1]
    k_pool_t = jnp.transpose(cache_k[li], (0, 2, 3, 4, 1)).reshape(n_pool, ATT_W, PAGE_SIZE)
    k_new_t = jnp.swapaxes(pad_rows(k_s), 1, 2)
    v_pool_r = cache_v[li].reshape(n_pool, PAGE_SIZE * N_HEADS, V_HEAD)
    v_new_r = pad_rows(v_s).reshape(db, PAGE_SIZE * N_HEADS, V_HEAD)
    tm_ffn = ROW_TILE
    zero_prev = jnp.zeros((nb, CONV_HALO, D_FF_PAD), F32)
    attn_s = (page_table, qbd, k_pool_t, v_pool_r, bias_past, k_new_t, v_new_r, bias_new,
              lam_rows, subln_g, PAGES_PER_STEP, lam_init)
    h2_p, conv_p, o_att_s = _ffn(h1_p, row(norm_ffn_g[li]), zero_prev, w_ug_b, w_uv_b, conv_w, conv_b,
                                 w_dn_b, tm=tm_ffn, shift=1, tiles_per_seq=t // tm_ffn, attn=attn_s)
    h0_s = jnp.swapaxes(_state_to_tiles(state_ssm_re[li], state_ssm_im[li]), 0, 1)[None]
    o_ssm_s, hf_s = _ssm(u_s[None], h0_s, *ssm_args, tc=m_s, chains=db, out_dtype=F32)
    h1_s = _out_proj(xs, tmaj(o_att_s), o_ssm_s[0], w_out_b, tm=m_s)
    prev_s = _pad_ff(jnp.swapaxes(state_conv[li], 0, 1).reshape(1, 2 * db, D_FF), 2)
    ffn_s = (h1_s, row(norm_ffn_g[li]), prev_s, w_ug_b, w_uv_b, conv_w, conv_b, w_dn_b, db)
    y_p, h2_s, conv_s = _ple(h2_p, row(norm_ple_g[li]), p_prompt[li].reshape(m_p, PLE_DIM), w_gate_b,
                             w_proj_b, row(norm_final_g), tm=ROW_TILE, side_ffn=ffn_s)
    y_s = _ple(h2_s, row(norm_ple_g[li]), tmaj(p_sample[li]), w_gate_b, w_proj_b,
               row(norm_final_g), tm=m_s)

    re_p, im_p = _tiles_to_state(hf_p[:, :, 0])
    re_s, im_s = _tiles_to_state(jnp.swapaxes(hf_s[0], 0, 1))
    tiles = t // tm_ffn
    conv_p = conv_p[tiles - 1::tiles][None, :, :, :D_FF]
    conv_s = jnp.swapaxes(conv_s.reshape(2, db, D_FF_PAD), 0, 1)[None, :, :, :D_FF]
    return (y_p.reshape(nb, t, D_MODEL),
            bmaj(y_s),
            jnp.transpose(kt_p.reshape(1, nb, N_HEADS, 2, QK_DIM, t), (0, 1, 5, 2, 3, 4)),
            v_p.reshape(1, nb, t, N_HEADS, V_HEAD),
            re_p, im_p, conv_p,
            bmaj(k_s).reshape(1, db, ds, N_HEADS, 2, QK_DIM),
            bmaj(v_s).reshape(1, db, ds, N_HEADS, V_HEAD),
            re_s, im_s, conv_s)
```
